```python
import math
import jax, jax.numpy as jnp
from jax import lax
import numpy as np

D_MODEL = 1024
BATCH = 16
SEQ = 2048
DEPTH = 1
DEC_BATCH = 8
DEC_SEQ = 32
PAST_LEN = 1024

CHUNK = 64
PLE_DIM = 256
EPS = 1e-6
NEG_INF = -1e30
N_HEADS_A = 16
HEAD_DIM_A = 64
ATT_WIDTH = N_HEADS_A * HEAD_DIM_A
LEFT_CHUNKS = 8
ATT_PAST = LEFT_CHUNKS * CHUNK
BAND = ATT_PAST + CHUNK
REL_CLIP = 128
SSM_EXPAND = 2
D_INNER = SSM_EXPAND * D_MODEL
SSM_HEAD_DIM = 64
N_HEADS_S = D_INNER // SSM_HEAD_DIM
N_GROUPS = 4
HEADS_PER_GROUP = N_HEADS_S // N_GROUPS
D_STATE = 128
SSM_CONV = 4
CONV_DIM = D_INNER + 2 * N_GROUPS * D_STATE
D_FF = 3 * D_MODEL
FFN_CONV = 3
IN_SIZES = (ATT_WIDTH, ATT_WIDTH, ATT_WIDTH, D_INNER, CONV_DIM, N_HEADS_S, 2 * D_MODEL)
N_IN = 3 * ATT_WIDTH + D_INNER + CONV_DIM + N_HEADS_S + 2 * D_MODEL

kernel_name = 'hybrid_streaming_band_attn_ssd_convffn_step'


def rmsnorm(x, g):
    xf = x.astype(jnp.float32)
    y = xf * lax.rsqrt(jnp.mean(xf * xf, axis=-1, keepdims=True) + EPS)
    return (y * g.astype(jnp.float32)).astype(x.dtype)


def causal_dwconv(x_ext, w, b):
    width = w.shape[0]
    l = x_ext.shape[1] - width + 1
    out = b
    for i in range(width):
        out = out + x_ext[:, i:i + l] * w[i]
    return out


def rel_bias_lookup(table, rel):
    return table[:, jnp.clip(rel, -REL_CLIP, REL_CLIP) + REL_CLIP].astype(jnp.float32)


def attend(q, k, v, bias):
    logits = jnp.einsum('bihd,bjhd->bhij', q.astype(jnp.float32), k.astype(jnp.float32)) * (HEAD_DIM_A ** -0.5)
    probs = jax.nn.softmax(logits + bias[None], axis=-1)
    return jnp.einsum('bhij,bjhd->bihd', probs, v.astype(jnp.float32))


def band_attention(q, k, v, table):
    b, l = q.shape[:2]
    nc = l // CHUNK
    pad = ((0, 0), (ATT_PAST, 0), (0, 0), (0, 0))
    kp = jnp.pad(k, pad)
    vp = jnp.pad(v, pad)
    qc = q.reshape(b, nc, CHUNK, N_HEADS_A, HEAD_DIM_A)
    rel = (ATT_PAST + jnp.arange(CHUNK))[:, None] - jnp.arange(BAND)[None, :]
    bias = rel_bias_lookup(table, rel)

    def one_chunk(c):
        qi = lax.dynamic_index_in_dim(qc, c, axis=1, keepdims=False)
        kb = lax.dynamic_slice_in_dim(kp, c * CHUNK, BAND, axis=1)
        vb = lax.dynamic_slice_in_dim(vp, c * CHUNK, BAND, axis=1)
        valid = (c * CHUNK - ATT_PAST + jnp.arange(BAND)) >= 0
        return attend(qi, kb, vb, jnp.where(valid[None, None, :], bias, NEG_INF))

    out = lax.map(one_chunk, jnp.arange(nc))
    return jnp.moveaxis(out, 0, 1).reshape(b, l, ATT_WIDTH)


def cached_attention(q, k, v, ck, cv, table):
    b, t = q.shape[:2]
    lc = ck.shape[1]
    kk = jnp.concatenate([ck.astype(k.dtype), k], axis=1)
    vv = jnp.concatenate([cv.astype(v.dtype), v], axis=1)
    rel = (lc + jnp.arange(t))[:, None] - jnp.arange(lc + t)[None, :]
    out = attend(q, kk, vv, rel_bias_lookup(table, rel))
    return out.reshape(b, t, ATT_WIDTH)


def ssd(xs, dt, a_log, bm, cm, h0):
    b, l = xs.shape[:2]
    q = min(CHUNK, l)
    nc = l // q
    a_neg = -jnp.exp(a_log.astype(jnp.float32)).reshape(N_GROUPS, HEADS_PER_GROUP)
    xg = xs.reshape(b, l, N_GROUPS, HEADS_PER_GROUP, SSM_HEAD_DIM)
    dtg = dt.reshape(b, l, N_GROUPS, HEADS_PER_GROUP)
    ag = dtg * a_neg
    tri = (jnp.arange(q)[:, None] >= jnp.arange(q)[None, :])[None, :, :, None, None]

    def chunks(t):
        return jnp.moveaxis(t.reshape((b, nc, q) + t.shape[2:]), 1, 0)

    def step(hc, inp):
        xc, dtc, ac, bc, cc = inp
        acs = jnp.cumsum(ac, axis=1)
        seg = acs[:, :, None] - acs[:, None, :]
        decay = jnp.where(tri, jnp.exp(jnp.where(tri, seg, 0.0)), 0.0)
        cb = jnp.einsum('bign,bjgn->bijg', cc, bc)
        dx = dtc[..., None] * xc
        y = jnp.einsum('bijg,bijgr,bjgrp->bigrp', cb, decay, dx)
        y = y + jnp.einsum('bign,bgrpn->bigrp', cc, hc) * jnp.exp(acs)[..., None]
        last = acs[:, -1]
        wj = jnp.exp(last[:, None] - acs)[..., None] * dx
        h_new = hc * jnp.exp(last)[..., None, None] + jnp.einsum('bjgn,bjgrp->bgrpn', bc, wj)
        return h_new, y

    h_init = h0.reshape(b, N_GROUPS, HEADS_PER_GROUP, SSM_HEAD_DIM, D_STATE)
    h_fin, ys = lax.scan(step, h_init, (chunks(xg), chunks(dtg), chunks(ag), chunks(bm), chunks(cm)))
    y = jnp.moveaxis(ys, 0, 1).reshape(b, l, N_HEADS_S, SSM_HEAD_DIM)
    return y, h_fin.reshape(b, N_HEADS_S, SSM_HEAD_DIM, D_STATE)


def layer(x, pe, kv_cache, ssm_ctx, h0, ffn_ctx, prm):
    (g_mix, w_in, b_gate, g_q, g_k, rel_bias, conv_ssm_w, conv_ssm_b, dt_bias, a_log, d_skip, g_ssm,
     w_proj_a, w_proj_b, w_out, g_ffn, w_up, ffn_conv_w, ffn_conv_b, w_down, g_ple, w_ple_gate, w_ple) = prm
    b, l, _ = x.shape
    h = rmsnorm(x, g_mix)
    proj = h @ w_in
    split_at = []
    acc = 0
    for s in IN_SIZES[:-1]:
        acc += s
        split_at.append(acc)
    q, k, v, z, xbc, dt_raw, gates = jnp.split(proj, split_at, axis=-1)

    q = rmsnorm(q.reshape(b, l, N_HEADS_A, HEAD_DIM_A), g_q)
    k = rmsnorm(k.reshape(b, l, N_HEADS_A, HEAD_DIM_A), g_k)
    v = v.reshape(b, l, N_HEADS_A, HEAD_DIM_A)
    if kv_cache is None:
        ya = band_attention(q, k, v, rel_bias)
        new_k, new_v = k[:, -ATT_PAST:], v[:, -ATT_PAST:]
    else:
        ya = cached_attention(q, k, v, kv_cache[0], kv_cache[1], rel_bias)
        new_k, new_v = k, v
    ya = ya.astype(x.dtype)

    xbc_ext = jnp.concatenate([ssm_ctx.astype(xbc.dtype), xbc], axis=1)
    new_ssm_ctx = xbc_ext[:, -(SSM_CONV - 1):]
    xbc_c = jax.nn.silu(causal_dwconv(xbc_ext, conv_ssm_w, conv_ssm_b)).astype(jnp.float32)
    xs, bm, cm = jnp.split(xbc_c, [D_INNER, D_INNER + N_GROUPS * D_STATE], axis=-1)
    xs = xs.reshape(b, l, N_HEADS_S, SSM_HEAD_DIM)
    dt = jax.nn.softplus(dt_raw.astype(jnp.float32) + dt_bias.astype(jnp.float32))
    y, h_fin = ssd(xs, dt, a_log, bm.reshape(b, l, N_GROUPS, D_STATE),
                   cm.reshape(b, l, N_GROUPS, D_STATE), h0.astype(jnp.float32))
    y = (y + d_skip.astype(jnp.float32)[:, None] * xs).reshape(b, l, D_INNER)
    yb = rmsnorm(y * jax.nn.silu(z.astype(jnp.float32)), g_ssm).astype(x.dtype)

    gate_a, gate_b = jnp.split(jax.nn.sigmoid(gates + b_gate), 2, axis=-1)
    x = x + (gate_a * (ya @ w_proj_a) + gate_b * (yb @ w_proj_b)) @ w_out

    u = rmsnorm(x, g_ffn) @ w_up
    u_ext = jnp.concatenate([ffn_ctx.astype(u.dtype), u], axis=1)
    new_ffn_ctx = u_ext[:, -(FFN_CONV - 1):]
    ug, uv = jnp.split(causal_dwconv(u_ext, ffn_conv_w, ffn_conv_b), 2, axis=-1)
    x = x + (jax.nn.gelu(ug, approximate=True) * uv) @ w_down

    x = x + jax.nn.sigmoid(rmsnorm(x, g_ple) @ w_ple_gate) * (pe.astype(x.dtype) @ w_ple)
    return x, (new_k, new_v, h_fin, new_ssm_ctx, new_ffn_ctx)


def setup_inputs(seed: int = 0) -> dict:
    key = jax.random.key(seed)
    ks = jax.random.split(key, 40)
    f32 = jnp.float32

    def nrm(k, shape, s):
        return s * jax.random.normal(k, shape, f32)

    l_att = min(ATT_PAST, PAST_LEN)
    dt0 = jnp.exp(jax.random.uniform(ks[20], (DEPTH, N_HEADS_S), f32, math.log(1e-3), math.log(1e-1)))
    dt_bias = dt0 + jnp.log(-jnp.expm1(-dt0))
    a_log = jnp.log(jax.random.uniform(ks[21], (DEPTH, N_HEADS_S), f32, 1.0, 16.0))
    return {
        'x_prompt': nrm(ks[0], (BATCH, SEQ, D_MODEL), 1.0),
        'x_sample': nrm(ks[1], (DEC_BATCH, DEC_SEQ, D_MODEL), 1.0),
        'cache_k': nrm(ks[2], (DEPTH, DEC_BATCH, l_att, N_HEADS_A, HEAD_DIM_A), 1.0),
        'cache_v': nrm(ks[3], (DEPTH, DEC_BATCH, l_att, N_HEADS_A, HEAD_DIM_A), 1.0),
        'state_ssm': nrm(ks[4], (DEPTH, DEC_BATCH, N_HEADS_S, SSM_HEAD_DIM, D_STATE), 0.1),
        'state_conv_ssm': nrm(ks[5], (DEPTH, DEC_BATCH, SSM_CONV - 1, CONV_DIM), 1.0),
        'state_conv_ffn': nrm(ks[6], (DEPTH, DEC_BATCH, FFN_CONV - 1, 2 * D_FF), 1.0),
        'p_prompt': nrm(ks[7], (DEPTH, BATCH, SEQ, PLE_DIM), 1.0),
        'p_sample': nrm(ks[8], (DEPTH, DEC_BATCH, DEC_SEQ, PLE_DIM), 1.0),
        'g_mix': 1.0 + nrm(ks[9], (DEPTH, D_MODEL), 0.05),
        'w_in': nrm(ks[10], (DEPTH, D_MODEL, N_IN), D_MODEL ** -0.5),
        'b_gate': nrm(ks[11], (DEPTH, 2 * D_MODEL), 0.01),
        'g_q': 1.0 + nrm(ks[12], (DEPTH, HEAD_DIM_A), 0.05),
        'g_k': 1.0 + nrm(ks[13], (DEPTH, HEAD_DIM_A), 0.05),
        'rel_bias': nrm(ks[14], (DEPTH, N_HEADS_A, 2 * REL_CLIP + 1), 0.1),
        'conv_ssm_w': nrm(ks[15], (DEPTH, SSM_CONV, CONV_DIM), SSM_CONV ** -0.5),
        'conv_ssm_b': nrm(ks[16], (DEPTH, CONV_DIM), 0.01),
        'dt_bias': dt_bias,
        'a_log': a_log,
        'd_skip': 1.0 + nrm(ks[17], (DEPTH, N_HEADS_S), 0.1),
        'g_ssm': 1.0 + nrm(ks[18], (DEPTH, D_INNER), 0.05),
        'w_proj_a': nrm(ks[19], (DEPTH, ATT_WIDTH, D_MODEL), ATT_WIDTH ** -0.5),
        'w_proj_b': nrm(ks[22], (DEPTH, D_INNER, D_MODEL), D_INNER ** -0.5),
        'w_out': nrm(ks[23], (DEPTH, D_MODEL, D_MODEL), D_MODEL ** -0.5),
        'g_ffn': 1.0 + nrm(ks[24], (DEPTH, D_MODEL), 0.05),
        'w_up': nrm(ks[25], (DEPTH, D_MODEL, 2 * D_FF), D_MODEL ** -0.5),
        'ffn_conv_w': nrm(ks[26], (DEPTH, FFN_CONV, 2 * D_FF), FFN_CONV ** -0.5),
        'ffn_conv_b': nrm(ks[27], (DEPTH, 2 * D_FF), 0.01),
        'w_down': nrm(ks[28], (DEPTH, D_FF, D_MODEL), D_FF ** -0.5),
        'g_ple': 1.0 + nrm(ks[29], (DEPTH, D_MODEL), 0.05),
        'w_ple_gate': nrm(ks[30], (DEPTH, D_MODEL, D_MODEL), D_MODEL ** -0.5),
        'w_ple': nrm(ks[31], (DEPTH, PLE_DIM, D_MODEL), PLE_DIM ** -0.5),
    }


def reference(x_prompt, x_sample, cache_k, cache_v, state_ssm, state_conv_ssm, state_conv_ffn,
              p_prompt, p_sample, g_mix, w_in, b_gate, g_q, g_k, rel_bias, conv_ssm_w, conv_ssm_b,
              dt_bias, a_log, d_skip, g_ssm, w_proj_a, w_proj_b, w_out, g_ffn, w_up, ffn_conv_w,
              ffn_conv_b, w_down, g_ple, w_ple_gate, w_ple):
    bp = x_prompt.shape[0]
    yp, ys = x_prompt, x_sample
    sp, ss = [], []
    for i in range(DEPTH):
        prm = (g_mix[i], w_in[i], b_gate[i], g_q[i], g_k[i], rel_bias[i], conv_ssm_w[i], conv_ssm_b[i],
               dt_bias[i], a_log[i], d_skip[i], g_ssm[i], w_proj_a[i], w_proj_b[i], w_out[i], g_ffn[i],
               w_up[i], ffn_conv_w[i], ffn_conv_b[i], w_down[i], g_ple[i], w_ple_gate[i], w_ple[i])
        yp, st_p = layer(yp, p_prompt[i], None,
                         jnp.zeros((bp, SSM_CONV - 1, CONV_DIM), yp.dtype),
                         jnp.zeros((bp, N_HEADS_S, SSM_HEAD_DIM, D_STATE), jnp.float32),
                         jnp.zeros((bp, FFN_CONV - 1, 2 * D_FF), yp.dtype), prm)
        ys, st_s = layer(ys, p_sample[i], (cache_k[i], cache_v[i]), state_conv_ssm[i], state_ssm[i],
                         state_conv_ffn[i], prm)
        sp.append(st_p)
        ss.append(st_s)

    def stack(lst, j):
        return jnp.stack([s[j] for s in lst], axis=0)

    return (yp, ys,
            stack(sp, 0), stack(sp, 1), stack(sp, 2), stack(sp, 3), stack(sp, 4),
            stack(ss, 0), stack(ss, 1), stack(ss, 2), stack(ss, 3), stack(ss, 4))
```

```python
import functools
import math

import jax
import jax.numpy as jnp
from jax import lax
from jax.experimental import pallas as pl
from jax.experimental.pallas import tpu as pltpu

F32 = jnp.float32
BF16 = jnp.bfloat16

D_MODEL = 1024
CHUNK = 64
PLE_DIM = 256
EPS = 1e-6
NEG_INF = -1e30
N_HEADS_A = 16
HEAD_DIM_A = 64
ATT_WIDTH = N_HEADS_A * HEAD_DIM_A
ATT_PAST = 8 * CHUNK
REL_CLIP = 128
D_INNER = 2 * D_MODEL
SSM_HEAD_DIM = 64
N_HEADS_S = D_INNER // SSM_HEAD_DIM
N_GROUPS = 4
GROUP_WIDTH = D_INNER // N_GROUPS
D_STATE = 128
SSM_CONV = 4
CONV_DIM = D_INNER + 2 * N_GROUPS * D_STATE
D_FF = 3 * D_MODEL
FFN_CONV = 3

LANES = 128
SUBLANES = 8
MXU_DIM = 256
VMEM_LIMIT_BYTES = 56 * 1024 * 1024

DT_PAD = LANES
HEADS_PER_MXU_TILE = MXU_DIM // SSM_HEAD_DIM
FFN_COL_BLOCK = 512

_SEG_Q = (0, ATT_WIDTH)
_SEG_K = (ATT_WIDTH, 2 * ATT_WIDTH)
_SEG_V = (2 * ATT_WIDTH, 3 * ATT_WIDTH)
_SEG_Z = (3 * ATT_WIDTH, 3 * ATT_WIDTH + D_INNER)
_SEG_XBC = (_SEG_Z[1], _SEG_Z[1] + CONV_DIM)
_SEG_GATES = (_SEG_XBC[1], _SEG_XBC[1] + 2 * D_MODEL)
_SEG_DT = (_SEG_GATES[1], _SEG_GATES[1] + DT_PAD)
N_IN_PAD = _SEG_DT[1]


def _resident(shape):
    nd = len(shape)
    return pl.BlockSpec(shape, lambda *_: (0,) * nd, pipeline_mode=pl.Buffered(1))


def _params(semantics):
    return pltpu.CompilerParams(dimension_semantics=semantics, vmem_limit_bytes=VMEM_LIMIT_BYTES)


def _rms_scale(x):
    return lax.rsqrt(jnp.mean(x * x, axis=-1, keepdims=True) + EPS)


def _inproj_kernel(x_ref, g_ref, w_ref, gq_ref, gk_ref, hsum_ref,
                   q_out, k_out, v_out, z_out, xbc_out, gates_out, dt_out):
    x = x_ref[...]
    h = (x * _rms_scale(x) * g_ref[...]).astype(BF16)

    def seg(bounds):
        return jnp.dot(h, w_ref[:, bounds[0]:bounds[1]], preferred_element_type=F32)

    def head_norm(t, g):
        sq = (t * t).astype(BF16)
        parts = []
        for blk in range(ATT_WIDTH // MXU_DIM):
            cols = slice(blk * MXU_DIM, (blk + 1) * MXU_DIM)
            ss = jnp.dot(sq[:, cols], hsum_ref[...], preferred_element_type=F32)
            parts.append(t[:, cols] * lax.rsqrt(ss * (1.0 / HEAD_DIM_A) + EPS))
        return jnp.concatenate(parts, axis=-1) * g

    q_out[...] = (head_norm(seg(_SEG_Q), gq_ref[...]) * (HEAD_DIM_A ** -0.5)).astype(BF16)
    k_out[...] = head_norm(seg(_SEG_K), gk_ref[...])
    v_out[...] = seg(_SEG_V)
    z_out[...] = seg(_SEG_Z).astype(BF16)
    xbc_out[...] = seg(_SEG_XBC)
    gates_out[...] = seg(_SEG_GATES).astype(BF16)
    dt_out[...] = seg(_SEG_DT)


def _inproj(x2d, g_mix, w_all, gq, gk, hsum, tm):
    t = x2d.shape[0]
    row = lambda w: pl.BlockSpec((tm, w), lambda i: (i, 0))
    outs = [(ATT_WIDTH, BF16), (ATT_WIDTH, F32), (ATT_WIDTH, F32), (D_INNER, BF16),
            (CONV_DIM, F32), (2 * D_MODEL, BF16), (DT_PAD, F32)]
    return pl.pallas_call(
        _inproj_kernel,
        grid=(t // tm,),
        in_specs=[row(D_MODEL), _resident((1, D_MODEL)), _resident((D_MODEL, N_IN_PAD)),
                  _resident((1, ATT_WIDTH)), _resident((1, ATT_WIDTH)), _resident((MXU_DIM, MXU_DIM))],
        out_specs=[row(w) for w, _ in outs],
        out_shape=[jax.ShapeDtypeStruct((t, w), d) for w, d in outs],
        compiler_params=_params(("parallel",)),
        name="inproj",
    )(x2d, g_mix, w_all, gq, gk, hsum)


HEADS_PER_ATT_BLOCK = MXU_DIM // HEAD_DIM_A


def _attn_kernel(*refs, has_cache, seq, tq):
    if has_cache:
        q_ref, k_ref, v_ref, bias_ref, ck_ref, cv_ref, o_ref, kp, vp = refs
    else:
        q_ref, k_ref, v_ref, bias_ref, o_ref, kp, vp = refs
    nk = ATT_PAST + tq
    qi = pl.program_id(2)

    @pl.when(qi == 0)
    def _fill_band_buffers():
        if has_cache:
            kp[0:ATT_PAST, :] = ck_ref[0].astype(BF16)
            vp[0:ATT_PAST, :] = cv_ref[0].astype(BF16)
        else:
            kp[0:ATT_PAST, :] = jnp.zeros((ATT_PAST, MXU_DIM), BF16)
            vp[0:ATT_PAST, :] = jnp.zeros((ATT_PAST, MXU_DIM), BF16)
        kp[ATT_PAST:ATT_PAST + seq, :] = k_ref[0].astype(BF16)
        vp[ATT_PAST:ATT_PAST + seq, :] = v_ref[0].astype(BF16)

    start = pl.multiple_of(qi * tq, tq)
    kb = kp[pl.ds(start, nk), :]
    vb = vp[pl.ds(start, nk), :]
    q = q_ref[0]
    if not has_cache:
        col = lax.broadcasted_iota(jnp.int32, (1, nk), 1)
        mask = jnp.where(col + start - ATT_PAST >= 0, 0.0, NEG_INF).astype(F32)
    outs = []
    for h in range(HEADS_PER_ATT_BLOCK):
        cols = slice(h * HEAD_DIM_A, (h + 1) * HEAD_DIM_A)
        s = lax.dot_general(q[:, cols], kb[:, cols], (((1,), (1,)), ((), ())),
                            preferred_element_type=F32)
        s = s + bias_ref[h]
        if not has_cache:
            s = s + mask
        m = jnp.max(s, axis=-1, keepdims=True)
        p = jnp.exp(s - m)
        denom = jnp.sum(p, axis=-1, keepdims=True)
        o = jnp.dot(p.astype(BF16), vb[:, cols], preferred_element_type=F32)
        outs.append(o / denom)
    o_ref[0] = jnp.concatenate(outs, axis=-1).astype(BF16)


def _attention(q, k, v, bias, cache, tq):
    b, seq, _ = q.shape
    nk = ATT_PAST + tq
    has_cache = cache is not None
    hb = ATT_WIDTH // MXU_DIM
    qspec = pl.BlockSpec((1, tq, MXU_DIM), lambda bi, hi, qi: (bi, qi, hi))
    kvspec = pl.BlockSpec((1, seq, MXU_DIM), lambda bi, hi, qi: (bi, 0, hi))
    in_specs = [qspec, kvspec, kvspec,
                pl.BlockSpec((HEADS_PER_ATT_BLOCK, tq, nk), lambda bi, hi, qi: (hi, 0, 0))]
    args = [q, k, v, bias]
    if has_cache:
        cspec = pl.BlockSpec((1, ATT_PAST, MXU_DIM), lambda bi, hi, qi: (bi, 0, hi))
        in_specs += [cspec, cspec]
        args += list(cache)
    return pl.pallas_call(
        functools.partial(_attn_kernel, has_cache=has_cache, seq=seq, tq=tq),
        grid=(b, hb, seq // tq),
        in_specs=in_specs,
        out_specs=qspec,
        out_shape=jax.ShapeDtypeStruct((b, seq, ATT_WIDTH), BF16),
        scratch_shapes=[pltpu.VMEM((ATT_PAST + seq, MXU_DIM), BF16),
                        pltpu.VMEM((ATT_PAST + seq, MXU_DIM), BF16)],
        compiler_params=_params(("parallel", "parallel", "arbitrary")),
        name="band_attention",
    )(*args)


def _softplus(x):
    return jnp.maximum(x, 0.0) + jnp.log(1.0 + jnp.exp(-jnp.abs(x)))


def _split3(v):
    hi = v.astype(BF16)
    r1 = v - hi.astype(F32)
    mid = r1.astype(BF16)
    lo = (r1 - mid.astype(F32)).astype(BF16)
    return jnp.concatenate([hi, mid, lo], axis=-1)


def _ssd_kernel(*refs, has_state, valid_rows, n_chunks):
    if has_state:
        (xbc_ref, dt_ref, z_ref, ctx_ref, h0_ref, cw_ref, cb_ref, dtb_ref, alog_ref, dskip_ref,
         gssm_ref, expand_ref, yb_out, st_out, ctx_out, state, xe, ybuf) = refs
    else:
        (xbc_ref, dt_ref, z_ref, ctx_ref, cw_ref, cb_ref, dtb_ref, alog_ref, dskip_ref,
         gssm_ref, expand_ref, yb_out, st_out, ctx_out, state, xe, ybuf) = refs
    q = CHUNK
    c = pl.program_id(1)

    @pl.when(c == 0)
    def _load_carry():
        xe[0:SUBLANES, :] = ctx_ref[0]
        for g in range(N_GROUPS):
            if has_state:
                state[g] = h0_ref[0, g * GROUP_WIDTH:(g + 1) * GROUP_WIDTH, :].T
            else:
                state[g] = jnp.zeros((D_STATE, GROUP_WIDTH), F32)

    xe[SUBLANES:SUBLANES + q, :] = xbc_ref[0]
    first = SUBLANES - (SSM_CONV - 1)
    acc = cb_ref[...] + xe[first:first + q, :] * cw_ref[0:1, :]
    for i in range(1, SSM_CONV):
        acc = acc + xe[first + i:first + i + q, :] * cw_ref[i:i + 1, :]
    xc = acc * jax.nn.sigmoid(acc)
    tail = xe[valid_rows:valid_rows + SUBLANES, :]
    xe[0:SUBLANES, :] = tail

    xs = xc[:, :D_INNER]
    bm = xc[:, D_INNER:D_INNER + N_GROUPS * D_STATE]
    cm = xc[:, D_INNER + N_GROUPS * D_STATE:].astype(BF16)

    dt = _softplus(dt_ref[0] + dtb_ref[...])
    if valid_rows < q:
        rows = lax.broadcasted_iota(jnp.int32, (q, DT_PAD), 0)
        dt = jnp.where(rows < valid_rows, dt, 0.0)
    ac = dt * (-jnp.exp(alog_ref[...]))
    ri = lax.broadcasted_iota(jnp.int32, (q, q), 0)
    ci = lax.broadcasted_iota(jnp.int32, (q, q), 1)
    acs = jnp.dot((ri >= ci).astype(F32), ac, preferred_element_type=F32,
                  precision=lax.Precision.HIGHEST)

    expand = lambda v: jnp.dot(_split3(v), expand_ref[...], preferred_element_type=F32)
    acol = expand(acs)
    dtcol = expand(dt)
    last = acol[q - 1:q, :]
    dx = dtcol * xs
    wj = (jnp.exp(last - acol) * dx).astype(BF16)
    ea = jnp.exp(acol)

    lane = lax.broadcasted_iota(jnp.int32, (q, D_INNER), 1)
    rowi = lax.broadcasted_iota(jnp.int32, (q, D_INNER), 0)
    j_of_lane = lane % q
    arow = jnp.sum(jnp.where(j_of_lane == rowi, acol, 0.0), axis=0, keepdims=True)

    tri = (lax.broadcasted_iota(jnp.int32, (q, MXU_DIM), 0)
           >= lax.broadcasted_iota(jnp.int32, (q, MXU_DIM), 1) % q)
    br = lax.broadcasted_iota(jnp.int32, (MXU_DIM, MXU_DIM), 0) // SSM_HEAD_DIM
    bc = lax.broadcasted_iota(jnp.int32, (MXU_DIM, MXU_DIM), 1) // SSM_HEAD_DIM
    blockdiag = br == bc

    for g in range(N_GROUPS):
        cg = cm[:, g * D_STATE:(g + 1) * D_STATE]
        bg = bm[:, g * D_STATE:(g + 1) * D_STATE].astype(BF16)
        bg4 = jnp.concatenate([bg] * HEADS_PER_MXU_TILE, axis=0)
        cb4 = lax.dot_general(cg, bg4, (((1,), (1,)), ((), ())), preferred_element_type=F32)
        sg = state[g].astype(BF16)
        for half in range(GROUP_WIDTH // MXU_DIM):
            lo = g * GROUP_WIDTH + half * MXU_DIM
            cols = slice(lo, lo + MXU_DIM)
            seg = acol[:, cols] - arow[:, cols]
            decay = jnp.where(tri, jnp.exp(jnp.where(tri, seg, 0.0)), 0.0)
            mp = (cb4 * decay).astype(BF16)
            dxb = dx[:, cols].astype(BF16)
            bd = jnp.where(blockdiag, jnp.concatenate([dxb] * HEADS_PER_MXU_TILE, axis=0),
                           jnp.zeros((), BF16))
            y = jnp.dot(mp, bd, preferred_element_type=F32)
            y = y + jnp.dot(cg, sg[:, half * MXU_DIM:(half + 1) * MXU_DIM],
                            preferred_element_type=F32) * ea[:, cols]
            ybuf[:, cols] = y + dskip_ref[:, cols] * xs[:, cols]

    el = jnp.exp(last)
    for g in range(N_GROUPS):
        cols = slice(g * GROUP_WIDTH, (g + 1) * GROUP_WIDTH)
        bgt = bm[:, g * D_STATE:(g + 1) * D_STATE].T.astype(BF16)
        state[g] = state[g] * el[:, cols] + jnp.dot(bgt, wj[:, cols], preferred_element_type=F32)

    zf = z_ref[0].astype(F32)
    yg = ybuf[...] * (zf * jax.nn.sigmoid(zf))
    yb_out[0] = (yg * _rms_scale(yg) * gssm_ref[...]).astype(BF16)

    @pl.when(c == n_chunks - 1)
    def _emit_state():
        ctx_out[0] = tail
        for g in range(N_GROUPS):
            st_out[0, g * GROUP_WIDTH:(g + 1) * GROUP_WIDTH, :] = state[g].T


def _ssd(xbc, dt_raw, z, ctx8, h0, prm, valid_rows):
    b, seq, _ = xbc.shape
    n_chunks = seq // CHUNK
    has_state = h0 is not None
    chunk = lambda w: pl.BlockSpec((1, CHUNK, w), lambda bi, ci: (bi, ci, 0))
    per_batch = lambda r, w: pl.BlockSpec((1, r, w), lambda bi, ci: (bi, 0, 0))
    in_specs = [chunk(CONV_DIM), chunk(DT_PAD), chunk(D_INNER), per_batch(SUBLANES, CONV_DIM)]
    args = [xbc, dt_raw, z, ctx8]
    if has_state:
        in_specs.append(per_batch(D_INNER, D_STATE))
        args.append(h0)
    in_specs += [_resident(p.shape) for p in prm]
    args += list(prm)
    return pl.pallas_call(
        functools.partial(_ssd_kernel, has_state=has_state, valid_rows=valid_rows, n_chunks=n_chunks),
        grid=(b, n_chunks),
        in_specs=in_specs,
        out_specs=[chunk(D_INNER), per_batch(D_INNER, D_STATE), per_batch(SUBLANES, CONV_DIM)],
        out_shape=[jax.ShapeDtypeStruct((b, seq, D_INNER), BF16),
                   jax.ShapeDtypeStruct((b, D_INNER, D_STATE), F32),
                   jax.ShapeDtypeStruct((b, SUBLANES, CONV_DIM), F32)],
        scratch_shapes=[pltpu.VMEM((N_GROUPS, D_STATE, GROUP_WIDTH), F32),
                        pltpu.VMEM((SUBLANES + CHUNK, CONV_DIM), F32),
                        pltpu.VMEM((CHUNK, D_INNER), F32)],
        compiler_params=_params(("parallel", "arbitrary")),
        name="ssd",
    )(*args)


def _merge_kernel(x_ref, ya_ref, yb_ref, gates_ref, bg_ref, wa_ref, wb_ref, wo_ref, o_ref):
    gate = jax.nn.sigmoid(gates_ref[...].astype(F32) + bg_ref[...])
    pa = jnp.dot(ya_ref[...], wa_ref[...], preferred_element_type=F32)
    pb = jnp.dot(yb_ref[...], wb_ref[...], preferred_element_type=F32)
    t = gate[:, :D_MODEL] * pa + gate[:, D_MODEL:] * pb
    o_ref[...] = x_ref[...] + jnp.dot(t.astype(BF16), wo_ref[...], preferred_element_type=F32)


def _merge(x2d, ya, yb, gates, b_gate, wa, wb, wo, tm):
    t = x2d.shape[0]
    row = lambda w: pl.BlockSpec((tm, w), lambda i: (i, 0))
    return pl.pallas_call(
        _merge_kernel,
        grid=(t // tm,),
        in_specs=[row(D_MODEL), row(ATT_WIDTH), row(D_INNER), row(2 * D_MODEL),
                  _resident(b_gate.shape), _resident(wa.shape), _resident(wb.shape), _resident(wo.shape)],
        out_specs=row(D_MODEL),
        out_shape=jax.ShapeDtypeStruct((t, D_MODEL), F32),
        compiler_params=_params(("parallel",)),
        name="merge_outproj",
    )(x2d, ya, yb, gates, b_gate, wa, wb, wo)


def _ffn_kernel(x_ref, pe_ref, ctx_ref, gffn_ref, wup_ref, cw_ref, cb_ref, wdown_ref,
                gple_ref, wgate_ref, wple_ref, o_ref, ctx_out, ue, *, tm, tiles_per_batch):
    i = pl.program_id(0)

    @pl.when(i % tiles_per_batch == 0)
    def _load_context():
        ue[0:SUBLANES, :] = ctx_ref[0]

    x = x_ref[...]
    xn = (x * _rms_scale(x) * gffn_ref[...]).astype(BF16)
    first = SUBLANES - (FFN_CONV - 1)
    acc = jnp.zeros((tm, D_MODEL), F32)
    for j in range(D_FF // FFN_COL_BLOCK):
        conv = []
        for part in range(2):
            lo = part * D_FF + j * FFN_COL_BLOCK
            cols = slice(lo, lo + FFN_COL_BLOCK)
            ue[SUBLANES:SUBLANES + tm, cols] = jnp.dot(xn, wup_ref[:, cols], preferred_element_type=F32)
            u = cb_ref[:, cols] + ue[first:first + tm, cols] * cw_ref[0:1, cols]
            for k in range(1, FFN_CONV):
                u = u + ue[first + k:first + k + tm, cols] * cw_ref[k:k + 1, cols]
            conv.append(u)
        hidden = (jax.nn.gelu(conv[0], approximate=True) * conv[1]).astype(BF16)
        acc = acc + jnp.dot(hidden, wdown_ref[j * FFN_COL_BLOCK:(j + 1) * FFN_COL_BLOCK, :],
                            preferred_element_type=F32)
    tail = ue[tm:tm + SUBLANES, :]
    ue[0:SUBLANES, :] = tail

    @pl.when(i % tiles_per_batch == tiles_per_batch - 1)
    def _emit_context():
        ctx_out[0] = tail

    x2 = x + acc
    gate = jax.nn.sigmoid(jnp.dot((x2 * _rms_scale(x2) * gple_ref[...]).astype(BF16), wgate_ref[...],
                                  preferred_element_type=F32))
    o_ref[...] = x2 + gate * jnp.dot(pe_ref[...].astype(BF16), wple_ref[...], preferred_element_type=F32)


def _ffn(x2d, pe2d, ctx8, prm, batch, tm):
    t = x2d.shape[0]
    tiles_per_batch = t // batch // tm
    row = lambda w: pl.BlockSpec((tm, w), lambda i: (i, 0))
    ctx_spec = pl.BlockSpec((1, SUBLANES, 2 * D_FF), lambda i: (i // tiles_per_batch, 0, 0))
    return pl.pallas_call(
        functools.partial(_ffn_kernel, tm=tm, tiles_per_batch=tiles_per_batch),
        grid=(t // tm,),
        in_specs=[row(D_MODEL), row(PLE_DIM), ctx_spec] + [_resident(p.shape) for p in prm],
        out_specs=[row(D_MODEL), ctx_spec],
        out_shape=[jax.ShapeDtypeStruct((t, D_MODEL), F32),
                   jax.ShapeDtypeStruct((batch, SUBLANES, 2 * D_FF), F32)],
        scratch_shapes=[pltpu.VMEM((SUBLANES + tm, 2 * D_FF), F32)],
        compiler_params=_params(("arbitrary",)),
        name="convffn_ple",
    )(x2d, pe2d, ctx8, *prm)


def _pad_rows_front(a, rows):
    return jnp.pad(a, ((0, 0), (rows - a.shape[1], 0), (0, 0)))


def _layer(x, pe, kv_cache, ssm_ctx, h0, ffn_ctx, w, tm, tq):
    b, seq, _ = x.shape
    t = b * seq
    x2d = x.reshape(t, D_MODEL)
    qn, kn, v, z, xbc, gates, dt_raw = _inproj(x2d, w["g_mix"], w["w_in"], w["g_q"], w["g_k"], w["hsum"], tm)

    nk = ATT_PAST + tq
    rel = (ATT_PAST + jnp.arange(tq))[:, None] - jnp.arange(nk)[None, :]
    bias = w["rel_bias"][:, jnp.clip(rel, -REL_CLIP, REL_CLIP) + REL_CLIP]
    as3 = lambda a: a.reshape(b, seq, -1)
    cache = None
    if kv_cache is not None:
        cache = tuple(ck.reshape(b, ATT_PAST, ATT_WIDTH) for ck in kv_cache)
    ya = _attention(as3(qn), as3(kn), as3(v), bias, cache, tq)

    seq_pad = -(-seq // CHUNK) * CHUNK
    valid_rows = CHUNK - (seq_pad - seq)
    pad = lambda a: jnp.pad(as3(a), ((0, 0), (0, seq_pad - seq), (0, 0)))
    ctx8 = _pad_rows_front(ssm_ctx, SUBLANES)
    ssd_prm = (w["conv_ssm_w"], w["conv_ssm_b"], w["dt_bias"], w["a_log"], w["d_skip"], w["g_ssm"], w["expand"])
    yb, st, ctx_ssm = _ssd(pad(xbc), pad(dt_raw), pad(z), ctx8, h0, ssd_prm, valid_rows)
    yb = yb[:, :seq].reshape(t, D_INNER)

    x1 = _merge(x2d, ya.reshape(t, ATT_WIDTH), yb, gates, w["b_gate"], w["w_proj_a"], w["w_proj_b"], w["w_out"], tm)
    ffn_prm = (w["g_ffn"], w["w_up"], w["ffn_conv_w"], w["ffn_conv_b"], w["w_down"],
               w["g_ple"], w["w_ple_gate"], w["w_ple"])
    tm_ffn = min(tm, seq)
    y, ctx_ffn = _ffn(x1, pe.reshape(t, PLE_DIM), _pad_rows_front(ffn_ctx, SUBLANES), ffn_prm, b, tm_ffn)

    heads = lambda a: a.reshape(b, seq, N_HEADS_A, HEAD_DIM_A)
    new_k, new_v = heads(kn), heads(v)
    if kv_cache is None:
        new_k, new_v = new_k[:, -ATT_PAST:], new_v[:, -ATT_PAST:]
    state = st.reshape(b, N_HEADS_S, SSM_HEAD_DIM, D_STATE)
    return y.reshape(b, seq, D_MODEL), (new_k, new_v, state,
                                        ctx_ssm[:, SUBLANES - (SSM_CONV - 1):],
                                        ctx_ffn[:, SUBLANES - (FFN_CONV - 1):])


def _prep_weights(i, g_mix, w_in, b_gate, g_q, g_k, rel_bias, conv_ssm_w, conv_ssm_b, dt_bias, a_log, d_skip,
                  g_ssm, w_proj_a, w_proj_b, w_out, g_ffn, w_up, ffn_conv_w, ffn_conv_b, w_down, g_ple,
                  w_ple_gate, w_ple):
    row = lambda a: a[i].reshape(1, -1).astype(F32)
    pad_lanes = lambda a, n: jnp.pad(a, ((0, 0), (0, n - a.shape[1])))
    wi = w_in[i]
    o_z = 3 * ATT_WIDTH
    o_xbc = o_z + D_INNER
    o_dt = o_xbc + CONV_DIM
    o_g = o_dt + N_HEADS_S
    w_all = jnp.concatenate([wi[:, :o_dt], wi[:, o_g:], pad_lanes(wi[:, o_dt:o_g], DT_PAD)], axis=1)
    head_of_lane = jnp.arange(D_INNER) // SSM_HEAD_DIM
    expand1 = (jnp.arange(DT_PAD)[:, None] == head_of_lane[None, :]).astype(BF16)
    blk = jnp.arange(MXU_DIM) // HEAD_DIM_A
    return {
        "g_mix": row(g_mix), "w_in": w_all.astype(BF16), "b_gate": row(b_gate),
        "g_q": jnp.tile(row(g_q), (1, N_HEADS_A)), "g_k": jnp.tile(row(g_k), (1, N_HEADS_A)),
        "hsum": (blk[:, None] == blk[None, :]).astype(BF16),
        "rel_bias": rel_bias[i].astype(F32),
        "conv_ssm_w": jnp.pad(conv_ssm_w[i], ((0, SUBLANES - SSM_CONV), (0, 0))), "conv_ssm_b": row(conv_ssm_b),
        "dt_bias": pad_lanes(row(dt_bias), DT_PAD), "a_log": pad_lanes(row(a_log), DT_PAD),
        "d_skip": jnp.repeat(row(d_skip), SSM_HEAD_DIM, axis=1), "g_ssm": row(g_ssm),
        "expand": jnp.concatenate([expand1] * 3, axis=0),
        "w_proj_a": w_proj_a[i].astype(BF16), "w_proj_b": w_proj_b[i].astype(BF16), "w_out": w_out[i].astype(BF16),
        "g_ffn": row(g_ffn), "w_up": w_up[i].astype(BF16),
        "ffn_conv_w": jnp.pad(ffn_conv_w[i], ((0, SUBLANES - FFN_CONV), (0, 0))), "ffn_conv_b": row(ffn_conv_b),
        "w_down": w_down[i].astype(BF16), "g_ple": row(g_ple),
        "w_ple_gate": w_ple_gate[i].astype(BF16), "w_ple": w_ple[i].astype(BF16),
    }


def kernel(x_prompt, x_sample, cache_k, cache_v, state_ssm, state_conv_ssm, state_conv_ffn, p_prompt, p_sample, g_mix, w_in, b_gate, g_q, g_k, rel_bias, conv_ssm_w, conv_ssm_b, dt_bias, a_log, d_skip, g_ssm, w_proj_a, w_proj_b, w_out, g_ffn, w_up, ffn_conv_w, ffn_conv_b, w_down, g_ple, w_ple_gate, w_ple):
    depth = w_in.shape[0]
    bp = x_prompt.shape[0]
    bs = x_sample.shape[0]
    yp, ys = x_prompt, x_sample
    sp, ss = [], []
    for i in range(depth):
        w = _prep_weights(i, g_mix, w_in, b_gate, g_q, g_k, rel_bias, conv_ssm_w, conv_ssm_b, dt_bias, a_log,
                          d_skip, g_ssm, w_proj_a, w_proj_b, w_out, g_ffn, w_up, ffn_conv_w, ffn_conv_b, w_down,
                          g_ple, w_ple_gate, w_ple)
        yp, st_p = _layer(yp, p_prompt[i], None,
                          jnp.zeros((bp, SSM_CONV - 1, CONV_DIM), F32), None,
                          jnp.zeros((bp, FFN_CONV - 1, 2 * D_FF), F32), w, tm=256, tq=CHUNK)
        h0 = state_ssm[i].reshape(bs, D_INNER, D_STATE)
        ys, st_s = _layer(ys, p_sample[i], (cache_k[i], cache_v[i]), state_conv_ssm[i], h0,
                          state_conv_ffn[i], w, tm=x_sample.shape[1] * bs, tq=x_sample.shape[1])
        sp.append(st_p)
        ss.append(st_s)

    stack = lambda lst, j: jnp.stack([s[j] for s in lst], axis=0)
    return (yp, ys,
            stack(sp, 0), stack(sp, 1), stack(sp, 2), stack(sp, 3), stack(sp, 4),
            stack(ss, 0), stack(ss, 1), stack(ss, 2), stack(ss, 3), stack(ss, 4))
```

```python
import functools
import math

import jax
import jax.numpy as jnp
from jax import lax
from jax.experimental import pallas as pl
from jax.experimental.pallas import tpu as pltpu

F32 = jnp.float32
BF16 = jnp.bfloat16

D_MODEL = 1024
CHUNK = 64
PLE_DIM = 256
EPS = 1e-6
NEG_INF = -1e30
N_HEADS_A = 16
HEAD_DIM_A = 64
ATT_WIDTH = N_HEADS_A * HEAD_DIM_A
ATT_PAST = 8 * CHUNK
REL_CLIP = 128
D_INNER = 2 * D_MODEL
SSM_HEAD_DIM = 64
N_HEADS_S = D_INNER // SSM_HEAD_DIM
N_GROUPS = 4
GROUP_WIDTH = D_INNER // N_GROUPS
D_STATE = 128
SSM_CONV = 4
CONV_DIM = D_INNER + 2 * N_GROUPS * D_STATE
D_FF = 3 * D_MODEL
FFN_CONV = 3

LANES = 128
SUBLANES = 8
MXU_DIM = 256
VMEM_LIMIT_BYTES = 56 * 1024 * 1024

DT_PAD = LANES
HEADS_PER_MXU_TILE = MXU_DIM // SSM_HEAD_DIM
FFN_COL_BLOCK = 512

_SEG_Q = (0, ATT_WIDTH)
_SEG_K = (ATT_WIDTH, 2 * ATT_WIDTH)
_SEG_V = (2 * ATT_WIDTH, 3 * ATT_WIDTH)
_SEG_Z = (3 * ATT_WIDTH, 3 * ATT_WIDTH + D_INNER)
_SEG_XBC = (_SEG_Z[1], _SEG_Z[1] + CONV_DIM)
_SEG_GATES = (_SEG_XBC[1], _SEG_XBC[1] + 2 * D_MODEL)
_SEG_DT = (_SEG_GATES[1], _SEG_GATES[1] + DT_PAD)
N_IN_PAD = _SEG_DT[1]


def _resident(shape):
    nd = len(shape)
    return pl.BlockSpec(shape, lambda *_: (0,) * nd, pipeline_mode=pl.Buffered(1))


def _params(semantics):
    return pltpu.CompilerParams(dimension_semantics=semantics, vmem_limit_bytes=VMEM_LIMIT_BYTES)


def _rms_scale(x):
    return lax.rsqrt(jnp.mean(x * x, axis=-1, keepdims=True) + EPS)


def _inproj_kernel(x_ref, g_ref, w_ref, gq_ref, gk_ref, hsum_ref,
                   q_out, k_out, v_out, z_out, xbc_out, gates_out, dt_out, ktail_out, vtail_out,
                   *, tiles_per_batch, tail_tiles):
    x = x_ref[...]
    h = (x * _rms_scale(x) * g_ref[...]).astype(BF16)

    def seg(bounds):
        return jnp.dot(h, w_ref[:, bounds[0]:bounds[1]], preferred_element_type=F32)

    def head_norm(t, g):
        sq = (t * t).astype(BF16)
        parts = []
        for blk in range(ATT_WIDTH // MXU_DIM):
            cols = slice(blk * MXU_DIM, (blk + 1) * MXU_DIM)
            ss = jnp.dot(sq[:, cols], hsum_ref[...], preferred_element_type=F32)
            parts.append(t[:, cols] * lax.rsqrt(ss * (1.0 / HEAD_DIM_A) + EPS))
        return jnp.concatenate(parts, axis=-1) * g

    q_out[...] = (head_norm(seg(_SEG_Q), gq_ref[...]) * (HEAD_DIM_A ** -0.5)).astype(BF16)
    k = head_norm(seg(_SEG_K), gk_ref[...])
    v = seg(_SEG_V)
    k_out[...] = k.astype(BF16)
    v_out[...] = v.astype(BF16)
    z_out[...] = seg(_SEG_Z).astype(BF16)
    xbc_out[...] = seg(_SEG_XBC)
    gates_out[...] = seg(_SEG_GATES).astype(BF16)
    dt_out[...] = seg(_SEG_DT)

    @pl.when(pl.program_id(0) % tiles_per_batch >= tiles_per_batch - tail_tiles)
    def _emit_tail():
        ktail_out[...] = k
        vtail_out[...] = v


def _inproj(x2d, g_mix, w_all, gq, gk, hsum, tm, seq):
    t = x2d.shape[0]
    tiles_per_batch = max(seq // tm, 1)
    tail_tiles = max(min(ATT_PAST, seq) // tm, 1)
    n_tail_rows = (t // tm) // tiles_per_batch * tail_tiles * tm
    row =lambda w: pl.BlockSpec((tm, w), lambda i: (i, 0))
    first_tail = tiles_per_batch - tail_tiles
    tail = pl.BlockSpec((tm, ATT_WIDTH), lambda i: (
        (i // tiles_per_batch) * tail_tiles + jnp.maximum(i % tiles_per_batch - first_tail, 0), 0))
    outs = [(ATT_WIDTH, BF16), (ATT_WIDTH, BF16), (ATT_WIDTH, BF16), (D_INNER, BF16),
            (CONV_DIM, F32), (2 * D_MODEL, BF16), (DT_PAD, F32)]
    return pl.pallas_call(
        functools.partial(_inproj_kernel, tiles_per_batch=tiles_per_batch, tail_tiles=tail_tiles),
        grid=(t // tm,),
        in_specs=[row(D_MODEL), _resident((1, D_MODEL)), _resident((D_MODEL, N_IN_PAD)),
                  _resident((1, ATT_WIDTH)), _resident((1, ATT_WIDTH)), _resident((MXU_DIM, MXU_DIM))],
        out_specs=[row(w) for w, _ in outs] + [tail, tail],
        out_shape=[jax.ShapeDtypeStruct((t, w), d) for w, d in outs]
        + [jax.ShapeDtypeStruct((n_tail_rows, ATT_WIDTH), F32)] * 2,
        compiler_params=_params(("arbitrary",)),
        name="inproj",
    )(x2d, g_mix, w_all, gq, gk, hsum)


HEADS_PER_ATT_BLOCK = MXU_DIM // HEAD_DIM_A
HEADS_PER_LANE_TILE = LANES // HEAD_DIM_A


def _attn_kernel(*refs, has_cache, seq, qb):
    if has_cache:
        q_ref, k_ref, v_ref, bias_ref, ck_ref, cv_ref, o_ref, kp, vp = refs
    else:
        q_ref, k_ref, v_ref, bias_ref, o_ref, kp, vp = refs
    nk = ATT_PAST + qb

    if has_cache:
        kp[0:ATT_PAST, :] = ck_ref[0].astype(BF16)
        vp[0:ATT_PAST, :] = cv_ref[0].astype(BF16)
    kp[ATT_PAST:ATT_PAST + seq, :] = k_ref[0]
    vp[ATT_PAST:ATT_PAST + seq, :] = v_ref[0]

    lane = lax.broadcasted_iota(jnp.int32, (qb, LANES), 1)
    low_half = lane < HEAD_DIM_A

    def q_block(start, first_col):
        kb = kp[pl.ds(start + first_col, nk - first_col), :]
        vb = vp[pl.ds(start + first_col, nk - first_col), :]
        q = q_ref[0, pl.ds(start, qb), :]
        outs = []
        for pair in range(HEADS_PER_ATT_BLOCK // HEADS_PER_LANE_TILE):
            cols = slice(pair * LANES, (pair + 1) * LANES)
            qp, kpair, vpair = q[:, cols], kb[:, cols], vb[:, cols]
            halves = []
            for sub in range(HEADS_PER_LANE_TILE):
                keep = low_half if sub == 0 else jnp.logical_not(low_half)
                qh = jnp.where(keep, qp, jnp.zeros((), BF16))
                s = lax.dot_general(qh, kpair, (((1,), (1,)), ((), ())), preferred_element_type=F32)
                s = s + bias_ref[pair * HEADS_PER_LANE_TILE + sub, :, first_col:]
                m = jnp.max(s, axis=-1, keepdims=True)
                p = jnp.exp(s - m)
                denom = jnp.sum(p, axis=-1, keepdims=True)
                halves.append(jnp.dot(p.astype(BF16), vpair, preferred_element_type=F32) / denom)
            outs.append(jnp.where(low_half, halves[0], halves[1]))
        o_ref[0, pl.ds(start, qb), :] = jnp.concatenate(outs, axis=-1).astype(BF16)

    n_blocks = seq // qb
    n_peeled = 0 if has_cache else min(ATT_PAST // qb, n_blocks)
    for i in range(n_peeled):
        q_block(i * qb, ATT_PAST - i * qb)

    def steady(i, carry):
        q_block(pl.multiple_of(i * qb, qb), 0)
        return carry

    lax.fori_loop(n_peeled, n_blocks, steady, 0)


def _attention(q, k, v, bias, cache, qb):
    b, seq, _ = q.shape
    nk = ATT_PAST + qb
    has_cache = cache is not None
    hb = ATT_WIDTH // MXU_DIM
    seq_spec = pl.BlockSpec((1, seq, MXU_DIM), lambda bi, hi: (bi, 0, hi))
    in_specs = [seq_spec, seq_spec, seq_spec,
                pl.BlockSpec((HEADS_PER_ATT_BLOCK, qb, nk), lambda bi, hi: (hi, 0, 0))]
    args = [q, k, v, bias]
    if has_cache:
        cspec = pl.BlockSpec((1, ATT_PAST, MXU_DIM), lambda bi, hi: (bi, 0, hi))
        in_specs += [cspec, cspec]
        args += list(cache)
    return pl.pallas_call(
        functools.partial(_attn_kernel, has_cache=has_cache, seq=seq, qb=qb),
        grid=(b, hb),
        in_specs=in_specs,
        out_specs=seq_spec,
        out_shape=jax.ShapeDtypeStruct((b, seq, ATT_WIDTH), BF16),
        scratch_shapes=[pltpu.VMEM((ATT_PAST + seq, MXU_DIM), BF16),
                        pltpu.VMEM((ATT_PAST + seq, MXU_DIM), BF16)],
        compiler_params=_params(("parallel", "parallel")),
        name="band_attention",
    )(*args)


def _attention_bias(table, qb):
    nk = ATT_PAST + qb
    n_diag = qb + nk - 1
    diag_rel = (ATT_PAST + qb - 1) - jnp.arange(n_diag)
    r = table[:, jnp.clip(diag_rel, -REL_CLIP, REL_CLIP) + REL_CLIP]
    h = table.shape[0]
    skew = jnp.tile(r, (1, qb + 1))[:, :qb * (n_diag + 1)].reshape(h, qb, n_diag + 1)
    bias = skew[:, ::-1, :nk]
    qc = jnp.arange(qb)[:, None] // CHUNK
    kc = jnp.arange(nk)[None, :] // CHUNK
    visible = (kc >= qc) & (kc <= qc + ATT_PAST // CHUNK)
    return jnp.where(visible[None], bias, NEG_INF).astype(F32)


def _softplus(x):
    return jnp.maximum(x, 0.0) + jnp.log(1.0 + jnp.exp(-jnp.abs(x)))


def _split3(v):
    hi = v.astype(BF16)
    r1 = v - hi.astype(F32)
    mid = r1.astype(BF16)
    lo = (r1 - mid.astype(F32)).astype(BF16)
    return jnp.concatenate([hi, mid, lo], axis=-1)


def _ssd_kernel(*refs, has_state, valid_rows, n_chunks):
    if has_state:
        (xbc_ref, dt_ref, z_ref, ctx_ref, h0_ref, cw_ref, cb_ref, dtb_ref, alog_ref, dskip_ref,
         gssm_ref, expand_ref, yb_out, st_out, ctx_out, state, xe, ybuf) = refs
    else:
        (xbc_ref, dt_ref, z_ref, ctx_ref, cw_ref, cb_ref, dtb_ref, alog_ref, dskip_ref,
         gssm_ref, expand_ref, yb_out, st_out, ctx_out, state, xe, ybuf) = refs
    q = CHUNK
    c = pl.program_id(1)

    @pl.when(c == 0)
    def _load_carry():
        xe[0:SUBLANES, :] = ctx_ref[0]
        for g in range(N_GROUPS):
            if has_state:
                state[g] = h0_ref[0, g * GROUP_WIDTH:(g + 1) * GROUP_WIDTH, :].T
            else:
                state[g] = jnp.zeros((D_STATE, GROUP_WIDTH), F32)

    xe[SUBLANES:SUBLANES + q, :] = xbc_ref[0]
    first = SUBLANES - (SSM_CONV - 1)
    acc = cb_ref[...] + xe[first:first + q, :] * cw_ref[0:1, :]
    for i in range(1, SSM_CONV):
        acc = acc + xe[first + i:first + i + q, :] * cw_ref[i:i + 1, :]
    xc = acc * jax.nn.sigmoid(acc)
    tail = xe[valid_rows:valid_rows + SUBLANES, :]
    xe[0:SUBLANES, :] = tail

    xs = xc[:, :D_INNER]
    bm = xc[:, D_INNER:D_INNER + N_GROUPS * D_STATE]
    cm = xc[:, D_INNER + N_GROUPS * D_STATE:].astype(BF16)

    dt = _softplus(dt_ref[0] + dtb_ref[...])
    if valid_rows < q:
        rows = lax.broadcasted_iota(jnp.int32, (q, DT_PAD), 0)
        dt = jnp.where(rows < valid_rows, dt, 0.0)
    ac = dt * (-jnp.exp(alog_ref[...]))
    ri = lax.broadcasted_iota(jnp.int32, (q, q), 0)
    ci = lax.broadcasted_iota(jnp.int32, (q, q), 1)
    acs = jnp.dot((ri >= ci).astype(F32), ac, preferred_element_type=F32,
                  precision=lax.Precision.HIGHEST)

    expand = lambda v: jnp.dot(_split3(v), expand_ref[...], preferred_element_type=F32)
    acol = expand(acs)
    dtcol = expand(dt)
    last = acol[q - 1:q, :]
    dx = dtcol * xs
    wj = (jnp.exp(last - acol) * dx).astype(BF16)
    ea = jnp.exp(acol)

    lane = lax.broadcasted_iota(jnp.int32, (q, D_INNER), 1)
    rowi = lax.broadcasted_iota(jnp.int32, (q, D_INNER), 0)
    j_of_lane = lane % q
    arow = jnp.sum(jnp.where(j_of_lane == rowi, acol, 0.0), axis=0, keepdims=True)

    tri = (lax.broadcasted_iota(jnp.int32, (q, MXU_DIM), 0)
           >= lax.broadcasted_iota(jnp.int32, (q, MXU_DIM), 1) % q)
    br = lax.broadcasted_iota(jnp.int32, (MXU_DIM, MXU_DIM), 0) // SSM_HEAD_DIM
    bc = lax.broadcasted_iota(jnp.int32, (MXU_DIM, MXU_DIM), 1) // SSM_HEAD_DIM
    blockdiag = br == bc

    for g in range(N_GROUPS):
        cg = cm[:, g * D_STATE:(g + 1) * D_STATE]
        bg = bm[:, g * D_STATE:(g + 1) * D_STATE].astype(BF16)
        bg4 = jnp.concatenate([bg] * HEADS_PER_MXU_TILE, axis=0)
        cb4 = lax.dot_general(cg, bg4, (((1,), (1,)), ((), ())), preferred_element_type=F32)
        sg = state[g].astype(BF16)
        for half in range(GROUP_WIDTH // MXU_DIM):
            lo = g * GROUP_WIDTH + half * MXU_DIM
            cols = slice(lo, lo + MXU_DIM)
            seg = acol[:, cols] - arow[:, cols]
            decay = jnp.where(tri, jnp.exp(jnp.where(tri, seg, 0.0)), 0.0)
            mp = (cb4 * decay).astype(BF16)
            dxb = dx[:, cols].astype(BF16)
            bd = jnp.where(blockdiag, jnp.concatenate([dxb] * HEADS_PER_MXU_TILE, axis=0),
                           jnp.zeros((), BF16))
            y = jnp.dot(mp, bd, preferred_element_type=F32)
            y = y + jnp.dot(cg, sg[:, half * MXU_DIM:(half + 1) * MXU_DIM],
                            preferred_element_type=F32) * ea[:, cols]
            ybuf[:, cols] = y + dskip_ref[:, cols] * xs[:, cols]

    el = jnp.exp(last)
    for g in range(N_GROUPS):
        cols = slice(g * GROUP_WIDTH, (g + 1) * GROUP_WIDTH)
        bgt = bm[:, g * D_STATE:(g + 1) * D_STATE].T.astype(BF16)
        state[g] = state[g] * el[:, cols] + jnp.dot(bgt, wj[:, cols], preferred_element_type=F32)

    zf = z_ref[0].astype(F32)
    yg = ybuf[...] * (zf * jax.nn.sigmoid(zf))
    yb_out[0] = (yg * _rms_scale(yg) * gssm_ref[...]).astype(BF16)

    @pl.when(c == n_chunks - 1)
    def _emit_state():
        ctx_out[0] = tail
        for g in range(N_GROUPS):
            st_out[0, g * GROUP_WIDTH:(g + 1) * GROUP_WIDTH, :] = state[g].T


def _ssd(xbc, dt_raw, z, ctx8, h0, prm, valid_rows):
    b, seq, _ = xbc.shape
    n_chunks = seq // CHUNK
    has_state = h0 is not None
    chunk = lambda w: pl.BlockSpec((1, CHUNK, w), lambda bi, ci: (bi, ci, 0))
    per_batch = lambda r, w: pl.BlockSpec((1, r, w), lambda bi, ci: (bi, 0, 0))
    in_specs = [chunk(CONV_DIM), chunk(DT_PAD), chunk(D_INNER), per_batch(SUBLANES, CONV_DIM)]
    args = [xbc, dt_raw, z, ctx8]
    if has_state:
        in_specs.append(per_batch(D_INNER, D_STATE))
        args.append(h0)
    in_specs += [_resident(p.shape) for p in prm]
    args += list(prm)
    return pl.pallas_call(
        functools.partial(_ssd_kernel, has_state=has_state, valid_rows=valid_rows, n_chunks=n_chunks),
        grid=(b, n_chunks),
        in_specs=in_specs,
        out_specs=[chunk(D_INNER), per_batch(D_INNER, D_STATE), per_batch(SUBLANES, CONV_DIM)],
        out_shape=[jax.ShapeDtypeStruct((b, seq, D_INNER), BF16),
                   jax.ShapeDtypeStruct((b, D_INNER, D_STATE), F32),
                   jax.ShapeDtypeStruct((b, SUBLANES, CONV_DIM), F32)],
        scratch_shapes=[pltpu.VMEM((N_GROUPS, D_STATE, GROUP_WIDTH), F32),
                        pltpu.VMEM((SUBLANES + CHUNK, CONV_DIM), F32),
                        pltpu.VMEM((CHUNK, D_INNER), F32)],
        compiler_params=_params(("parallel", "arbitrary")),
        name="ssd",
    )(*args)


def _merge_kernel(x_ref, ya_ref, yb_ref, gates_ref, bg_ref, wa_ref, wb_ref, wo_ref, o_ref):
    gate = jax.nn.sigmoid(gates_ref[...].astype(F32) + bg_ref[...])
    pa = jnp.dot(ya_ref[...], wa_ref[...], preferred_element_type=F32)
    pb = jnp.dot(yb_ref[...], wb_ref[...], preferred_element_type=F32)
    t = gate[:, :D_MODEL] * pa + gate[:, D_MODEL:] * pb
    o_ref[...] = x_ref[...] + jnp.dot(t.astype(BF16), wo_ref[...], preferred_element_type=F32)


def _merge(x2d, ya, yb, gates, b_gate, wa, wb, wo, tm):
    t = x2d.shape[0]
    row = lambda w: pl.BlockSpec((tm, w), lambda i: (i, 0))
    return pl.pallas_call(
        _merge_kernel,
        grid=(t // tm,),
        in_specs=[row(D_MODEL), row(ATT_WIDTH), row(D_INNER), row(2 * D_MODEL),
                  _resident(b_gate.shape), _resident(wa.shape), _resident(wb.shape), _resident(wo.shape)],
        out_specs=row(D_MODEL),
        out_shape=jax.ShapeDtypeStruct((t, D_MODEL), F32),
        compiler_params=_params(("parallel",)),
        name="merge_outproj",
    )(x2d, ya, yb, gates, b_gate, wa, wb, wo)


def _ffn_kernel(x_ref, pe_ref, ctx_ref, gffn_ref, wup_ref, cw_ref, cb_ref, wdown_ref,
                gple_ref, wgate_ref, wple_ref, o_ref, ctx_out, ue, *, tm, tiles_per_batch):
    i = pl.program_id(0)

    @pl.when(i % tiles_per_batch == 0)
    def _load_context():
        ue[0:SUBLANES, :] = ctx_ref[0]

    x = x_ref[...]
    xn = (x * _rms_scale(x) * gffn_ref[...]).astype(BF16)
    first = SUBLANES - (FFN_CONV - 1)
    acc = jnp.zeros((tm, D_MODEL), F32)
    for j in range(D_FF // FFN_COL_BLOCK):
        conv = []
        for part in range(2):
            lo = part * D_FF + j * FFN_COL_BLOCK
            cols = slice(lo, lo + FFN_COL_BLOCK)
            ue[SUBLANES:SUBLANES + tm, cols] = jnp.dot(xn, wup_ref[:, cols], preferred_element_type=F32)
            u = cb_ref[:, cols] + ue[first:first + tm, cols] * cw_ref[0:1, cols]
            for k in range(1, FFN_CONV):
                u = u + ue[first + k:first + k + tm, cols] * cw_ref[k:k + 1, cols]
            conv.append(u)
        hidden = (jax.nn.gelu(conv[0], approximate=True) * conv[1]).astype(BF16)
        acc = acc + jnp.dot(hidden, wdown_ref[j * FFN_COL_BLOCK:(j + 1) * FFN_COL_BLOCK, :],
                            preferred_element_type=F32)
    tail = ue[tm:tm + SUBLANES, :]
    ue[0:SUBLANES, :] = tail

    @pl.when(i % tiles_per_batch == tiles_per_batch - 1)
    def _emit_context():
        ctx_out[0] = tail

    x2 = x + acc
    gate = jax.nn.sigmoid(jnp.dot((x2 * _rms_scale(x2) * gple_ref[...]).astype(BF16), wgate_ref[...],
                                  preferred_element_type=F32))
    o_ref[...] = x2 + gate * jnp.dot(pe_ref[...].astype(BF16), wple_ref[...], preferred_element_type=F32)


def _ffn(x2d, pe2d, ctx8, prm, batch, tm):
    t = x2d.shape[0]
    tiles_per_batch = t // batch // tm
    row = lambda w: pl.BlockSpec((tm, w), lambda i: (i, 0))
    ctx_spec = pl.BlockSpec((1, SUBLANES, 2 * D_FF), lambda i: (i // tiles_per_batch, 0, 0))
    return pl.pallas_call(
        functools.partial(_ffn_kernel, tm=tm, tiles_per_batch=tiles_per_batch),
        grid=(t // tm,),
        in_specs=[row(D_MODEL), row(PLE_DIM), ctx_spec] + [_resident(p.shape) for p in prm],
        out_specs=[row(D_MODEL), ctx_spec],
        out_shape=[jax.ShapeDtypeStruct((t, D_MODEL), F32),
                   jax.ShapeDtypeStruct((batch, SUBLANES, 2 * D_FF), F32)],
        scratch_shapes=[pltpu.VMEM((SUBLANES + tm, 2 * D_FF), F32)],
        compiler_params=_params(("arbitrary",)),
        name="convffn_ple",
    )(x2d, pe2d, ctx8, *prm)


def _pad_rows_front(a, rows):
    return jnp.pad(a, ((0, 0), (rows - a.shape[1], 0), (0, 0)))


def _layer(x, pe, kv_cache, ssm_ctx, h0, ffn_ctx, w, tm, tq):
    b, seq, _ = x.shape
    t = b * seq
    x2d = x.reshape(t, D_MODEL)
    qn, kn, v, z, xbc, gates, dt_raw, k_tail, v_tail = _inproj(
        x2d, w["g_mix"], w["w_in"], w["g_q"], w["g_k"], w["hsum"], tm, seq)

    as3 = lambda a: a.reshape(b, seq, -1)
    cache = None
    if kv_cache is not None:
        cache = tuple(ck.reshape(b, ATT_PAST, ATT_WIDTH) for ck in kv_cache)
    ya = _attention(as3(qn), as3(kn), as3(v), _attention_bias(w["rel_bias"], tq), cache, tq)

    seq_pad = -(-seq // CHUNK) * CHUNK
    valid_rows = CHUNK - (seq_pad - seq)
    pad = lambda a: jnp.pad(as3(a), ((0, 0), (0, seq_pad - seq), (0, 0)))
    ctx8 = _pad_rows_front(ssm_ctx, SUBLANES)
    ssd_prm = (w["conv_ssm_w"], w["conv_ssm_b"], w["dt_bias"], w["a_log"], w["d_skip"], w["g_ssm"], w["expand"])
    yb, st, ctx_ssm = _ssd(pad(xbc), pad(dt_raw), pad(z), ctx8, h0, ssd_prm, valid_rows)
    yb = yb[:, :seq].reshape(t, D_INNER)

    x1 = _merge(x2d, ya.reshape(t, ATT_WIDTH), yb, gates, w["b_gate"], w["w_proj_a"], w["w_proj_b"], w["w_out"], tm)
    ffn_prm = (w["g_ffn"], w["w_up"], w["ffn_conv_w"], w["ffn_conv_b"], w["w_down"],
               w["g_ple"], w["w_ple_gate"], w["w_ple"])
    tm_ffn = min(tm, seq)
    y, ctx_ffn = _ffn(x1, pe.reshape(t, PLE_DIM), _pad_rows_front(ffn_ctx, SUBLANES), ffn_prm, b, tm_ffn)

    heads = lambda a: a.reshape(b, min(seq, ATT_PAST), N_HEADS_A, HEAD_DIM_A)
    new_k, new_v = heads(k_tail), heads(v_tail)
    state = st.reshape(b, N_HEADS_S, SSM_HEAD_DIM, D_STATE)
    return y.reshape(b, seq, D_MODEL), (new_k, new_v, state,
                                        ctx_ssm[:, SUBLANES - (SSM_CONV - 1):],
                                        ctx_ffn[:, SUBLANES - (FFN_CONV - 1):])


def _prep_weights(i, g_mix, w_in, b_gate, g_q, g_k, rel_bias, conv_ssm_w, conv_ssm_b, dt_bias, a_log, d_skip,
                  g_ssm, w_proj_a, w_proj_b, w_out, g_ffn, w_up, ffn_conv_w, ffn_conv_b, w_down, g_ple,
                  w_ple_gate, w_ple):
    row = lambda a: a[i].reshape(1, -1).astype(F32)
    pad_lanes = lambda a, n: jnp.pad(a, ((0, 0), (0, n - a.shape[1])))
    wi = w_in[i]
    o_z = 3 * ATT_WIDTH
    o_xbc = o_z + D_INNER
    o_dt = o_xbc + CONV_DIM
    o_g = o_dt + N_HEADS_S
    w_all = jnp.concatenate([wi[:, :o_dt], wi[:, o_g:], pad_lanes(wi[:, o_dt:o_g], DT_PAD)], axis=1)
    head_of_lane = jnp.arange(D_INNER) // SSM_HEAD_DIM
    expand1 = (jnp.arange(DT_PAD)[:, None] == head_of_lane[None, :]).astype(BF16)
    blk = jnp.arange(MXU_DIM) // HEAD_DIM_A
    return {
        "g_mix": row(g_mix), "w_in": w_all.astype(BF16), "b_gate": row(b_gate),
        "g_q": jnp.tile(row(g_q), (1, N_HEADS_A)), "g_k": jnp.tile(row(g_k), (1, N_HEADS_A)),
        "hsum": (blk[:, None] == blk[None, :]).astype(BF16),
        "rel_bias": rel_bias[i].astype(F32),
        "conv_ssm_w": jnp.pad(conv_ssm_w[i], ((0, SUBLANES - SSM_CONV), (0, 0))), "conv_ssm_b": row(conv_ssm_b),
        "dt_bias": pad_lanes(row(dt_bias), DT_PAD), "a_log": pad_lanes(row(a_log), DT_PAD),
        "d_skip": jnp.repeat(row(d_skip), SSM_HEAD_DIM, axis=1), "g_ssm": row(g_ssm),
        "expand": jnp.concatenate([expand1] * 3, axis=0),
        "w_proj_a": w_proj_a[i].astype(BF16), "w_proj_b": w_proj_b[i].astype(BF16), "w_out": w_out[i].astype(BF16),
        "g_ffn": row(g_ffn), "w_up": w_up[i].astype(BF16),
        "ffn_conv_w": jnp.pad(ffn_conv_w[i], ((0, SUBLANES - FFN_CONV), (0, 0))), "ffn_conv_b": row(ffn_conv_b),
        "w_down": w_down[i].astype(BF16), "g_ple": row(g_ple),
        "w_ple_gate": w_ple_gate[i].astype(BF16), "w_ple": w_ple[i].astype(BF16),
    }


def kernel(x_prompt, x_sample, cache_k, cache_v, state_ssm, state_conv_ssm, state_conv_ffn, p_prompt, p_sample, g_mix, w_in, b_gate, g_q, g_k, rel_bias, conv_ssm_w, conv_ssm_b, dt_bias, a_log, d_skip, g_ssm, w_proj_a, w_proj_b, w_out, g_ffn, w_up, ffn_conv_w, ffn_conv_b, w_down, g_ple, w_ple_gate, w_ple):
    depth = w_in.shape[0]
    bp = x_prompt.shape[0]
    bs = x_sample.shape[0]
    yp, ys = x_prompt, x_sample
    sp, ss = [], []
    for i in range(depth):
        w = _prep_weights(i, g_mix, w_in, b_gate, g_q, g_k, rel_bias, conv_ssm_w, conv_ssm_b, dt_bias, a_log,
                          d_skip, g_ssm, w_proj_a, w_proj_b, w_out, g_ffn, w_up, ffn_conv_w, ffn_conv_b, w_down,
                          g_ple, w_ple_gate, w_ple)
        yp, st_p = _layer(yp, p_prompt[i], None,
                          jnp.zeros((bp, SSM_CONV - 1, CONV_DIM), F32), None,
                          jnp.zeros((bp, FFN_CONV - 1, 2 * D_FF), F32), w, tm=256, tq=4 * CHUNK)
        h0 = state_ssm[i].reshape(bs, D_INNER, D_STATE)
        ys, st_s = _layer(ys, p_sample[i], (cache_k[i], cache_v[i]), state_conv_ssm[i], h0,
                          state_conv_ffn[i], w, tm=x_sample.shape[1] * bs, tq=x_sample.shape[1])
        sp.append(st_p)
        ss.append(st_s)

    stack = lambda lst, j: jnp.stack([s[j] for s in lst], axis=0)
    return (yp, ys,
            stack(sp, 0), stack(sp, 1), stack(sp, 2), stack(sp, 3), stack(sp, 4),
            stack(ss, 0), stack(ss, 1), stack(ss, 2), stack(ss, 3), stack(ss, 4))
```

```python
import functools
import math

import jax
import jax.numpy as jnp
from jax import lax
from jax.experimental import pallas as pl
from jax.experimental.pallas import tpu as pltpu

F32 = jnp.float32
BF16 = jnp.bfloat16

D_MODEL = 1024
CHUNK = 64
PLE_DIM = 256
EPS = 1e-6
NEG_INF = -1e30
N_HEADS_A = 16
HEAD_DIM_A = 64
ATT_WIDTH = N_HEADS_A * HEAD_DIM_A
ATT_PAST = 8 * CHUNK
REL_CLIP = 128
D_INNER = 2 * D_MODEL
SSM_HEAD_DIM = 64
N_HEADS_S = D_INNER // SSM_HEAD_DIM
N_GROUPS = 4
GROUP_WIDTH = D_INNER // N_GROUPS
D_STATE = 128
SSM_CONV = 4
CONV_DIM = D_INNER + 2 * N_GROUPS * D_STATE
D_FF = 3 * D_MODEL
FFN_CONV = 3

LANES = 128
SUBLANES = 8
MXU_DIM = 256
VMEM_LIMIT_BYTES = 56 * 1024 * 1024

DT_PAD = LANES
HEADS_PER_MXU_TILE = MXU_DIM // SSM_HEAD_DIM
FFN_COL_BLOCK = 512

_SEG_Q = (0, ATT_WIDTH)
_SEG_K = (ATT_WIDTH, 2 * ATT_WIDTH)
_SEG_V = (2 * ATT_WIDTH, 3 * ATT_WIDTH)
_SEG_Z = (3 * ATT_WIDTH, 3 * ATT_WIDTH + D_INNER)
_SEG_XBC = (_SEG_Z[1], _SEG_Z[1] + CONV_DIM)
_SEG_DT = (_SEG_XBC[1], _SEG_XBC[1] + N_HEADS_S)
_SEG_GATES = (_SEG_DT[1], _SEG_DT[1] + 2 * D_MODEL)


def _resident(shape):
    nd = len(shape)
    return pl.BlockSpec(shape, lambda *_: (0,) * nd, pipeline_mode=pl.Buffered(1))


def _params(semantics):
    return pltpu.CompilerParams(dimension_semantics=semantics, vmem_limit_bytes=VMEM_LIMIT_BYTES)


def _rms_scale(x):
    return lax.rsqrt(jnp.mean(x * x, axis=-1, keepdims=True) + EPS)


def _inproj_kernel(x_ref, g_ref, w_ref, wgates_ref, wdt_ref, gq_ref, gk_ref, hsum_ref,
                   q_out, k_out, v_out, z_out, xbc_out, gates_out, dt_out, ktail_out, vtail_out,
                   *, tiles_per_batch, tail_tiles):
    x = x_ref[...]
    h = (x * _rms_scale(x) * g_ref[...]).astype(BF16)

    def seg(bounds):
        return jnp.dot(h, w_ref[:, bounds[0]:bounds[1]], preferred_element_type=F32)

    def head_norm(t, g):
        sq = (t * t).astype(BF16)
        parts = []
        for blk in range(ATT_WIDTH // MXU_DIM):
            cols = slice(blk * MXU_DIM, (blk + 1) * MXU_DIM)
            ss = jnp.dot(sq[:, cols], hsum_ref[...], preferred_element_type=F32)
            parts.append(t[:, cols] * lax.rsqrt(ss * (1.0 / HEAD_DIM_A) + EPS))
        return jnp.concatenate(parts, axis=-1) * g

    q_out[...] = (head_norm(seg(_SEG_Q), gq_ref[...]) * (HEAD_DIM_A ** -0.5)).astype(BF16)
    k = head_norm(seg(_SEG_K), gk_ref[...])
    v = seg(_SEG_V)
    k_out[...] = k.astype(BF16)
    v_out[...] = v.astype(BF16)
    z_out[...] = seg(_SEG_Z).astype(BF16)
    xbc_out[...] = seg(_SEG_XBC)
    gates_out[...] = jnp.dot(h, wgates_ref[...], preferred_element_type=F32).astype(BF16)
    dt_out[...] = jnp.dot(h, wdt_ref[...], preferred_element_type=F32)

    @pl.when(pl.program_id(0) % tiles_per_batch >= tiles_per_batch - tail_tiles)
    def _emit_tail():
        ktail_out[...] = k
        vtail_out[...] = v


def _inproj(x2d, g_mix, w_in, w_gates, w_dt, gq, gk, hsum, tm, seq):
    t = x2d.shape[0]
    tiles_per_batch = max(seq // tm, 1)
    tail_tiles = max(min(ATT_PAST, seq) // tm, 1)
    n_tail_rows = (t // tm) // tiles_per_batch * tail_tiles * tm
    row =lambda w: pl.BlockSpec((tm, w), lambda i: (i, 0))
    first_tail = tiles_per_batch - tail_tiles
    tail = pl.BlockSpec((tm, ATT_WIDTH), lambda i: (
        (i // tiles_per_batch) * tail_tiles + jnp.maximum(i % tiles_per_batch - first_tail, 0), 0))
    outs = [(ATT_WIDTH, BF16), (ATT_WIDTH, BF16), (ATT_WIDTH, BF16), (D_INNER, BF16),
            (CONV_DIM, F32), (2 * D_MODEL, BF16), (DT_PAD, F32)]
    return pl.pallas_call(
        functools.partial(_inproj_kernel, tiles_per_batch=tiles_per_batch, tail_tiles=tail_tiles),
        grid=(t // tm,),
        in_specs=[row(D_MODEL), _resident((1, D_MODEL)), _resident((D_MODEL, _SEG_XBC[1])),
                  _resident(w_gates.shape), _resident(w_dt.shape),
                  _resident((1, ATT_WIDTH)), _resident((1, ATT_WIDTH)), _resident((MXU_DIM, MXU_DIM))],
        out_specs=[row(w) for w, _ in outs] + [tail, tail],
        out_shape=[jax.ShapeDtypeStruct((t, w), d) for w, d in outs]
        + [jax.ShapeDtypeStruct((n_tail_rows, ATT_WIDTH), F32)] * 2,
        compiler_params=_params(("arbitrary",)),
        name="inproj",
    )(x2d, g_mix, w_in, w_gates, w_dt, gq, gk, hsum)


HEADS_PER_ATT_BLOCK = MXU_DIM // HEAD_DIM_A
HEADS_PER_LANE_TILE = LANES // HEAD_DIM_A


def _attn_kernel(*refs, has_cache, seq, qb):
    if has_cache:
        q_ref, k_ref, v_ref, bias_ref, ck_ref, cv_ref, o_ref, kp, vp = refs
    else:
        q_ref, k_ref, v_ref, bias_ref, o_ref, kp, vp = refs
    nk = ATT_PAST + qb

    if has_cache:
        kp[0:ATT_PAST, :] = ck_ref[0].astype(BF16)
        vp[0:ATT_PAST, :] = cv_ref[0].astype(BF16)
    kp[ATT_PAST:ATT_PAST + seq, :] = k_ref[0]
    vp[ATT_PAST:ATT_PAST + seq, :] = v_ref[0]

    lane = lax.broadcasted_iota(jnp.int32, (qb, LANES), 1)
    low_half = lane < HEAD_DIM_A

    def q_block(start, first_col):
        kb = kp[pl.ds(start + first_col, nk - first_col), :]
        vb = vp[pl.ds(start + first_col, nk - first_col), :]
        q = q_ref[0, pl.ds(start, qb), :]
        outs = []
        for pair in range(HEADS_PER_ATT_BLOCK // HEADS_PER_LANE_TILE):
            cols = slice(pair * LANES, (pair + 1) * LANES)
            qp, kpair, vpair = q[:, cols], kb[:, cols], vb[:, cols]
            halves = []
            for sub in range(HEADS_PER_LANE_TILE):
                keep = low_half if sub == 0 else jnp.logical_not(low_half)
                qh = jnp.where(keep, qp, jnp.zeros((), BF16))
                s = lax.dot_general(qh, kpair, (((1,), (1,)), ((), ())), preferred_element_type=F32)
                s = s + bias_ref[pair * HEADS_PER_LANE_TILE + sub, :, first_col:]
                m = jnp.max(s, axis=-1, keepdims=True)
                p = jnp.exp(s - m)
                denom = jnp.sum(p, axis=-1, keepdims=True)
                halves.append(jnp.dot(p.astype(BF16), vpair, preferred_element_type=F32) / denom)
            outs.append(jnp.where(low_half, halves[0], halves[1]))
        o_ref[0, pl.ds(start, qb), :] = jnp.concatenate(outs, axis=-1).astype(BF16)

    n_blocks = seq // qb
    n_peeled = 0 if has_cache else min(ATT_PAST // qb, n_blocks)
    for i in range(n_peeled):
        q_block(i * qb, ATT_PAST - i * qb)

    def steady(i, carry):
        q_block(pl.multiple_of(i * qb, qb), 0)
        return carry

    lax.fori_loop(n_peeled, n_blocks, steady, 0)


def _attention(q, k, v, bias, cache, qb):
    b, seq, _ = q.shape
    nk = ATT_PAST + qb
    has_cache = cache is not None
    hb = ATT_WIDTH // MXU_DIM
    seq_spec = pl.BlockSpec((1, seq, MXU_DIM), lambda bi, hi: (bi, 0, hi))
    in_specs = [seq_spec, seq_spec, seq_spec,
                pl.BlockSpec((HEADS_PER_ATT_BLOCK, qb, nk), lambda bi, hi: (hi, 0, 0))]
    args = [q, k, v, bias]
    if has_cache:
        cspec = pl.BlockSpec((1, ATT_PAST, MXU_DIM), lambda bi, hi: (bi, 0, hi))
        in_specs += [cspec, cspec]
        args += list(cache)
    return pl.pallas_call(
        functools.partial(_attn_kernel, has_cache=has_cache, seq=seq, qb=qb),
        grid=(b, hb),
        in_specs=in_specs,
        out_specs=seq_spec,
        out_shape=jax.ShapeDtypeStruct((b, seq, ATT_WIDTH), BF16),
        scratch_shapes=[pltpu.VMEM((ATT_PAST + seq, MXU_DIM), BF16),
                        pltpu.VMEM((ATT_PAST + seq, MXU_DIM), BF16)],
        compiler_params=_params(("parallel", "parallel")),
        name="band_attention",
    )(*args)


def _attention_bias(table, qb):
    nk = ATT_PAST + qb
    n_diag = qb + nk - 1
    diag_rel = (ATT_PAST + qb - 1) - jnp.arange(n_diag)
    r = table[:, jnp.clip(diag_rel, -REL_CLIP, REL_CLIP) + REL_CLIP]
    h = table.shape[0]
    rp = jnp.pad(r, ((0, 0), (0, 1)))
    skew = jnp.tile(rp, (1, qb))[:, :qb * n_diag].reshape(h, qb, n_diag)
    bias = skew[:, :, qb - 1:qb - 1 + nk]
    qc = jnp.arange(qb)[:, None] // CHUNK
    kc = jnp.arange(nk)[None, :] // CHUNK
    visible = (kc >= qc) & (kc <= qc + ATT_PAST // CHUNK)
    return jnp.where(visible[None], bias, NEG_INF).astype(F32)


def _softplus(x):
    return jnp.maximum(x, 0.0) + jnp.log(1.0 + jnp.exp(-jnp.abs(x)))


def _split3(v):
    hi = v.astype(BF16)
    r1 = v - hi.astype(F32)
    mid = r1.astype(BF16)
    lo = (r1 - mid.astype(F32)).astype(BF16)
    return jnp.concatenate([hi, mid, lo], axis=-1)


def _ssd_kernel(*refs, has_state, valid_rows, n_chunks):
    if has_state:
        (xbc_ref, dt_ref, z_ref, ctx_ref, h0_ref, cw_ref, cb_ref, dtb_ref, alog_ref, dskip_ref,
         gssm_ref, expand_ref, yb_out, st_out, ctx_out, state, xe, ybuf) = refs
    else:
        (xbc_ref, dt_ref, z_ref, ctx_ref, cw_ref, cb_ref, dtb_ref, alog_ref, dskip_ref,
         gssm_ref, expand_ref, yb_out, st_out, ctx_out, state, xe, ybuf) = refs
    q = CHUNK
    c = pl.program_id(1)

    @pl.when(c == 0)
    def _load_carry():
        for k in range(SSM_CONV - 1):
            xe[k, 0:SUBLANES, :] = ctx_ref[0, k * SUBLANES:(k + 1) * SUBLANES, :]
        for g in range(N_GROUPS):
            if has_state:
                state[g] = h0_ref[0, g * GROUP_WIDTH:(g + 1) * GROUP_WIDTH, :].T
            else:
                state[g] = jnp.zeros((D_STATE, GROUP_WIDTH), F32)

    x_raw = xbc_ref[0]
    row_in_slab = lax.broadcasted_iota(jnp.int32, (q, CONV_DIM), 0) % SUBLANES
    acc = cb_ref[...] + x_raw * cw_ref[SSM_CONV - 1:SSM_CONV, :]
    for k in range(1, SSM_CONV):
        rot = pltpu.roll(x_raw.reshape(q // SUBLANES, SUBLANES, CONV_DIM), k, 1).reshape(q, CONV_DIM)
        xe[k - 1, SUBLANES:SUBLANES + q, :] = rot
        shifted = jnp.where(row_in_slab < k, xe[k - 1, 0:q, :], rot)
        acc = acc + shifted * cw_ref[SSM_CONV - 1 - k:SSM_CONV - k, :]
    xc = acc * jax.nn.sigmoid(acc)
    for k in range(SSM_CONV - 1):
        xe[k, 0:SUBLANES, :] = xe[k, valid_rows:valid_rows + SUBLANES, :]

    xs = xc[:, :D_INNER]
    bm = xc[:, D_INNER:D_INNER + N_GROUPS * D_STATE]
    cm = xc[:, D_INNER + N_GROUPS * D_STATE:].astype(BF16)

    dt = _softplus(dt_ref[0] + dtb_ref[...])
    if valid_rows < q:
        rows = lax.broadcasted_iota(jnp.int32, (q, DT_PAD), 0)
        dt = jnp.where(rows < valid_rows, dt, 0.0)
    ac = dt * (-jnp.exp(alog_ref[...]))
    ri = lax.broadcasted_iota(jnp.int32, (q, q), 0)
    ci = lax.broadcasted_iota(jnp.int32, (q, q), 1)
    acs = jnp.dot((ri >= ci).astype(F32), ac, preferred_element_type=F32,
                  precision=lax.Precision.HIGHEST)

    expand = lambda v: jnp.dot(_split3(v), expand_ref[...], preferred_element_type=F32)
    acol = expand(acs)
    dtcol = expand(dt)
    last = acol[q - 1:q, :]
    dx = dtcol * xs
    wj = (jnp.exp(last - acol) * dx).astype(BF16)
    ea = jnp.exp(acol)

    lane = lax.broadcasted_iota(jnp.int32, (q, D_INNER), 1)
    rowi = lax.broadcasted_iota(jnp.int32, (q, D_INNER), 0)
    j_of_lane = lane % q
    arow = jnp.sum(jnp.where(j_of_lane == rowi, acol, 0.0), axis=0, keepdims=True)

    tri = (lax.broadcasted_iota(jnp.int32, (q, MXU_DIM), 0)
           >= lax.broadcasted_iota(jnp.int32, (q, MXU_DIM), 1) % q)
    br = lax.broadcasted_iota(jnp.int32, (MXU_DIM, MXU_DIM), 0) // SSM_HEAD_DIM
    bc = lax.broadcasted_iota(jnp.int32, (MXU_DIM, MXU_DIM), 1) // SSM_HEAD_DIM
    blockdiag = br == bc

    for g in range(N_GROUPS):
        cg = cm[:, g * D_STATE:(g + 1) * D_STATE]
        bg = bm[:, g * D_STATE:(g + 1) * D_STATE].astype(BF16)
        bg4 = jnp.concatenate([bg] * HEADS_PER_MXU_TILE, axis=0)
        cb4 = lax.dot_general(cg, bg4, (((1,), (1,)), ((), ())), preferred_element_type=F32)
        sg = state[g].astype(BF16)
        for half in range(GROUP_WIDTH // MXU_DIM):
            lo = g * GROUP_WIDTH + half * MXU_DIM
            cols = slice(lo, lo + MXU_DIM)
            seg = acol[:, cols] - arow[:, cols]
            decay = jnp.where(tri, jnp.exp(jnp.where(tri, seg, 0.0)), 0.0)
            mp = (cb4 * decay).astype(BF16)
            dxb = dx[:, cols].astype(BF16)
            bd = jnp.where(blockdiag, jnp.concatenate([dxb] * HEADS_PER_MXU_TILE, axis=0),
                           jnp.zeros((), BF16))
            y = jnp.dot(mp, bd, preferred_element_type=F32)
            y = y + jnp.dot(cg, sg[:, half * MXU_DIM:(half + 1) * MXU_DIM],
                            preferred_element_type=F32) * ea[:, cols]
            ybuf[:, cols] = y + dskip_ref[:, cols] * xs[:, cols]

    el = jnp.exp(last)
    for g in range(N_GROUPS):
        cols = slice(g * GROUP_WIDTH, (g + 1) * GROUP_WIDTH)
        bgt = bm[:, g * D_STATE:(g + 1) * D_STATE].T.astype(BF16)
        state[g] = state[g] * el[:, cols] + jnp.dot(bgt, wj[:, cols], preferred_element_type=F32)

    zf = z_ref[0].astype(F32)
    yg = ybuf[...] * (zf * jax.nn.sigmoid(zf))
    yb_out[0] = (yg * _rms_scale(yg) * gssm_ref[...]).astype(BF16)

    @pl.when(c == n_chunks - 1)
    def _emit_state():
        ctx_out[0] = xe[SSM_CONV - 2, 0:SUBLANES, :]
        for g in range(N_GROUPS):
            st_out[0, g * GROUP_WIDTH:(g + 1) * GROUP_WIDTH, :] = state[g].T


def _ssd(xbc, dt_raw, z, ctx, h0, prm, valid_rows):
    b, seq, _ = xbc.shape
    n_chunks = seq // CHUNK
    has_state = h0 is not None
    heads = [jnp.pad(ctx[:, SSM_CONV - 1 - k:], ((0, 0), (0, SUBLANES - k), (0, 0))) for k in range(1, SSM_CONV)]
    chunk = lambda w: pl.BlockSpec((1, CHUNK, w), lambda bi, ci: (bi, ci, 0))
    per_batch = lambda r, w: pl.BlockSpec((1, r, w), lambda bi, ci: (bi, 0, 0))
    in_specs = [chunk(CONV_DIM), chunk(DT_PAD), chunk(D_INNER), per_batch((SSM_CONV - 1) * SUBLANES, CONV_DIM)]
    args = [xbc, dt_raw, z, jnp.concatenate(heads, axis=1)]
    if has_state:
        in_specs.append(per_batch(D_INNER, D_STATE))
        args.append(h0)
    in_specs += [_resident(p.shape) for p in prm]
    args += list(prm)
    return pl.pallas_call(
        functools.partial(_ssd_kernel, has_state=has_state, valid_rows=valid_rows, n_chunks=n_chunks),
        grid=(b, n_chunks),
        in_specs=in_specs,
        out_specs=[chunk(D_INNER), per_batch(D_INNER, D_STATE), per_batch(SUBLANES, CONV_DIM)],
        out_shape=[jax.ShapeDtypeStruct((b, seq, D_INNER), BF16),
                   jax.ShapeDtypeStruct((b, D_INNER, D_STATE), F32),
                   jax.ShapeDtypeStruct((b, SUBLANES, CONV_DIM), F32)],
        scratch_shapes=[pltpu.VMEM((N_GROUPS, D_STATE, GROUP_WIDTH), F32),
                        pltpu.VMEM((SSM_CONV - 1, SUBLANES + CHUNK, CONV_DIM), F32),
                        pltpu.VMEM((CHUNK, D_INNER), F32)],
        compiler_params=_params(("parallel", "arbitrary")),
        name="ssd",
    )(*args)


def _merge_kernel(x_ref, ya_ref, yb_ref, gates_ref, bg_ref, wa_ref, wb_ref, wo_ref, o_ref):
    gate = jax.nn.sigmoid(gates_ref[...].astype(F32) + bg_ref[...])
    pa = jnp.dot(ya_ref[...], wa_ref[...], preferred_element_type=F32)
    pb = jnp.dot(yb_ref[...], wb_ref[...], preferred_element_type=F32)
    t = gate[:, :D_MODEL] * pa + gate[:, D_MODEL:] * pb
    o_ref[...] = x_ref[...] + jnp.dot(t.astype(BF16), wo_ref[...], preferred_element_type=F32)


def _merge(x2d, ya, yb, gates, b_gate, wa, wb, wo, tm):
    t = x2d.shape[0]
    row = lambda w: pl.BlockSpec((tm, w), lambda i: (i, 0))
    return pl.pallas_call(
        _merge_kernel,
        grid=(t // tm,),
        in_specs=[row(D_MODEL), row(ATT_WIDTH), row(D_INNER), row(2 * D_MODEL),
                  _resident(b_gate.shape), _resident(wa.shape), _resident(wb.shape), _resident(wo.shape)],
        out_specs=row(D_MODEL),
        out_shape=jax.ShapeDtypeStruct((t, D_MODEL), F32),
        compiler_params=_params(("parallel",)),
        name="merge_outproj",
    )(x2d, ya, yb, gates, b_gate, wa, wb, wo)


def _ffn_kernel(x_ref, pe_ref, ctx_ref, gffn_ref, wup_ref, cw_ref, cb_ref, wdown_ref,
                gple_ref, wgate_ref, wple_ref, o_ref, ctx_out, rs, *, tm, tiles_per_batch):
    i = pl.program_id(0)

    @pl.when(i % tiles_per_batch == 0)
    def _load_context():
        for k in range(FFN_CONV - 1):
            rs[k, 0:SUBLANES, :] = ctx_ref[0, k * SUBLANES:(k + 1) * SUBLANES, :]

    x = x_ref[...]
    xn = (x * _rms_scale(x) * gffn_ref[...]).astype(BF16)
    n_slabs = tm // SUBLANES
    row_in_slab = lax.broadcasted_iota(jnp.int32, (tm, FFN_COL_BLOCK), 0) % SUBLANES

    def up(j, part):
        lo = part * D_FF + j * FFN_COL_BLOCK
        return jnp.dot(xn, wup_ref[:, lo:lo + FFN_COL_BLOCK], preferred_element_type=F32)

    def conv(u, j, part):
        lo = part * D_FF + j * FFN_COL_BLOCK
        cols = slice(lo, lo + FFN_COL_BLOCK)
        out = cb_ref[:, cols] + u * cw_ref[FFN_CONV - 1:FFN_CONV, cols]
        for k in range(1, FFN_CONV):
            rot = pltpu.roll(u.reshape(n_slabs, SUBLANES, FFN_COL_BLOCK), k, 1).reshape(tm, FFN_COL_BLOCK)
            rs[k - 1, SUBLANES:SUBLANES + tm, cols] = rot
            shifted = jnp.where(row_in_slab < k, rs[k - 1, 0:tm, cols], rot)
            out = out + shifted * cw_ref[FFN_CONV - 1 - k:FFN_CONV - k, cols]
        return out

    n_blocks = D_FF // FFN_COL_BLOCK
    acc = jnp.zeros((tm, D_MODEL), F32)
    pending = (up(0, 0), up(0, 1))
    for j in range(n_blocks):
        ug, uv = pending
        if j + 1 < n_blocks:
            pending = (up(j + 1, 0), up(j + 1, 1))
        hidden = (jax.nn.gelu(conv(ug, j, 0), approximate=True) * conv(uv, j, 1)).astype(BF16)
        acc = acc + jnp.dot(hidden, wdown_ref[j * FFN_COL_BLOCK:(j + 1) * FFN_COL_BLOCK, :],
                            preferred_element_type=F32)

    @pl.when(i % tiles_per_batch == tiles_per_batch - 1)
    def _emit_context():
        ctx_out[0] = rs[FFN_CONV - 2, tm:tm + SUBLANES, :]

    for k in range(FFN_CONV - 1):
        rs[k, 0:SUBLANES, :] = rs[k, tm:tm + SUBLANES, :]

    x2 = x + acc
    gate = jax.nn.sigmoid(jnp.dot((x2 * _rms_scale(x2) * gple_ref[...]).astype(BF16), wgate_ref[...],
                                  preferred_element_type=F32))
    o_ref[...] = x2 + gate * jnp.dot(pe_ref[...].astype(BF16), wple_ref[...], preferred_element_type=F32)


def _ffn(x2d, pe2d, ctx, prm, batch, tm):
    t = x2d.shape[0]
    tiles_per_batch = t // batch // tm
    heads = [jnp.pad(ctx[:, FFN_CONV - 1 - k:], ((0, 0), (0, SUBLANES - k), (0, 0))) for k in range(1, FFN_CONV)]
    ctx_heads = jnp.concatenate(heads, axis=1)
    row = lambda w: pl.BlockSpec((tm, w), lambda i: (i, 0))
    per_batch = lambda r: pl.BlockSpec((1, r, 2 * D_FF), lambda i: (i // tiles_per_batch, 0, 0))
    y, ctx_out = pl.pallas_call(
        functools.partial(_ffn_kernel, tm=tm, tiles_per_batch=tiles_per_batch),
        grid=(t // tm,),
        in_specs=[row(D_MODEL), row(PLE_DIM), per_batch((FFN_CONV - 1) * SUBLANES)]
        + [_resident(p.shape) for p in prm],
        out_specs=[row(D_MODEL), per_batch(SUBLANES)],
        out_shape=[jax.ShapeDtypeStruct((t, D_MODEL), F32),
                   jax.ShapeDtypeStruct((batch, SUBLANES, 2 * D_FF), F32)],
        scratch_shapes=[pltpu.VMEM((FFN_CONV - 1, SUBLANES + tm, 2 * D_FF), F32)],
        compiler_params=_params(("arbitrary",)),
        name="convffn_ple",
    )(x2d, pe2d, ctx_heads, *prm)
    return y, ctx_out[:, :FFN_CONV - 1]


def _layer(x, pe, kv_cache, ssm_ctx, h0, ffn_ctx, w, tm, tq):
    b, seq, _ = x.shape
    t = b * seq
    x2d = x.reshape(t, D_MODEL)
    qn, kn, v, z, xbc, gates, dt_raw, k_tail, v_tail = _inproj(
        x2d, w["g_mix"], w["w_in"], w["w_gates"], w["w_dt"], w["g_q"], w["g_k"], w["hsum"], tm, seq)

    as3 = lambda a: a.reshape(b, seq, -1)
    cache = None
    if kv_cache is not None:
        cache = tuple(ck.reshape(b, ATT_PAST, ATT_WIDTH) for ck in kv_cache)
    ya = _attention(as3(qn), as3(kn), as3(v), _attention_bias(w["rel_bias"], tq), cache, tq)

    seq_pad = -(-seq // CHUNK) * CHUNK
    valid_rows = CHUNK - (seq_pad - seq)
    pad = lambda a: jnp.pad(as3(a), ((0, 0), (0, seq_pad - seq), (0, 0)))
    ssd_prm = (w["conv_ssm_w"], w["conv_ssm_b"], w["dt_bias"], w["a_log"], w["d_skip"], w["g_ssm"], w["expand"])
    yb, st, ctx_ssm = _ssd(pad(xbc), pad(dt_raw), pad(z), ssm_ctx, h0, ssd_prm, valid_rows)
    yb = yb[:, :seq].reshape(t, D_INNER)

    x1 = _merge(x2d, ya.reshape(t, ATT_WIDTH), yb, gates, w["b_gate"], w["w_proj_a"], w["w_proj_b"], w["w_out"], tm)
    ffn_prm = (w["g_ffn"], w["w_up"], w["ffn_conv_w"], w["ffn_conv_b"], w["w_down"],
               w["g_ple"], w["w_ple_gate"], w["w_ple"])
    tm_ffn = min(tm, seq)
    y, ctx_ffn = _ffn(x1, pe.reshape(t, PLE_DIM), ffn_ctx, ffn_prm, b, tm_ffn)

    heads = lambda a: a.reshape(b, min(seq, ATT_PAST), N_HEADS_A, HEAD_DIM_A)
    new_k, new_v = heads(k_tail), heads(v_tail)
    state = st.reshape(b, N_HEADS_S, SSM_HEAD_DIM, D_STATE)
    return y.reshape(b, seq, D_MODEL), (new_k, new_v, state,
                                        ctx_ssm[:, :SSM_CONV - 1], ctx_ffn)


def _prep_weights(i, g_mix, w_in, b_gate, g_q, g_k, rel_bias, conv_ssm_w, conv_ssm_b, dt_bias, a_log, d_skip,
                  g_ssm, w_proj_a, w_proj_b, w_out, g_ffn, w_up, ffn_conv_w, ffn_conv_b, w_down, g_ple,
                  w_ple_gate, w_ple):
    row = lambda a: a[i].reshape(1, -1).astype(F32)
    pad_lanes = lambda a, n: jnp.pad(a, ((0, 0), (0, n - a.shape[1])))
    wi = w_in[i].astype(BF16)
    head_of_lane = jnp.arange(D_INNER) // SSM_HEAD_DIM
    expand1 = (jnp.arange(DT_PAD)[:, None] == head_of_lane[None, :]).astype(BF16)
    blk = jnp.arange(MXU_DIM) // HEAD_DIM_A
    return {
        "g_mix": row(g_mix), "w_in": wi, "w_gates": wi[:, _SEG_GATES[0]:_SEG_GATES[1]],
        "w_dt": pad_lanes(wi[:, _SEG_DT[0]:_SEG_DT[1]], DT_PAD), "b_gate": row(b_gate),
        "g_q": jnp.tile(row(g_q), (1, N_HEADS_A)), "g_k": jnp.tile(row(g_k), (1, N_HEADS_A)),
        "hsum": (blk[:, None] == blk[None, :]).astype(BF16),
        "rel_bias": rel_bias[i].astype(F32),
        "conv_ssm_w": jnp.pad(conv_ssm_w[i], ((0, SUBLANES - SSM_CONV), (0, 0))), "conv_ssm_b": row(conv_ssm_b),
        "dt_bias": pad_lanes(row(dt_bias), DT_PAD), "a_log": pad_lanes(row(a_log), DT_PAD),
        "d_skip": jnp.repeat(row(d_skip), SSM_HEAD_DIM, axis=1), "g_ssm": row(g_ssm),
        "expand": jnp.concatenate([expand1] * 3, axis=0),
        "w_proj_a": w_proj_a[i].astype(BF16), "w_proj_b": w_proj_b[i].astype(BF16), "w_out": w_out[i].astype(BF16),
        "g_ffn": row(g_ffn), "w_up": w_up[i].astype(BF16),
        "ffn_conv_w": jnp.pad(ffn_conv_w[i], ((0, SUBLANES - FFN_CONV), (0, 0))), "ffn_conv_b": row(ffn_conv_b),
        "w_down": w_down[i].astype(BF16), "g_ple": row(g_ple),
        "w_ple_gate": w_ple_gate[i].astype(BF16), "w_ple": w_ple[i].astype(BF16),
    }


def kernel(x_prompt, x_sample, cache_k, cache_v, state_ssm, state_conv_ssm, state_conv_ffn, p_prompt, p_sample, g_mix, w_in, b_gate, g_q, g_k, rel_bias, conv_ssm_w, conv_ssm_b, dt_bias, a_log, d_skip, g_ssm, w_proj_a, w_proj_b, w_out, g_ffn, w_up, ffn_conv_w, ffn_conv_b, w_down, g_ple, w_ple_gate, w_ple):
    depth = w_in.shape[0]
    bp = x_prompt.shape[0]
    bs = x_sample.shape[0]
    yp, ys = x_prompt, x_sample
    sp, ss = [], []
    for i in range(depth):
        w = _prep_weights(i, g_mix, w_in, b_gate, g_q, g_k, rel_bias, conv_ssm_w, conv_ssm_b, dt_bias, a_log,
                          d_skip, g_ssm, w_proj_a, w_proj_b, w_out, g_ffn, w_up, ffn_conv_w, ffn_conv_b, w_down,
                          g_ple, w_ple_gate, w_ple)
        yp, st_p = _layer(yp, p_prompt[i], None,
                          jnp.zeros((bp, SSM_CONV - 1, CONV_DIM), F32), None,
                          jnp.zeros((bp, FFN_CONV - 1, 2 * D_FF), F32), w, tm=256, tq=4 * CHUNK)
        h0 = state_ssm[i].reshape(bs, D_INNER, D_STATE)
        ys, st_s = _layer(ys, p_sample[i], (cache_k[i], cache_v[i]), state_conv_ssm[i], h0,
                          state_conv_ffn[i], w, tm=x_sample.shape[1] * bs, tq=x_sample.shape[1])
        sp.append(st_p)
        ss.append(st_s)

    stack = lambda lst, j: jnp.stack([s[j] for s in lst], axis=0)
    return (yp, ys,
            stack(sp, 0), stack(sp, 1), stack(sp, 2), stack(sp, 3), stack(sp, 4),
            stack(ss, 0), stack(ss, 1), stack(ss, 2), stack(ss, 3), stack(ss, 4))
```

```python
import functools

import jax
import jax.numpy as jnp
from jax import lax
from jax.experimental import pallas as pl
from jax.experimental.pallas import tpu as pltpu

F32 = jnp.float32
BF16 = jnp.bfloat16

D_MODEL = 1024
CHUNK = 64
PLE_DIM = 256
EPS = 1e-6
NEG_INF = -1e30
N_HEADS_A = 16
HEAD_DIM_A = 64
ATT_WIDTH = N_HEADS_A * HEAD_DIM_A
ATT_PAST = 8 * CHUNK
REL_CLIP = 128
D_INNER = 2 * D_MODEL
SSM_HEAD_DIM = 64
N_HEADS_S = D_INNER // SSM_HEAD_DIM
N_GROUPS = 4
GROUP_WIDTH = D_INNER // N_GROUPS
D_STATE = 128
SSM_CONV = 4
CONV_DIM = D_INNER + 2 * N_GROUPS * D_STATE
D_FF = 3 * D_MODEL
FFN_CONV = 3

LANES = 128
SUBLANES = 8
MXU_DIM = 256
VMEM_LIMIT_BYTES = 56 * 1024 * 1024

DT_PAD = LANES
HEADS_PER_MXU_TILE = MXU_DIM // SSM_HEAD_DIM
FFN_COL_BLOCK = 512

_SEG_Q = (0, ATT_WIDTH)
_SEG_K = (ATT_WIDTH, 2 * ATT_WIDTH)
_SEG_V = (2 * ATT_WIDTH, 3 * ATT_WIDTH)
_SEG_Z = (3 * ATT_WIDTH, 3 * ATT_WIDTH + D_INNER)
_SEG_XBC = (_SEG_Z[1], _SEG_Z[1] + CONV_DIM)
_SEG_DT = (_SEG_XBC[1], _SEG_XBC[1] + N_HEADS_S)
_SEG_GATES = (_SEG_DT[1], _SEG_DT[1] + 2 * D_MODEL)


def _resident(shape):
    nd = len(shape)
    return pl.BlockSpec(shape, lambda *_: (0,) * nd, pipeline_mode=pl.Buffered(1))


def _params(semantics):
    return pltpu.CompilerParams(dimension_semantics=semantics, vmem_limit_bytes=VMEM_LIMIT_BYTES)


def _rms_scale(x):
    return lax.rsqrt(jnp.mean(x * x, axis=-1, keepdims=True) + EPS)


def _conv_heads(ctx, width):
    heads = [jnp.pad(ctx[:, width - 1 - k:], ((0, 0), (0, SUBLANES - k), (0, 0))) for k in range(1, width)]
    return jnp.concatenate(heads, axis=1)


def _softplus(x):
    return jnp.maximum(x, 0.0) + jnp.log(1.0 + jnp.exp(-jnp.abs(x)))


def _split3(v):
    hi = v.astype(BF16)
    r1 = v - hi.astype(F32)
    mid = r1.astype(BF16)
    lo = (r1 - mid.astype(F32)).astype(BF16)
    return jnp.concatenate([hi, mid, lo], axis=-1)


def _ssd_load_carry(ctx_ref, h0_ref, state, xe):
    for k in range(SSM_CONV - 1):
        xe[k, 0:SUBLANES, :] = ctx_ref[0, k * SUBLANES:(k + 1) * SUBLANES, :]
    for g in range(N_GROUPS):
        if h0_ref is None:
            state[g] = jnp.zeros((D_STATE, GROUP_WIDTH), F32)
        else:
            state[g] = h0_ref[0, g * GROUP_WIDTH:(g + 1) * GROUP_WIDTH, :].T


def _ssd_emit_carry(st_out, ctx_out, state, xe):
    ctx_out[0] = xe[SSM_CONV - 2, 0:SUBLANES, :]
    for g in range(N_GROUPS):
        st_out[0, g * GROUP_WIDTH:(g + 1) * GROUP_WIDTH, :] = state[g].T


def _ssd_chunk(x_raw, dt_raw, z, prm, state, xe, ybuf, valid_rows):
    cw_ref, cb_ref, dtb_ref, alog_ref, dskip_ref, gssm_ref, expand_ref = prm
    q = CHUNK

    row_in_slab = lax.broadcasted_iota(jnp.int32, (q, CONV_DIM), 0) % SUBLANES
    acc = cb_ref[...] + x_raw * cw_ref[SSM_CONV - 1:SSM_CONV, :]
    for k in range(1, SSM_CONV):
        rot = pltpu.roll(x_raw.reshape(q // SUBLANES, SUBLANES, CONV_DIM), k, 1).reshape(q, CONV_DIM)
        xe[k - 1, SUBLANES:SUBLANES + q, :] = rot
        shifted = jnp.where(row_in_slab < k, xe[k - 1, 0:q, :], rot)
        acc = acc + shifted * cw_ref[SSM_CONV - 1 - k:SSM_CONV - k, :]
    xc = acc * jax.nn.sigmoid(acc)
    for k in range(SSM_CONV - 1):
        xe[k, 0:SUBLANES, :] = xe[k, valid_rows:valid_rows + SUBLANES, :]

    xs = xc[:, :D_INNER]
    bm = xc[:, D_INNER:D_INNER + N_GROUPS * D_STATE]
    cm = xc[:, D_INNER + N_GROUPS * D_STATE:].astype(BF16)

    dt = _softplus(dt_raw + dtb_ref[...])
    if valid_rows < q:
        rows = lax.broadcasted_iota(jnp.int32, (q, DT_PAD), 0)
        dt = jnp.where(rows < valid_rows, dt, 0.0)
    ac = dt * (-jnp.exp(alog_ref[...]))
    ri = lax.broadcasted_iota(jnp.int32, (q, q), 0)
    ci = lax.broadcasted_iota(jnp.int32, (q, q), 1)
    acs = jnp.dot((ri >= ci).astype(F32), ac, preferred_element_type=F32,
                  precision=lax.Precision.HIGHEST)

    expand = lambda v: jnp.dot(_split3(v), expand_ref[...], preferred_element_type=F32)
    acol = expand(acs)
    dtcol = expand(dt)
    last = acol[q - 1:q, :]
    dx = dtcol * xs
    wj = (jnp.exp(last - acol) * dx).astype(BF16)
    ea = jnp.exp(acol)

    lane = lax.broadcasted_iota(jnp.int32, (q, D_INNER), 1)
    rowi = lax.broadcasted_iota(jnp.int32, (q, D_INNER), 0)
    arow = jnp.sum(jnp.where(lane % q == rowi, acol, 0.0), axis=0, keepdims=True)

    tri = (lax.broadcasted_iota(jnp.int32, (q, MXU_DIM), 0)
           >= lax.broadcasted_iota(jnp.int32, (q, MXU_DIM), 1) % q)
    br = lax.broadcasted_iota(jnp.int32, (MXU_DIM, MXU_DIM), 0) // SSM_HEAD_DIM
    bc = lax.broadcasted_iota(jnp.int32, (MXU_DIM, MXU_DIM), 1) // SSM_HEAD_DIM
    blockdiag = br == bc

    for g in range(N_GROUPS):
        cg = cm[:, g * D_STATE:(g + 1) * D_STATE]
        bg = bm[:, g * D_STATE:(g + 1) * D_STATE].astype(BF16)
        bg4 = jnp.concatenate([bg] * HEADS_PER_MXU_TILE, axis=0)
        cb4 = lax.dot_general(cg, bg4, (((1,), (1,)), ((), ())), preferred_element_type=F32)
        sg = state[g].astype(BF16)
        for half in range(GROUP_WIDTH // MXU_DIM):
            lo = g * GROUP_WIDTH + half * MXU_DIM
            cols = slice(lo, lo + MXU_DIM)
            seg = acol[:, cols] - arow[:, cols]
            decay = jnp.where(tri, jnp.exp(jnp.where(tri, seg, 0.0)), 0.0)
            mp = (cb4 * decay).astype(BF16)
            dxb = dx[:, cols].astype(BF16)
            bd = jnp.where(blockdiag, jnp.concatenate([dxb] * HEADS_PER_MXU_TILE, axis=0),
                           jnp.zeros((), BF16))
            y = jnp.dot(mp, bd, preferred_element_type=F32)
            y = y + jnp.dot(cg, sg[:, half * MXU_DIM:(half + 1) * MXU_DIM],
                            preferred_element_type=F32) * ea[:, cols]
            ybuf[:, cols] = y + dskip_ref[:, cols] * xs[:, cols]

    el = jnp.exp(last)
    for g in range(N_GROUPS):
        cols = slice(g * GROUP_WIDTH, (g + 1) * GROUP_WIDTH)
        bgt = bm[:, g * D_STATE:(g + 1) * D_STATE].T.astype(BF16)
        state[g] = state[g] * el[:, cols] + jnp.dot(bgt, wj[:, cols], preferred_element_type=F32)

    zf = z.astype(F32)
    yg = ybuf[...] * (zf * jax.nn.sigmoid(zf))
    return (yg * _rms_scale(yg) * gssm_ref[...]).astype(BF16)


def _ssd_scratch():
    return [pltpu.VMEM((N_GROUPS, D_STATE, GROUP_WIDTH), F32),
            pltpu.VMEM((SSM_CONV - 1, SUBLANES + CHUNK, CONV_DIM), F32),
            pltpu.VMEM((CHUNK, D_INNER), F32)]


def _ssd_kernel(*refs, has_state, valid_rows, n_chunks):
    if has_state:
        xbc_ref, dt_ref, z_ref, ctx_ref, h0_ref = refs[:5]
        refs = refs[5:]
    else:
        xbc_ref, dt_ref, z_ref, ctx_ref = refs[:4]
        h0_ref = None
        refs = refs[4:]
    prm, (yb_out, st_out, ctx_out, state, xe, ybuf) = refs[:7], refs[7:]
    c = pl.program_id(1)

    @pl.when(c == 0)
    def _load_carry():
        _ssd_load_carry(ctx_ref, h0_ref, state, xe)

    yb_out[0] = _ssd_chunk(xbc_ref[0], dt_ref[0], z_ref[0], prm, state, xe, ybuf, valid_rows)

    @pl.when(c == n_chunks - 1)
    def _emit_carry():
        _ssd_emit_carry(st_out, ctx_out, state, xe)


def _ssd(xbc, dt_raw, z, ctx, h0, prm, valid_rows):
    b, seq, _ = xbc.shape
    n_chunks = seq // CHUNK
    has_state = h0 is not None
    chunk = lambda w: pl.BlockSpec((1, CHUNK, w), lambda bi, ci: (bi, ci, 0))
    per_batch = lambda r, w: pl.BlockSpec((1, r, w), lambda bi, ci: (bi, 0, 0))
    in_specs = [chunk(CONV_DIM), chunk(DT_PAD), chunk(D_INNER), per_batch((SSM_CONV - 1) * SUBLANES, CONV_DIM)]
    args = [xbc, dt_raw, z, _conv_heads(ctx, SSM_CONV)]
    if has_state:
        in_specs.append(per_batch(D_INNER, D_STATE))
        args.append(h0)
    in_specs += [_resident(p.shape) for p in prm]
    args += list(prm)
    return pl.pallas_call(
        functools.partial(_ssd_kernel, has_state=has_state, valid_rows=valid_rows, n_chunks=n_chunks),
        grid=(b, n_chunks),
        in_specs=in_specs,
        out_specs=[chunk(D_INNER), per_batch(D_INNER, D_STATE), per_batch(SUBLANES, CONV_DIM)],
        out_shape=[jax.ShapeDtypeStruct((b, seq, D_INNER), BF16),
                   jax.ShapeDtypeStruct((b, D_INNER, D_STATE), F32),
                   jax.ShapeDtypeStruct((b, SUBLANES, CONV_DIM), F32)],
        scratch_shapes=_ssd_scratch(),
        compiler_params=_params(("parallel", "arbitrary")),
        name="ssd",
    )(*args)


def _inproj_kernel(*refs, tiles_per_batch, tail_tiles, fuse_ssd, tm):
    x_ref, g_ref, w_ref, wgates_ref, wdt_ref, gq_ref, gk_ref, hsum_ref = refs[:8]
    if fuse_ssd:
        ctx_ref, prm = refs[8], refs[9:16]
        (q_out, k_out, v_out, gates_out, ktail_out, vtail_out, yb_out, st_out, ctx_out,
         state, xe, ybuf, xraw, zbuf, dtbuf) = refs[16:]
    else:
        q_out, k_out, v_out, gates_out, ktail_out, vtail_out, z_out, xbc_out, dt_out = refs[8:]
    tile_in_batch = pl.program_id(0) % tiles_per_batch

    if fuse_ssd:
        @pl.when(tile_in_batch == 0)
        def _load_carry():
            _ssd_load_carry(ctx_ref, None, state, xe)

    x = x_ref[...]
    h = (x * _rms_scale(x) * g_ref[...]).astype(BF16)

    def seg(bounds):
        return jnp.dot(h, w_ref[:, bounds[0]:bounds[1]], preferred_element_type=F32)

    def head_norm(t, g):
        sq = (t * t).astype(BF16)
        parts = []
        for blk in range(ATT_WIDTH // MXU_DIM):
            cols = slice(blk * MXU_DIM, (blk + 1) * MXU_DIM)
            ss = jnp.dot(sq[:, cols], hsum_ref[...], preferred_element_type=F32)
            parts.append(t[:, cols] * lax.rsqrt(ss * (1.0 / HEAD_DIM_A) + EPS))
        return jnp.concatenate(parts, axis=-1) * g

    xbc = seg(_SEG_XBC)
    zb = seg(_SEG_Z).astype(BF16)
    dt_raw = jnp.dot(h, wdt_ref[...], preferred_element_type=F32)
    if fuse_ssd:
        xraw[...] = xbc
        zbuf[...] = zb
        dtbuf[...] = dt_raw
        for c in range(tm // CHUNK):
            rows = slice(c * CHUNK, (c + 1) * CHUNK)
            yb_out[rows, :] = _ssd_chunk(xraw[rows, :], dtbuf[rows, :], zbuf[rows, :], prm, state, xe, ybuf, CHUNK)
    else:
        xbc_out[...] = xbc
        z_out[...] = zb
        dt_out[...] = dt_raw

    q_out[...] = (head_norm(seg(_SEG_Q), gq_ref[...]) * (HEAD_DIM_A ** -0.5)).astype(BF16)
    k = head_norm(seg(_SEG_K), gk_ref[...])
    v = seg(_SEG_V)
    k_out[...] = k.astype(BF16)
    v_out[...] = v.astype(BF16)
    gates_out[...] = jnp.dot(h, wgates_ref[...], preferred_element_type=F32).astype(BF16)

    @pl.when(tile_in_batch >= tiles_per_batch - tail_tiles)
    def _emit_tail():
        ktail_out[...] = k
        vtail_out[...] = v

    if fuse_ssd:
        @pl.when(tile_in_batch == tiles_per_batch - 1)
        def _emit_carry():
            _ssd_emit_carry(st_out, ctx_out, state, xe)


def _inproj(x2d, w, tm, seq, ssm_ctx=None):
    t = x2d.shape[0]
    fuse_ssd = ssm_ctx is not None
    tiles_per_batch = max(seq // tm, 1)
    tail_tiles = max(min(ATT_PAST, seq) // tm, 1)
    n_batches = (t // tm) // tiles_per_batch
    row = lambda width: pl.BlockSpec((tm, width), lambda i: (i, 0))
    first_tail = tiles_per_batch - tail_tiles
    tail = pl.BlockSpec((tm, ATT_WIDTH), lambda i: (
        (i // tiles_per_batch) * tail_tiles + jnp.maximum(i % tiles_per_batch - first_tail, 0), 0))
    per_batch = lambda r, width: pl.BlockSpec((1, r, width), lambda i: (i // tiles_per_batch, 0, 0))
    rows_out = lambda width, dtype: (row(width), jax.ShapeDtypeStruct((t, width), dtype))
    tail_out = (tail, jax.ShapeDtypeStruct((n_batches * tail_tiles * tm, ATT_WIDTH), F32))

    in_specs = [row(D_MODEL), _resident((1, D_MODEL)), _resident((D_MODEL, _SEG_XBC[1])),
                _resident(w["w_gates"].shape), _resident(w["w_dt"].shape),
                _resident((1, ATT_WIDTH)), _resident((1, ATT_WIDTH)), _resident((MXU_DIM, MXU_DIM))]
    args = [x2d, w["g_mix"], w["w_in"], w["w_gates"], w["w_dt"], w["g_q"], w["g_k"], w["hsum"]]
    outs = [rows_out(ATT_WIDTH, BF16), rows_out(ATT_WIDTH, BF16), rows_out(ATT_WIDTH, BF16),
            rows_out(2 * D_MODEL, BF16), tail_out, tail_out]
    scratch = []
    if fuse_ssd:
        assert seq % tm == 0 and tm % CHUNK == 0
        prm = _ssd_params(w)
        in_specs += [per_batch((SSM_CONV - 1) * SUBLANES, CONV_DIM)] + [_resident(p.shape) for p in prm]
        args += [_conv_heads(ssm_ctx, SSM_CONV)] + list(prm)
        outs += [rows_out(D_INNER, BF16),
                 (per_batch(D_INNER, D_STATE), jax.ShapeDtypeStruct((n_batches, D_INNER, D_STATE), F32)),
                 (per_batch(SUBLANES, CONV_DIM), jax.ShapeDtypeStruct((n_batches, SUBLANES, CONV_DIM), F32))]
        scratch = _ssd_scratch() + [pltpu.VMEM((tm, CONV_DIM), F32), pltpu.VMEM((tm, D_INNER), BF16),
                                    pltpu.VMEM((tm, DT_PAD), F32)]
    else:
        outs += [rows_out(D_INNER, BF16), rows_out(CONV_DIM, F32), rows_out(DT_PAD, F32)]
    return pl.pallas_call(
        functools.partial(_inproj_kernel, tiles_per_batch=tiles_per_batch, tail_tiles=tail_tiles,
                          fuse_ssd=fuse_ssd, tm=tm),
        grid=(t // tm,),
        in_specs=in_specs,
        out_specs=[o[0] for o in outs],
        out_shape=[o[1] for o in outs],
        scratch_shapes=scratch,
        compiler_params=_params(("arbitrary",)),
        name="inproj_ssd" if fuse_ssd else "inproj",
    )(*args)


HEADS_PER_ATT_BLOCK = MXU_DIM // HEAD_DIM_A
HEADS_PER_LANE_TILE = LANES // HEAD_DIM_A


def _attn_kernel(*refs, has_cache, seq, qb):
    if has_cache:
        q_ref, k_ref, v_ref, bias_ref, ck_ref, cv_ref, o_ref, kp, vp = refs
    else:
        q_ref, k_ref, v_ref, bias_ref, o_ref, kp, vp = refs
    nk = ATT_PAST + qb

    if has_cache:
        kp[0:ATT_PAST, :] = ck_ref[0].astype(BF16)
        vp[0:ATT_PAST, :] = cv_ref[0].astype(BF16)
    kp[ATT_PAST:ATT_PAST + seq, :] = k_ref[0]
    vp[ATT_PAST:ATT_PAST + seq, :] = v_ref[0]

    lane = lax.broadcasted_iota(jnp.int32, (qb, LANES), 1)
    low_half = lane < HEAD_DIM_A

    def q_block(start, first_col):
        kb = kp[pl.ds(start + first_col, nk - first_col), :]
        vb = vp[pl.ds(start + first_col, nk - first_col), :]
        q = q_ref[0, pl.ds(start, qb), :]
        outs = []
        for pair in range(HEADS_PER_ATT_BLOCK // HEADS_PER_LANE_TILE):
            cols = slice(pair * LANES, (pair + 1) * LANES)
            qp, kpair, vpair = q[:, cols], kb[:, cols], vb[:, cols]
            halves = []
            for sub in range(HEADS_PER_LANE_TILE):
                keep = low_half if sub == 0 else jnp.logical_not(low_half)
                qh = jnp.where(keep, qp, jnp.zeros((), BF16))
                s = lax.dot_general(qh, kpair, (((1,), (1,)), ((), ())), preferred_element_type=F32)
                s = s + bias_ref[pair * HEADS_PER_LANE_TILE + sub, :, first_col:]
                m = jnp.max(s, axis=-1, keepdims=True)
                p = jnp.exp(s - m)
                denom = jnp.sum(p, axis=-1, keepdims=True)
                halves.append(jnp.dot(p.astype(BF16), vpair, preferred_element_type=F32) / denom)
            outs.append(jnp.where(low_half, halves[0], halves[1]))
        o_ref[0, pl.ds(start, qb), :] = jnp.concatenate(outs, axis=-1).astype(BF16)

    n_blocks = seq // qb
    n_peeled = 0 if has_cache else min(ATT_PAST // qb, n_blocks)
    for i in range(n_peeled):
        q_block(i * qb, ATT_PAST - i * qb)

    n_steady = n_blocks - n_peeled
    unroll = 2 if n_steady % 2 == 0 and n_steady > 0 else 1

    def steady(i, carry):
        for u in range(unroll):
            q_block(pl.multiple_of((n_peeled + i * unroll + u) * qb, qb), 0)
        return carry

    lax.fori_loop(0, n_steady // unroll, steady, 0)


def _attention(q, k, v, bias, cache, qb):
    b, seq, _ = q.shape
    nk = ATT_PAST + qb
    has_cache = cache is not None
    hb = ATT_WIDTH // MXU_DIM
    seq_spec = pl.BlockSpec((1, seq, MXU_DIM), lambda bi, hi: (bi, 0, hi))
    in_specs = [seq_spec, seq_spec, seq_spec,
                pl.BlockSpec((HEADS_PER_ATT_BLOCK, qb, nk), lambda bi, hi: (hi, 0, 0))]
    args = [q, k, v, bias]
    if has_cache:
        cspec = pl.BlockSpec((1, ATT_PAST, MXU_DIM), lambda bi, hi: (bi, 0, hi))
        in_specs += [cspec, cspec]
        args += list(cache)
    return pl.pallas_call(
        functools.partial(_attn_kernel, has_cache=has_cache, seq=seq, qb=qb),
        grid=(b, hb),
        in_specs=in_specs,
        out_specs=seq_spec,
        out_shape=jax.ShapeDtypeStruct((b, seq, ATT_WIDTH), BF16),
        scratch_shapes=[pltpu.VMEM((ATT_PAST + seq, MXU_DIM), BF16),
                        pltpu.VMEM((ATT_PAST + seq, MXU_DIM), BF16)],
        compiler_params=_params(("parallel", "parallel")),
        name="band_attention",
    )(*args)


def _attention_bias(table, qb):
    nk = ATT_PAST + qb
    n_diag = qb + nk - 1
    diag_rel = (ATT_PAST + qb - 1) - jnp.arange(n_diag)
    r = table[:, jnp.clip(diag_rel, -REL_CLIP, REL_CLIP) + REL_CLIP]
    h = table.shape[0]
    rp = jnp.pad(r, ((0, 0), (0, 1)))
    skew = jnp.tile(rp, (1, qb))[:, :qb * n_diag].reshape(h, qb, n_diag)
    bias = skew[:, :, qb - 1:qb - 1 + nk]
    qc = jnp.arange(qb)[:, None] // CHUNK
    kc = jnp.arange(nk)[None, :] // CHUNK
    visible = (kc >= qc) & (kc <= qc + ATT_PAST // CHUNK)
    return jnp.where(visible[None], bias, NEG_INF).astype(F32)


def _merge_kernel(x_ref, ya_ref, yb_ref, gates_ref, bg_ref, wa_ref, wb_ref, wo_ref, o_ref):
    gate = jax.nn.sigmoid(gates_ref[...].astype(F32) + bg_ref[...])
    pa = jnp.dot(ya_ref[...], wa_ref[...], preferred_element_type=F32)
    pb = jnp.dot(yb_ref[...], wb_ref[...], preferred_element_type=F32)
    t = gate[:, :D_MODEL] * pa + gate[:, D_MODEL:] * pb
    o_ref[...] = x_ref[...] + jnp.dot(t.astype(BF16), wo_ref[...], preferred_element_type=F32)


def _merge(x2d, ya, yb, gates, b_gate, wa, wb, wo, tm):
    t = x2d.shape[0]
    row = lambda w: pl.BlockSpec((tm, w), lambda i: (i, 0))
    return pl.pallas_call(
        _merge_kernel,
        grid=(t // tm,),
        in_specs=[row(D_MODEL), row(ATT_WIDTH), row(D_INNER), row(2 * D_MODEL),
                  _resident(b_gate.shape), _resident(wa.shape), _resident(wb.shape), _resident(wo.shape)],
        out_specs=row(D_MODEL),
        out_shape=jax.ShapeDtypeStruct((t, D_MODEL), F32),
        compiler_params=_params(("parallel",)),
        name="merge_outproj",
    )(x2d, ya, yb, gates, b_gate, wa, wb, wo)


def _ffn_kernel(x_ref, pe_ref, ctx_ref, gffn_ref, wup_ref, cw_ref, cb_ref, wdown_ref,
                gple_ref, wgate_ref, wple_ref, o_ref, ctx_out, rs, *, tm, tiles_per_batch):
    i = pl.program_id(0)

    @pl.when(i % tiles_per_batch == 0)
    def _load_context():
        for k in range(FFN_CONV - 1):
            rs[k, 0:SUBLANES, :] = ctx_ref[0, k * SUBLANES:(k + 1) * SUBLANES, :]

    x = x_ref[...]
    xn = (x * _rms_scale(x) * gffn_ref[...]).astype(BF16)
    n_slabs = tm // SUBLANES
    row_in_slab = lax.broadcasted_iota(jnp.int32, (tm, FFN_COL_BLOCK), 0) % SUBLANES

    def up(j, part):
        lo = part * D_FF + j * FFN_COL_BLOCK
        return jnp.dot(xn, wup_ref[:, lo:lo + FFN_COL_BLOCK], preferred_element_type=F32)

    def conv(u, j, part):
        lo = part * D_FF + j * FFN_COL_BLOCK
        cols = slice(lo, lo + FFN_COL_BLOCK)
        out = cb_ref[:, cols] + u * cw_ref[FFN_CONV - 1:FFN_CONV, cols]
        for k in range(1, FFN_CONV):
            rot = pltpu.roll(u.reshape(n_slabs, SUBLANES, FFN_COL_BLOCK), k, 1).reshape(tm, FFN_COL_BLOCK)
            rs[k - 1, SUBLANES:SUBLANES + tm, cols] = rot
            shifted = jnp.where(row_in_slab < k, rs[k - 1, 0:tm, cols], rot)
            out = out + shifted * cw_ref[FFN_CONV - 1 - k:FFN_CONV - k, cols]
        return out

    n_blocks = D_FF // FFN_COL_BLOCK
    acc = jnp.zeros((tm, D_MODEL), F32)
    pending = (up(0, 0), up(0, 1))
    for j in range(n_blocks):
        ug, uv = pending
        if j + 1 < n_blocks:
            pending = (up(j + 1, 0), up(j + 1, 1))
        hidden = (jax.nn.gelu(conv(ug, j, 0), approximate=True) * conv(uv, j, 1)).astype(BF16)
        acc = acc + jnp.dot(hidden, wdown_ref[j * FFN_COL_BLOCK:(j + 1) * FFN_COL_BLOCK, :],
                            preferred_element_type=F32)

    @pl.when(i % tiles_per_batch == tiles_per_batch - 1)
    def _emit_context():
        ctx_out[0] = rs[FFN_CONV - 2, tm:tm + SUBLANES, :]

    for k in range(FFN_CONV - 1):
        rs[k, 0:SUBLANES, :] = rs[k, tm:tm + SUBLANES, :]

    x2 = x + acc
    gate = jax.nn.sigmoid(jnp.dot((x2 * _rms_scale(x2) * gple_ref[...]).astype(BF16), wgate_ref[...],
                                  preferred_element_type=F32))
    o_ref[...] = x2 + gate * jnp.dot(pe_ref[...].astype(BF16), wple_ref[...], preferred_element_type=F32)


def _ffn(x2d, pe2d, ctx, prm, batch, tm):
    t = x2d.shape[0]
    tiles_per_batch = t // batch // tm
    row = lambda w: pl.BlockSpec((tm, w), lambda i: (i, 0))
    per_batch = lambda r: pl.BlockSpec((1, r, 2 * D_FF), lambda i: (i // tiles_per_batch, 0, 0))
    y, ctx_out = pl.pallas_call(
        functools.partial(_ffn_kernel, tm=tm, tiles_per_batch=tiles_per_batch),
        grid=(t // tm,),
        in_specs=[row(D_MODEL), row(PLE_DIM), per_batch((FFN_CONV - 1) * SUBLANES)]
        + [_resident(p.shape) for p in prm],
        out_specs=[row(D_MODEL), per_batch(SUBLANES)],
        out_shape=[jax.ShapeDtypeStruct((t, D_MODEL), F32),
                   jax.ShapeDtypeStruct((batch, SUBLANES, 2 * D_FF), F32)],
        scratch_shapes=[pltpu.VMEM((FFN_CONV - 1, SUBLANES + tm, 2 * D_FF), F32)],
        compiler_params=_params(("arbitrary",)),
        name="convffn_ple",
    )(x2d, pe2d, _conv_heads(ctx, FFN_CONV), *prm)
    return y, ctx_out[:, :FFN_CONV - 1]


def _ssd_params(w):
    return (w["conv_ssm_w"], w["conv_ssm_b"], w["dt_bias"], w["a_log"], w["d_skip"], w["g_ssm"], w["expand"])


def _layer(x, pe, kv_cache, ssm_ctx, h0, ffn_ctx, w, tm, tq):
    b, seq, _ = x.shape
    t = b * seq
    x2d = x.reshape(t, D_MODEL)
    as3 = lambda a: a.reshape(b, seq, -1)
    fuse_ssd = h0 is None and seq % tm == 0 and tm % CHUNK == 0
    if fuse_ssd:
        qn, kn, v, gates, k_tail, v_tail, yb, st, ctx_ssm = _inproj(x2d, w, tm, seq, ssm_ctx)
    else:
        qn, kn, v, gates, k_tail, v_tail, z, xbc, dt_raw = _inproj(x2d, w, tm, seq)
        seq_pad = -(-seq // CHUNK) * CHUNK
        valid_rows = CHUNK - (seq_pad - seq)
        pad = lambda a: jnp.pad(as3(a), ((0, 0), (0, seq_pad - seq), (0, 0)))
        yb, st, ctx_ssm = _ssd(pad(xbc), pad(dt_raw), pad(z), ssm_ctx, h0, _ssd_params(w), valid_rows)
        yb = yb[:, :seq].reshape(t, D_INNER)

    cache = None
    if kv_cache is not None:
        cache = tuple(ck.reshape(b, ATT_PAST, ATT_WIDTH) for ck in kv_cache)
    ya = _attention(as3(qn), as3(kn), as3(v), _attention_bias(w["rel_bias"], tq), cache, tq)

    x1 = _merge(x2d, ya.reshape(t, ATT_WIDTH), yb, gates, w["b_gate"], w["w_proj_a"], w["w_proj_b"], w["w_out"], tm)
    ffn_prm = (w["g_ffn"], w["w_up"], w["ffn_conv_w"], w["ffn_conv_b"], w["w_down"],
               w["g_ple"], w["w_ple_gate"], w["w_ple"])
    y, ctx_ffn = _ffn(x1, pe.reshape(t, PLE_DIM), ffn_ctx, ffn_prm, b, min(tm, seq))

    heads = lambda a: a.reshape(b, min(seq, ATT_PAST), N_HEADS_A, HEAD_DIM_A)
    state = st.reshape(b, N_HEADS_S, SSM_HEAD_DIM, D_STATE)
    return y.reshape(b, seq, D_MODEL), (heads(k_tail), heads(v_tail), state, ctx_ssm[:, :SSM_CONV - 1], ctx_ffn)


def _prep_weights(i, g_mix, w_in, b_gate, g_q, g_k, rel_bias, conv_ssm_w, conv_ssm_b, dt_bias, a_log, d_skip,
                  g_ssm, w_proj_a, w_proj_b, w_out, g_ffn, w_up, ffn_conv_w, ffn_conv_b, w_down, g_ple,
                  w_ple_gate, w_ple):
    row = lambda a: a[i].reshape(1, -1).astype(F32)
    pad_lanes = lambda a, n: jnp.pad(a, ((0, 0), (0, n - a.shape[1])))
    wi = w_in[i].astype(BF16)
    head_of_lane = jnp.arange(D_INNER) // SSM_HEAD_DIM
    expand1 = (jnp.arange(DT_PAD)[:, None] == head_of_lane[None, :]).astype(BF16)
    blk = jnp.arange(MXU_DIM) // HEAD_DIM_A
    return {
        "g_mix": row(g_mix), "w_in": wi, "w_gates": wi[:, _SEG_GATES[0]:_SEG_GATES[1]],
        "w_dt": pad_lanes(wi[:, _SEG_DT[0]:_SEG_DT[1]], DT_PAD), "b_gate": row(b_gate),
        "g_q": jnp.tile(row(g_q), (1, N_HEADS_A)), "g_k": jnp.tile(row(g_k), (1, N_HEADS_A)),
        "hsum": (blk[:, None] == blk[None, :]).astype(BF16),
        "rel_bias": rel_bias[i].astype(F32),
        "conv_ssm_w": jnp.pad(conv_ssm_w[i], ((0, SUBLANES - SSM_CONV), (0, 0))), "conv_ssm_b": row(conv_ssm_b),
        "dt_bias": pad_lanes(row(dt_bias), DT_PAD), "a_log": pad_lanes(row(a_log), DT_PAD),
        "d_skip": jnp.repeat(row(d_skip), SSM_HEAD_DIM, axis=1), "g_ssm": row(g_ssm),
        "expand": jnp.concatenate([expand1] * 3, axis=0),
        "w_proj_a": w_proj_a[i].astype(BF16), "w_proj_b": w_proj_b[i].astype(BF16), "w_out": w_out[i].astype(BF16),
        "g_ffn": row(g_ffn), "w_up": w_up[i].astype(BF16),
        "ffn_conv_w": jnp.pad(ffn_conv_w[i], ((0, SUBLANES - FFN_CONV), (0, 0))), "ffn_conv_b": row(ffn_conv_b),
        "w_down": w_down[i].astype(BF16), "g_ple": row(g_ple),
        "w_ple_gate": w_ple_gate[i].astype(BF16), "w_ple": w_ple[i].astype(BF16),
    }


def kernel(x_prompt, x_sample, cache_k, cache_v, state_ssm, state_conv_ssm, state_conv_ffn, p_prompt, p_sample, g_mix, w_in, b_gate, g_q, g_k, rel_bias, conv_ssm_w, conv_ssm_b, dt_bias, a_log, d_skip, g_ssm, w_proj_a, w_proj_b, w_out, g_ffn, w_up, ffn_conv_w, ffn_conv_b, w_down, g_ple, w_ple_gate, w_ple):
    depth = w_in.shape[0]
    bp = x_prompt.shape[0]
    bs, seq_s = x_sample.shape[:2]
    yp, ys = x_prompt, x_sample
    sp, ss = [], []
    for i in range(depth):
        w = _prep_weights(i, g_mix, w_in, b_gate, g_q, g_k, rel_bias, conv_ssm_w, conv_ssm_b, dt_bias, a_log,
                          d_skip, g_ssm, w_proj_a, w_proj_b, w_out, g_ffn, w_up, ffn_conv_w, ffn_conv_b, w_down,
                          g_ple, w_ple_gate, w_ple)
        yp, st_p = _layer(yp, p_prompt[i], None,
                          jnp.zeros((bp, SSM_CONV - 1, CONV_DIM), F32), None,
                          jnp.zeros((bp, FFN_CONV - 1, 2 * D_FF), F32), w, tm=256, tq=4 * CHUNK)
        h0 = state_ssm[i].reshape(bs, D_INNER, D_STATE)
        ys, st_s = _layer(ys, p_sample[i], (cache_k[i], cache_v[i]), state_conv_ssm[i], h0,
                          state_conv_ffn[i], w, tm=seq_s * bs, tq=seq_s)
        sp.append(st_p)
        ss.append(st_s)

    stack = lambda lst, j: jnp.stack([s[j] for s in lst], axis=0)
    return (yp, ys,
            stack(sp, 0), stack(sp, 1), stack(sp, 2), stack(sp, 3), stack(sp, 4),
            stack(ss, 0), stack(ss, 1), stack(ss, 2), stack(ss, 3), stack(ss, 4))
```

```python
import functools

import jax
import jax.numpy as jnp
from jax import lax
from jax.experimental import pallas as pl
from jax.experimental.pallas import tpu as pltpu

F32 = jnp.float32
BF16 = jnp.bfloat16

D_MODEL = 1024
CHUNK = 64
PLE_DIM = 256
EPS = 1e-6
NEG_INF = -1e30
N_HEADS_A = 16
HEAD_DIM_A = 64
ATT_WIDTH = N_HEADS_A * HEAD_DIM_A
ATT_PAST = 8 * CHUNK
REL_CLIP = 128
D_INNER = 2 * D_MODEL
SSM_HEAD_DIM = 64
N_HEADS_S = D_INNER // SSM_HEAD_DIM
N_GROUPS = 4
GROUP_WIDTH = D_INNER // N_GROUPS
D_STATE = 128
SSM_CONV = 4
CONV_DIM = D_INNER + 2 * N_GROUPS * D_STATE
D_FF = 3 * D_MODEL
FFN_CONV = 3

LANES = 128
SUBLANES = 8
MXU_DIM = 256
VMEM_LIMIT_BYTES = 56 * 1024 * 1024

DT_PAD = LANES
HEADS_PER_MXU_TILE = MXU_DIM // SSM_HEAD_DIM
FFN_COL_BLOCK = 512

_SEG_Q = (0, ATT_WIDTH)
_SEG_K = (ATT_WIDTH, 2 * ATT_WIDTH)
_SEG_V = (2 * ATT_WIDTH, 3 * ATT_WIDTH)
_SEG_Z = (3 * ATT_WIDTH, 3 * ATT_WIDTH + D_INNER)
_SEG_XBC = (_SEG_Z[1], _SEG_Z[1] + CONV_DIM)
_SEG_DT = (_SEG_XBC[1], _SEG_XBC[1] + N_HEADS_S)
_SEG_GATES = (_SEG_DT[1], _SEG_DT[1] + 2 * D_MODEL)


def _resident(shape):
    nd = len(shape)
    return pl.BlockSpec(shape, lambda *_: (0,) * nd, pipeline_mode=pl.Buffered(1))


def _params(semantics):
    return pltpu.CompilerParams(dimension_semantics=semantics, vmem_limit_bytes=VMEM_LIMIT_BYTES)


def _rms_scale(x):
    return lax.rsqrt(jnp.mean(x * x, axis=-1, keepdims=True) + EPS)


def _conv_heads(ctx, width):
    heads = [jnp.pad(ctx[:, width - 1 - k:], ((0, 0), (0, SUBLANES - k), (0, 0))) for k in range(1, width)]
    return jnp.concatenate(heads, axis=1)


def _softplus(x):
    return jnp.maximum(x, 0.0) + jnp.log(1.0 + jnp.exp(-jnp.abs(x)))


def _split3(v):
    hi = v.astype(BF16)
    r1 = v - hi.astype(F32)
    mid = r1.astype(BF16)
    lo = (r1 - mid.astype(F32)).astype(BF16)
    return jnp.concatenate([hi, mid, lo], axis=-1)


def _ssd_load_carry(ctx_ref, h0_ref, state, heads):
    for k in range(SSM_CONV - 1):
        heads[k] = ctx_ref[0, k * SUBLANES:(k + 1) * SUBLANES, :]
    for g in range(N_GROUPS):
        if h0_ref is None:
            state[g] = jnp.zeros((D_STATE, GROUP_WIDTH), F32)
        else:
            state[g] = h0_ref[0, g * GROUP_WIDTH:(g + 1) * GROUP_WIDTH, :].T


def _ssd_emit_carry(st_out, ctx_out, state, heads):
    ctx_out[0] = heads[SSM_CONV - 2]
    for g in range(N_GROUPS):
        st_out[0, g * GROUP_WIDTH:(g + 1) * GROUP_WIDTH, :] = state[g].T


def _ssd_conv(x_raw, prm, heads, valid_rows):
    cw_ref, cb_ref = prm[:2]
    n_rows = x_raw.shape[0]
    row_in_slab = lax.broadcasted_iota(jnp.int32, (n_rows, CONV_DIM), 0) % SUBLANES
    acc = cb_ref[...] + x_raw * cw_ref[SSM_CONV - 1:SSM_CONV, :]
    for k in range(1, SSM_CONV):
        rot = pltpu.roll(x_raw.reshape(n_rows // SUBLANES, SUBLANES, CONV_DIM), k, 1).reshape(n_rows, CONV_DIM)
        above = jnp.concatenate([heads[k - 1], rot[:n_rows - SUBLANES]], axis=0)
        acc = acc + jnp.where(row_in_slab < k, above, rot) * cw_ref[SSM_CONV - 1 - k:SSM_CONV - k, :]
        heads[k - 1] = rot[valid_rows - SUBLANES:valid_rows]
    return acc * jax.nn.sigmoid(acc)


def _ssd_steps(dt_raw, prm, valid_rows):
    dtb_ref, alog_ref, expand_ref = prm[2], prm[3], prm[6]
    n_rows = dt_raw.shape[0]
    dt = _softplus(dt_raw + dtb_ref[...])
    if valid_rows < n_rows:
        rows = lax.broadcasted_iota(jnp.int32, (n_rows, DT_PAD), 0)
        dt = jnp.where(rows < valid_rows, dt, 0.0)
    ac = dt * (-jnp.exp(alog_ref[...]))
    ri = lax.broadcasted_iota(jnp.int32, (n_rows, n_rows), 0)
    ci = lax.broadcasted_iota(jnp.int32, (n_rows, n_rows), 1)
    same_chunk_causal = jnp.logical_and(ri // CHUNK == ci // CHUNK, ri >= ci)
    acs = jnp.dot(same_chunk_causal.astype(F32), ac, preferred_element_type=F32,
                  precision=lax.Precision.HIGHEST)
    expand = lambda v: jnp.dot(_split3(v), expand_ref[...], preferred_element_type=F32)
    return expand(acs), expand(dt)


def _ssd_chunk(xc, acol, dtcol, z, prm, state):
    dskip_ref, gssm_ref = prm[4], prm[5]
    q = CHUNK

    xs = xc[:, :D_INNER]
    bm = xc[:, D_INNER:D_INNER + N_GROUPS * D_STATE]
    cm = xc[:, D_INNER + N_GROUPS * D_STATE:].astype(BF16)

    last = acol[q - 1:q, :]
    dx = dtcol * xs
    wj = (jnp.exp(last - acol) * dx).astype(BF16)
    ea = jnp.exp(acol)

    lane = lax.broadcasted_iota(jnp.int32, (q, D_INNER), 1)
    rowi = lax.broadcasted_iota(jnp.int32, (q, D_INNER), 0)
    arow = jnp.sum(jnp.where(lane % q == rowi, acol, 0.0), axis=0, keepdims=True)

    tri = (lax.broadcasted_iota(jnp.int32, (q, MXU_DIM), 0)
           >= lax.broadcasted_iota(jnp.int32, (q, MXU_DIM), 1) % q)
    br = lax.broadcasted_iota(jnp.int32, (MXU_DIM, MXU_DIM), 0) // SSM_HEAD_DIM
    bc = lax.broadcasted_iota(jnp.int32, (MXU_DIM, MXU_DIM), 1) // SSM_HEAD_DIM
    blockdiag = br == bc

    y_blocks = []
    for g in range(N_GROUPS):
        cg = cm[:, g * D_STATE:(g + 1) * D_STATE]
        bg = bm[:, g * D_STATE:(g + 1) * D_STATE].astype(BF16)
        bg4 = jnp.concatenate([bg] * HEADS_PER_MXU_TILE, axis=0)
        cb4 = lax.dot_general(cg, bg4, (((1,), (1,)), ((), ())), preferred_element_type=F32)
        sg = state[g].astype(BF16)
        for half in range(GROUP_WIDTH // MXU_DIM):
            lo = g * GROUP_WIDTH + half * MXU_DIM
            cols = slice(lo, lo + MXU_DIM)
            seg = acol[:, cols] - arow[:, cols]
            decay = jnp.where(tri, jnp.exp(jnp.where(tri, seg, 0.0)), 0.0)
            mp = (cb4 * decay).astype(BF16)
            dxb = dx[:, cols].astype(BF16)
            bd = jnp.where(blockdiag, jnp.concatenate([dxb] * HEADS_PER_MXU_TILE, axis=0),
                           jnp.zeros((), BF16))
            y = jnp.dot(mp, bd, preferred_element_type=F32)
            y = y + jnp.dot(cg, sg[:, half * MXU_DIM:(half + 1) * MXU_DIM],
                            preferred_element_type=F32) * ea[:, cols]
            y_blocks.append(y + dskip_ref[:, cols] * xs[:, cols])

    el = jnp.exp(last)
    for g in range(N_GROUPS):
        cols = slice(g * GROUP_WIDTH, (g + 1) * GROUP_WIDTH)
        bgt = bm[:, g * D_STATE:(g + 1) * D_STATE].T.astype(BF16)
        state[g] = state[g] * el[:, cols] + jnp.dot(bgt, wj[:, cols], preferred_element_type=F32)

    zf = z.astype(F32)
    yg = jnp.concatenate(y_blocks, axis=-1) * (zf * jax.nn.sigmoid(zf))
    return (yg * _rms_scale(yg) * gssm_ref[...]).astype(BF16)


def _ssd_scratch():
    return [pltpu.VMEM((N_GROUPS, D_STATE, GROUP_WIDTH), F32),
            pltpu.VMEM((SSM_CONV - 1, SUBLANES, CONV_DIM), F32)]


def _ssd_kernel(*refs, has_state, valid_rows, n_chunks):
    if has_state:
        xbc_ref, dt_ref, z_ref, ctx_ref, h0_ref = refs[:5]
        refs = refs[5:]
    else:
        xbc_ref, dt_ref, z_ref, ctx_ref = refs[:4]
        h0_ref = None
        refs = refs[4:]
    prm, (yb_out, st_out, ctx_out, state, heads) = refs[:7], refs[7:]
    c = pl.program_id(1)

    @pl.when(c == 0)
    def _load_carry():
        _ssd_load_carry(ctx_ref, h0_ref, state, heads)

    xc = _ssd_conv(xbc_ref[0], prm, heads, valid_rows)
    acol, dtcol = _ssd_steps(dt_ref[0], prm, valid_rows)
    yb_out[0] = _ssd_chunk(xc, acol, dtcol, z_ref[0], prm, state)

    @pl.when(c == n_chunks - 1)
    def _emit_carry():
        _ssd_emit_carry(st_out, ctx_out, state, heads)


def _ssd(xbc, dt_raw, z, ctx, h0, prm, valid_rows):
    b, seq, _ = xbc.shape
    n_chunks = seq // CHUNK
    has_state = h0 is not None
    chunk = lambda w: pl.BlockSpec((1, CHUNK, w), lambda bi, ci: (bi, ci, 0))
    per_batch = lambda r, w: pl.BlockSpec((1, r, w), lambda bi, ci: (bi, 0, 0))
    in_specs = [chunk(CONV_DIM), chunk(DT_PAD), chunk(D_INNER), per_batch((SSM_CONV - 1) * SUBLANES, CONV_DIM)]
    args = [xbc, dt_raw, z, _conv_heads(ctx, SSM_CONV)]
    if has_state:
        in_specs.append(per_batch(D_INNER, D_STATE))
        args.append(h0)
    in_specs += [_resident(p.shape) for p in prm]
    args += list(prm)
    return pl.pallas_call(
        functools.partial(_ssd_kernel, has_state=has_state, valid_rows=valid_rows, n_chunks=n_chunks),
        grid=(b, n_chunks),
        in_specs=in_specs,
        out_specs=[chunk(D_INNER), per_batch(D_INNER, D_STATE), per_batch(SUBLANES, CONV_DIM)],
        out_shape=[jax.ShapeDtypeStruct((b, seq, D_INNER), BF16),
                   jax.ShapeDtypeStruct((b, D_INNER, D_STATE), F32),
                   jax.ShapeDtypeStruct((b, SUBLANES, CONV_DIM), F32)],
        scratch_shapes=_ssd_scratch(),
        compiler_params=_params(("parallel", "arbitrary")),
        name="ssd",
    )(*args)


def _inproj_kernel(*refs, tiles_per_batch, tail_tiles, fuse_ssd, tm):
    x_ref, g_ref, w_ref, wgates_ref, wdt_ref, gq_ref, gk_ref, hsum_ref = refs[:8]
    if fuse_ssd:
        ctx_ref, prm = refs[8], refs[9:16]
        q_out, k_out, v_out, gates_out, ktail_out, vtail_out, yb_out, st_out, ctx_out, state, heads = refs[16:]
    else:
        q_out, k_out, v_out, gates_out, ktail_out, vtail_out, z_out, xbc_out, dt_out = refs[8:]
    tile_in_batch = pl.program_id(0) % tiles_per_batch

    if fuse_ssd:
        @pl.when(tile_in_batch == 0)
        def _load_carry():
            _ssd_load_carry(ctx_ref, None, state, heads)

    x = x_ref[...]
    h = (x * _rms_scale(x) * g_ref[...]).astype(BF16)

    def seg(bounds):
        return jnp.dot(h, w_ref[:, bounds[0]:bounds[1]], preferred_element_type=F32)

    def head_norm(t, g):
        sq = (t * t).astype(BF16)
        parts = []
        for blk in range(ATT_WIDTH // MXU_DIM):
            cols = slice(blk * MXU_DIM, (blk + 1) * MXU_DIM)
            ss = jnp.dot(sq[:, cols], hsum_ref[...], preferred_element_type=F32)
            parts.append(t[:, cols] * lax.rsqrt(ss * (1.0 / HEAD_DIM_A) + EPS))
        return jnp.concatenate(parts, axis=-1) * g

    xbc = seg(_SEG_XBC)
    zb = seg(_SEG_Z).astype(BF16)
    dt_raw = jnp.dot(h, wdt_ref[...], preferred_element_type=F32)
    if fuse_ssd:
        xc = _ssd_conv(xbc, prm, heads, tm)
        acol, dtcol = _ssd_steps(dt_raw, prm, tm)
        for c in range(tm // CHUNK):
            rows = slice(c * CHUNK, (c + 1) * CHUNK)
            yb_out[rows, :] = _ssd_chunk(xc[rows, :], acol[rows, :], dtcol[rows, :], zb[rows, :], prm, state)
    else:
        xbc_out[...] = xbc
        z_out[...] = zb
        dt_out[...] = dt_raw

    q_out[...] = (head_norm(seg(_SEG_Q), gq_ref[...]) * (HEAD_DIM_A ** -0.5)).astype(BF16)
    k = head_norm(seg(_SEG_K), gk_ref[...])
    v = seg(_SEG_V)
    k_out[...] = k.astype(BF16)
    v_out[...] = v.astype(BF16)
    gates_out[...] = jnp.dot(h, wgates_ref[...], preferred_element_type=F32).astype(BF16)

    @pl.when(tile_in_batch >= tiles_per_batch - tail_tiles)
    def _emit_tail():
        ktail_out[...] = k
        vtail_out[...] = v

    if fuse_ssd:
        @pl.when(tile_in_batch == tiles_per_batch - 1)
        def _emit_carry():
            _ssd_emit_carry(st_out, ctx_out, state, heads)


def _inproj(x2d, w, tm, seq, ssm_ctx=None):
    t = x2d.shape[0]
    fuse_ssd = ssm_ctx is not None
    tiles_per_batch = max(seq // tm, 1)
    tail_tiles = max(min(ATT_PAST, seq) // tm, 1)
    n_tiles = t // tm
    n_batches = n_tiles // tiles_per_batch
    row = lambda width: pl.BlockSpec((tm, width), lambda i: (i, 0))
    first_tail = tiles_per_batch - tail_tiles
    tail = pl.BlockSpec((tm, ATT_WIDTH), lambda i: (
        (i // tiles_per_batch) * tail_tiles + jnp.maximum(i % tiles_per_batch - first_tail, 0), 0))
    per_batch = lambda r, width: pl.BlockSpec((1, r, width), lambda i: (i // tiles_per_batch, 0, 0))
    rows_out = lambda width, dtype: (row(width), jax.ShapeDtypeStruct((t, width), dtype))
    tail_out = (tail, jax.ShapeDtypeStruct((n_batches * tail_tiles * tm, ATT_WIDTH), F32))

    in_specs = [row(D_MODEL), _resident((1, D_MODEL)), _resident((D_MODEL, _SEG_XBC[1])),
                _resident(w["w_gates"].shape), _resident(w["w_dt"].shape),
                _resident((1, ATT_WIDTH)), _resident((1, ATT_WIDTH)), _resident((MXU_DIM, MXU_DIM))]
    args = [x2d, w["g_mix"], w["w_in"], w["w_gates"], w["w_dt"], w["g_q"], w["g_k"], w["hsum"]]
    outs = [rows_out(ATT_WIDTH, BF16), rows_out(ATT_WIDTH, BF16), rows_out(ATT_WIDTH, BF16),
            rows_out(2 * D_MODEL, BF16), tail_out, tail_out]
    scratch = []
    if fuse_ssd:
        assert seq % tm == 0 and tm % CHUNK == 0
        prm = _ssd_params(w)
        in_specs += [per_batch((SSM_CONV - 1) * SUBLANES, CONV_DIM)] + [_resident(p.shape) for p in prm]
        args += [_conv_heads(ssm_ctx, SSM_CONV)] + list(prm)
        outs += [rows_out(D_INNER, BF16),
                 (per_batch(D_INNER, D_STATE), jax.ShapeDtypeStruct((n_batches, D_INNER, D_STATE), F32)),
                 (per_batch(SUBLANES, CONV_DIM), jax.ShapeDtypeStruct((n_batches, SUBLANES, CONV_DIM), F32))]
        scratch = _ssd_scratch()
    else:
        outs += [rows_out(D_INNER, BF16), rows_out(CONV_DIM, F32), rows_out(DT_PAD, F32)]
    return pl.pallas_call(
        functools.partial(_inproj_kernel, tiles_per_batch=tiles_per_batch, tail_tiles=tail_tiles,
                          fuse_ssd=fuse_ssd, tm=tm),
        grid=(n_tiles,),
        in_specs=in_specs,
        out_specs=[o[0] for o in outs],
        out_shape=[o[1] for o in outs],
        scratch_shapes=scratch,
        compiler_params=_params(("arbitrary",)),
        name="inproj_ssd" if fuse_ssd else "inproj",
    )(*args)


HEADS_PER_ATT_BLOCK = MXU_DIM // HEAD_DIM_A
HEADS_PER_LANE_TILE = LANES // HEAD_DIM_A


def _attn_kernel(*refs, has_cache, seq, qb):
    if has_cache:
        q_ref, k_ref, v_ref, bias_ref, ck_ref, cv_ref, o_ref, kp, vp = refs
    else:
        q_ref, k_ref, v_ref, bias_ref, o_ref, kp, vp = refs
    nk = ATT_PAST + qb

    if has_cache:
        kp[0:ATT_PAST, :] = ck_ref[0].astype(BF16)
        vp[0:ATT_PAST, :] = cv_ref[0].astype(BF16)
    kp[ATT_PAST:ATT_PAST + seq, :] = k_ref[0]
    vp[ATT_PAST:ATT_PAST + seq, :] = v_ref[0]

    lane = lax.broadcasted_iota(jnp.int32, (qb, LANES), 1)
    low_half = lane < HEAD_DIM_A

    def q_block(start, first_col):
        kb = kp[pl.ds(start + first_col, nk - first_col), :]
        vb = vp[pl.ds(start + first_col, nk - first_col), :]
        q = q_ref[0, pl.ds(start, qb), :]
        outs = []
        for pair in range(HEADS_PER_ATT_BLOCK // HEADS_PER_LANE_TILE):
            cols = slice(pair * LANES, (pair + 1) * LANES)
            qp, kpair, vpair = q[:, cols], kb[:, cols], vb[:, cols]
            halves = []
            for sub in range(HEADS_PER_LANE_TILE):
                keep = low_half if sub == 0 else jnp.logical_not(low_half)
                qh = jnp.where(keep, qp, jnp.zeros((), BF16))
                s = lax.dot_general(qh, kpair, (((1,), (1,)), ((), ())), preferred_element_type=F32)
                s = s + bias_ref[pair * HEADS_PER_LANE_TILE + sub, :, first_col:]
                m = jnp.max(s, axis=-1, keepdims=True)
                p = jnp.exp(s - m)
                denom = jnp.sum(p, axis=-1, keepdims=True)
                halves.append(jnp.dot(p.astype(BF16), vpair, preferred_element_type=F32) / denom)
            outs.append(jnp.where(low_half, halves[0], halves[1]))
        o_ref[0, pl.ds(start, qb), :] = jnp.concatenate(outs, axis=-1).astype(BF16)

    n_blocks = seq // qb
    n_peeled = 0 if has_cache else min(ATT_PAST // qb, n_blocks)
    for i in range(n_peeled):
        q_block(i * qb, ATT_PAST - i * qb)

    n_steady = n_blocks - n_peeled
    unroll = 2 if n_steady % 2 == 0 and n_steady > 0 else 1

    def steady(i, carry):
        for u in range(unroll):
            q_block(pl.multiple_of((n_peeled + i * unroll + u) * qb, qb), 0)
        return carry

    lax.fori_loop(0, n_steady // unroll, steady, 0)


def _attention(q, k, v, bias, cache, qb):
    b, seq, _ = q.shape
    nk = ATT_PAST + qb
    has_cache = cache is not None
    hb = ATT_WIDTH // MXU_DIM
    seq_spec = pl.BlockSpec((1, seq, MXU_DIM), lambda bi, hi: (bi, 0, hi))
    in_specs = [seq_spec, seq_spec, seq_spec,
                pl.BlockSpec((HEADS_PER_ATT_BLOCK, qb, nk), lambda bi, hi: (hi, 0, 0))]
    args = [q, k, v, bias]
    if has_cache:
        cspec = pl.BlockSpec((1, ATT_PAST, MXU_DIM), lambda bi, hi: (bi, 0, hi))
        in_specs += [cspec, cspec]
        args += list(cache)
    return pl.pallas_call(
        functools.partial(_attn_kernel, has_cache=has_cache, seq=seq, qb=qb),
        grid=(b, hb),
        in_specs=in_specs,
        out_specs=seq_spec,
        out_shape=jax.ShapeDtypeStruct((b, seq, ATT_WIDTH), BF16),
        scratch_shapes=[pltpu.VMEM((ATT_PAST + seq, MXU_DIM), BF16),
                        pltpu.VMEM((ATT_PAST + seq, MXU_DIM), BF16)],
        compiler_params=_params(("parallel", "parallel")),
        name="band_attention",
    )(*args)


def _attention_bias(table, qb):
    nk = ATT_PAST + qb
    n_diag = qb + nk - 1
    diag_rel = (ATT_PAST + qb - 1) - jnp.arange(n_diag)
    r = table[:, jnp.clip(diag_rel, -REL_CLIP, REL_CLIP) + REL_CLIP]
    h = table.shape[0]
    rp = jnp.pad(r, ((0, 0), (0, 1)))
    skew = jnp.tile(rp, (1, qb))[:, :qb * n_diag].reshape(h, qb, n_diag)
    bias = skew[:, :, qb - 1:qb - 1 + nk]
    qc = jnp.arange(qb)[:, None] // CHUNK
    kc = jnp.arange(nk)[None, :] // CHUNK
    visible = (kc >= qc) & (kc <= qc + ATT_PAST // CHUNK)
    return jnp.where(visible[None], bias, NEG_INF).astype(F32)


def _merge_kernel(x_ref, ya_ref, yb_ref, gates_ref, bg_ref, wa_ref, wb_ref, wo_ref, o_ref):
    gate = jax.nn.sigmoid(gates_ref[...].astype(F32) + bg_ref[...])
    pa = jnp.dot(ya_ref[...], wa_ref[...], preferred_element_type=F32)
    pb = jnp.dot(yb_ref[...], wb_ref[...], preferred_element_type=F32)
    t = gate[:, :D_MODEL] * pa + gate[:, D_MODEL:] * pb
    o_ref[...] = x_ref[...] + jnp.dot(t.astype(BF16), wo_ref[...], preferred_element_type=F32)


def _merge(x2d, ya, yb, gates, b_gate, wa, wb, wo, tm):
    t = x2d.shape[0]
    row = lambda w: pl.BlockSpec((tm, w), lambda i: (i, 0))
    return pl.pallas_call(
        _merge_kernel,
        grid=(t // tm,),
        in_specs=[row(D_MODEL), row(ATT_WIDTH), row(D_INNER), row(2 * D_MODEL),
                  _resident(b_gate.shape), _resident(wa.shape), _resident(wb.shape), _resident(wo.shape)],
        out_specs=row(D_MODEL),
        out_shape=jax.ShapeDtypeStruct((t, D_MODEL), F32),
        compiler_params=_params(("parallel",)),
        name="merge_outproj",
    )(x2d, ya, yb, gates, b_gate, wa, wb, wo)


def _ffn_kernel(x_ref, pe_ref, ctx_ref, gffn_ref, wup_ref, cw_ref, cb_ref, wdown_ref,
                gple_ref, wgate_ref, wple_ref, o_ref, ctx_out, rs, *, tm, tiles_per_batch):
    i = pl.program_id(0)

    @pl.when(i % tiles_per_batch == 0)
    def _load_context():
        for k in range(FFN_CONV - 1):
            rs[k, 0:SUBLANES, :] = ctx_ref[0, k * SUBLANES:(k + 1) * SUBLANES, :]

    x = x_ref[...]
    xn = (x * _rms_scale(x) * gffn_ref[...]).astype(BF16)
    n_slabs = tm // SUBLANES
    row_in_slab = lax.broadcasted_iota(jnp.int32, (tm, FFN_COL_BLOCK), 0) % SUBLANES

    def up(j, part):
        lo = part * D_FF + j * FFN_COL_BLOCK
        return jnp.dot(xn, wup_ref[:, lo:lo + FFN_COL_BLOCK], preferred_element_type=F32)

    def conv(u, j, part):
        lo = part * D_FF + j * FFN_COL_BLOCK
        cols = slice(lo, lo + FFN_COL_BLOCK)
        out = cb_ref[:, cols] + u * cw_ref[FFN_CONV - 1:FFN_CONV, cols]
        for k in range(1, FFN_CONV):
            rot = pltpu.roll(u.reshape(n_slabs, SUBLANES, FFN_COL_BLOCK), k, 1).reshape(tm, FFN_COL_BLOCK)
            rs[k - 1, SUBLANES:SUBLANES + tm, cols] = rot
            shifted = jnp.where(row_in_slab < k, rs[k - 1, 0:tm, cols], rot)
            out = out + shifted * cw_ref[FFN_CONV - 1 - k:FFN_CONV - k, cols]
        return out

    n_blocks = D_FF // FFN_COL_BLOCK
    acc = jnp.zeros((tm, D_MODEL), F32)
    pending = (up(0, 0), up(0, 1))
    for j in range(n_blocks):
        ug, uv = pending
        if j + 1 < n_blocks:
            pending = (up(j + 1, 0), up(j + 1, 1))
        hidden = (jax.nn.gelu(conv(ug, j, 0), approximate=True) * conv(uv, j, 1)).astype(BF16)
        acc = acc + jnp.dot(hidden, wdown_ref[j * FFN_COL_BLOCK:(j + 1) * FFN_COL_BLOCK, :],
                            preferred_element_type=F32)

    @pl.when(i % tiles_per_batch == tiles_per_batch - 1)
    def _emit_context():
        ctx_out[0] = rs[FFN_CONV - 2, tm:tm + SUBLANES, :]

    for k in range(FFN_CONV - 1):
        rs[k, 0:SUBLANES, :] = rs[k, tm:tm + SUBLANES, :]

    x2 = x + acc
    gate = jax.nn.sigmoid(jnp.dot((x2 * _rms_scale(x2) * gple_ref[...]).astype(BF16), wgate_ref[...],
                                  preferred_element_type=F32))
    o_ref[...] = x2 + gate * jnp.dot(pe_ref[...].astype(BF16), wple_ref[...], preferred_element_type=F32)


def _ffn(x2d, pe2d, ctx, prm, batch, tm):
    t = x2d.shape[0]
    tiles_per_batch = t // batch // tm
    row = lambda w: pl.BlockSpec((tm, w), lambda i: (i, 0))
    per_batch = lambda r: pl.BlockSpec((1, r, 2 * D_FF), lambda i: (i // tiles_per_batch, 0, 0))
    y, ctx_out = pl.pallas_call(
        functools.partial(_ffn_kernel, tm=tm, tiles_per_batch=tiles_per_batch),
        grid=(t // tm,),
        in_specs=[row(D_MODEL), row(PLE_DIM), per_batch((FFN_CONV - 1) * SUBLANES)]
        + [_resident(p.shape) for p in prm],
        out_specs=[row(D_MODEL), per_batch(SUBLANES)],
        out_shape=[jax.ShapeDtypeStruct((t, D_MODEL), F32),
                   jax.ShapeDtypeStruct((batch, SUBLANES, 2 * D_FF), F32)],
        scratch_shapes=[pltpu.VMEM((FFN_CONV - 1, SUBLANES + tm, 2 * D_FF), F32)],
        compiler_params=_params(("arbitrary",)),
        name="convffn_ple",
    )(x2d, pe2d, _conv_heads(ctx, FFN_CONV), *prm)
    return y, ctx_out[:, :FFN_CONV - 1]


def _ssd_params(w):
    return (w["conv_ssm_w"], w["conv_ssm_b"], w["dt_bias"], w["a_log"], w["d_skip"], w["g_ssm"], w["expand"])


def _layer(x, pe, kv_cache, ssm_ctx, h0, ffn_ctx, w, tm, tq):
    b, seq, _ = x.shape
    t = b * seq
    x2d = x.reshape(t, D_MODEL)
    as3 = lambda a: a.reshape(b, seq, -1)
    fuse_ssd = h0 is None and seq % tm == 0 and tm % CHUNK == 0
    if fuse_ssd:
        qn, kn, v, gates, k_tail, v_tail, yb, st, ctx_ssm = _inproj(x2d, w, tm, seq, ssm_ctx)
    else:
        qn, kn, v, gates, k_tail, v_tail, z, xbc, dt_raw = _inproj(x2d, w, tm, seq)
        seq_pad = -(-seq // CHUNK) * CHUNK
        valid_rows = CHUNK - (seq_pad - seq)
        pad = lambda a: jnp.pad(as3(a), ((0, 0), (0, seq_pad - seq), (0, 0)))
        yb, st, ctx_ssm = _ssd(pad(xbc), pad(dt_raw), pad(z), ssm_ctx, h0, _ssd_params(w), valid_rows)
        yb = yb[:, :seq].reshape(t, D_INNER)

    cache = None
    if kv_cache is not None:
        cache = tuple(ck.reshape(b, ATT_PAST, ATT_WIDTH) for ck in kv_cache)
    ya = _attention(as3(qn), as3(kn), as3(v), _attention_bias(w["rel_bias"], tq), cache, tq)

    x1 = _merge(x2d, ya.reshape(t, ATT_WIDTH), yb, gates, w["b_gate"], w["w_proj_a"], w["w_proj_b"], w["w_out"], tm)
    ffn_prm = (w["g_ffn"], w["w_up"], w["ffn_conv_w"], w["ffn_conv_b"], w["w_down"],
               w["g_ple"], w["w_ple_gate"], w["w_ple"])
    y, ctx_ffn = _ffn(x1, pe.reshape(t, PLE_DIM), ffn_ctx, ffn_prm, b, min(tm, seq))

    heads = lambda a: a.reshape(b, min(seq, ATT_PAST), N_HEADS_A, HEAD_DIM_A)
    state = st.reshape(b, N_HEADS_S, SSM_HEAD_DIM, D_STATE)
    return y.reshape(b, seq, D_MODEL), (heads(k_tail), heads(v_tail), state, ctx_ssm[:, :SSM_CONV - 1], ctx_ffn)


def _prep_weights(i, g_mix, w_in, b_gate, g_q, g_k, rel_bias, conv_ssm_w, conv_ssm_b, dt_bias, a_log, d_skip,
                  g_ssm, w_proj_a, w_proj_b, w_out, g_ffn, w_up, ffn_conv_w, ffn_conv_b, w_down, g_ple,
                  w_ple_gate, w_ple):
    row = lambda a: a[i].reshape(1, -1).astype(F32)
    pad_lanes = lambda a, n: jnp.pad(a, ((0, 0), (0, n - a.shape[1])))
    wi = w_in[i].astype(BF16)
    head_of_lane = jnp.arange(D_INNER) // SSM_HEAD_DIM
    expand1 = (jnp.arange(DT_PAD)[:, None] == head_of_lane[None, :]).astype(BF16)
    blk = jnp.arange(MXU_DIM) // HEAD_DIM_A
    return {
        "g_mix": row(g_mix), "w_in": wi, "w_gates": wi[:, _SEG_GATES[0]:_SEG_GATES[1]],
        "w_dt": pad_lanes(wi[:, _SEG_DT[0]:_SEG_DT[1]], DT_PAD), "b_gate": row(b_gate),
        "g_q": jnp.tile(row(g_q), (1, N_HEADS_A)), "g_k": jnp.tile(row(g_k), (1, N_HEADS_A)),
        "hsum": (blk[:, None] == blk[None, :]).astype(BF16),
        "rel_bias": rel_bias[i].astype(F32),
        "conv_ssm_w": jnp.pad(conv_ssm_w[i], ((0, SUBLANES - SSM_CONV), (0, 0))), "conv_ssm_b": row(conv_ssm_b),
        "dt_bias": pad_lanes(row(dt_bias), DT_PAD), "a_log": pad_lanes(row(a_log), DT_PAD),
        "d_skip": jnp.repeat(row(d_skip), SSM_HEAD_DIM, axis=1), "g_ssm": row(g_ssm),
        "expand": jnp.concatenate([expand1] * 3, axis=0),
        "w_proj_a": w_proj_a[i].astype(BF16), "w_proj_b": w_proj_b[i].astype(BF16), "w_out": w_out[i].astype(BF16),
        "g_ffn": row(g_ffn), "w_up": w_up[i].astype(BF16),
        "ffn_conv_w": jnp.pad(ffn_conv_w[i], ((0, SUBLANES - FFN_CONV), (0, 0))), "ffn_conv_b": row(ffn_conv_b),
        "w_down": w_down[i].astype(BF16), "g_ple": row(g_ple),
        "w_ple_gate": w_ple_gate[i].astype(BF16), "w_ple": w_ple[i].astype(BF16),
    }


def kernel(x_prompt, x_sample, cache_k, cache_v, state_ssm, state_conv_ssm, state_conv_ffn, p_prompt, p_sample, g_mix, w_in, b_gate, g_q, g_k, rel_bias, conv_ssm_w, conv_ssm_b, dt_bias, a_log, d_skip, g_ssm, w_proj_a, w_proj_b, w_out, g_ffn, w_up, ffn_conv_w, ffn_conv_b, w_down, g_ple, w_ple_gate, w_ple):
    depth = w_in.shape[0]
    bp = x_prompt.shape[0]
    bs, seq_s = x_sample.shape[:2]
    yp, ys = x_prompt, x_sample
    sp, ss = [], []
    for i in range(depth):
        w = _prep_weights(i, g_mix, w_in, b_gate, g_q, g_k, rel_bias, conv_ssm_w, conv_ssm_b, dt_bias, a_log,
                          d_skip, g_ssm, w_proj_a, w_proj_b, w_out, g_ffn, w_up, ffn_conv_w, ffn_conv_b, w_down,
                          g_ple, w_ple_gate, w_ple)
        yp, st_p = _layer(yp, p_prompt[i], None,
                          jnp.zeros((bp, SSM_CONV - 1, CONV_DIM), F32), None,
                          jnp.zeros((bp, FFN_CONV - 1, 2 * D_FF), F32), w, tm=256, tq=4 * CHUNK)
        h0 = state_ssm[i].reshape(bs, D_INNER, D_STATE)
        ys, st_s = _layer(ys, p_sample[i], (cache_k[i], cache_v[i]), state_conv_ssm[i], h0,
                          state_conv_ffn[i], w, tm=seq_s * bs, tq=seq_s)
        sp.append(st_p)
        ss.append(st_s)

    stack = lambda lst, j: jnp.stack([s[j] for s in lst], axis=0)
    return (yp, ys,
            stack(sp, 0), stack(sp, 1), stack(sp, 2), stack(sp, 3), stack(sp, 4),
            stack(ss, 0), stack(ss, 1), stack(ss, 2), stack(ss, 3), stack(ss, 4))
```

```python
import functools

import jax
import jax.numpy as jnp
from jax import lax
from jax.experimental import pallas as pl
from jax.experimental.pallas import tpu as pltpu

F32 = jnp.float32
BF16 = jnp.bfloat16

D_MODEL = 1024
CHUNK = 64
PLE_DIM = 256
EPS = 1e-6
NEG_INF = -1e30
N_HEADS_A = 16
HEAD_DIM_A = 64
ATT_WIDTH = N_HEADS_A * HEAD_DIM_A
ATT_PAST = 8 * CHUNK
REL_CLIP = 128
D_INNER = 2 * D_MODEL
SSM_HEAD_DIM = 64
N_HEADS_S = D_INNER // SSM_HEAD_DIM
N_GROUPS = 4
GROUP_WIDTH = D_INNER // N_GROUPS
D_STATE = 128
SSM_CONV = 4
CONV_DIM = D_INNER + 2 * N_GROUPS * D_STATE
D_FF = 3 * D_MODEL
FFN_CONV = 3

LANES = 128
SUBLANES = 8
MXU_DIM = 256
VMEM_LIMIT_BYTES = 56 * 1024 * 1024

DT_PAD = LANES
HEADS_PER_MXU_TILE = MXU_DIM // SSM_HEAD_DIM
FFN_COL_BLOCK = 1024

_SEG_Q = (0, ATT_WIDTH)
_SEG_K = (ATT_WIDTH, 2 * ATT_WIDTH)
_SEG_V = (2 * ATT_WIDTH, 3 * ATT_WIDTH)
_SEG_Z = (3 * ATT_WIDTH, 3 * ATT_WIDTH + D_INNER)
_SEG_XBC = (_SEG_Z[1], _SEG_Z[1] + CONV_DIM)
_SEG_DT = (_SEG_XBC[1], _SEG_XBC[1] + N_HEADS_S)
_SEG_GATES = (_SEG_DT[1], _SEG_DT[1] + 2 * D_MODEL)


def _resident(shape):
    nd = len(shape)
    return pl.BlockSpec(shape, lambda *_: (0,) * nd, pipeline_mode=pl.Buffered(1))


def _params(semantics):
    return pltpu.CompilerParams(dimension_semantics=semantics, vmem_limit_bytes=VMEM_LIMIT_BYTES)


def _rms_scale(x):
    return lax.rsqrt(jnp.mean(x * x, axis=-1, keepdims=True) + EPS)


def _conv_heads(ctx, width):
    heads = [jnp.pad(ctx[:, width - 1 - k:], ((0, 0), (0, SUBLANES - k), (0, 0))) for k in range(1, width)]
    return jnp.concatenate(heads, axis=1)


def _softplus(x):
    return jnp.maximum(x, 0.0) + jnp.log(1.0 + jnp.exp(-jnp.abs(x)))


def _split3(v):
    hi = v.astype(BF16)
    r1 = v - hi.astype(F32)
    mid = r1.astype(BF16)
    lo = (r1 - mid.astype(F32)).astype(BF16)
    return jnp.concatenate([hi, mid, lo], axis=-1)


def _ssd_load_carry(ctx_ref, h0_ref, state, heads):
    for k in range(SSM_CONV - 1):
        heads[k] = ctx_ref[0, k * SUBLANES:(k + 1) * SUBLANES, :]
    for g in range(N_GROUPS):
        if h0_ref is None:
            state[g] = jnp.zeros((D_STATE, GROUP_WIDTH), F32)
        else:
            state[g] = h0_ref[0, g * GROUP_WIDTH:(g + 1) * GROUP_WIDTH, :].T


def _ssd_emit_carry(st_out, ctx_out, state, heads):
    ctx_out[0] = heads[SSM_CONV - 2]
    for g in range(N_GROUPS):
        st_out[0, g * GROUP_WIDTH:(g + 1) * GROUP_WIDTH, :] = state[g].T


def _ssd_conv(x_raw, prm, heads, valid_rows):
    cw_ref, cb_ref = prm[:2]
    n_rows = x_raw.shape[0]
    row_in_slab = lax.broadcasted_iota(jnp.int32, (n_rows, CONV_DIM), 0) % SUBLANES
    acc = cb_ref[...] + x_raw * cw_ref[SSM_CONV - 1:SSM_CONV, :]
    for k in range(1, SSM_CONV):
        rot = pltpu.roll(x_raw.reshape(n_rows // SUBLANES, SUBLANES, CONV_DIM), k, 1).reshape(n_rows, CONV_DIM)
        above = jnp.concatenate([heads[k - 1], rot[:n_rows - SUBLANES]], axis=0)
        acc = acc + jnp.where(row_in_slab < k, above, rot) * cw_ref[SSM_CONV - 1 - k:SSM_CONV - k, :]
        heads[k - 1] = rot[valid_rows - SUBLANES:valid_rows]
    return acc * jax.nn.sigmoid(acc)


def _ssd_steps(dt_raw, prm, valid_rows):
    dtb_ref, alog_ref, expand_ref = prm[2], prm[3], prm[6]
    n_rows = dt_raw.shape[0]
    dt = _softplus(dt_raw + dtb_ref[...])
    if valid_rows < n_rows:
        rows = lax.broadcasted_iota(jnp.int32, (n_rows, DT_PAD), 0)
        dt = jnp.where(rows < valid_rows, dt, 0.0)
    ac = dt * (-jnp.exp(alog_ref[...]))
    ri = lax.broadcasted_iota(jnp.int32, (n_rows, n_rows), 0)
    ci = lax.broadcasted_iota(jnp.int32, (n_rows, n_rows), 1)
    same_chunk_causal = jnp.logical_and(ri // CHUNK == ci // CHUNK, ri >= ci)
    acs = jnp.dot(same_chunk_causal.astype(F32), ac, preferred_element_type=F32,
                  precision=lax.Precision.HIGHEST)
    expand = lambda v: jnp.dot(_split3(v), expand_ref[...], preferred_element_type=F32)
    return expand(acs), expand(dt)


def _ssd_chunk(xc, acol, dtcol, z, prm, state):
    dskip_ref, gssm_ref = prm[4], prm[5]
    q = CHUNK

    xs = xc[:, :D_INNER]
    bm = xc[:, D_INNER:D_INNER + N_GROUPS * D_STATE]
    cm = xc[:, D_INNER + N_GROUPS * D_STATE:].astype(BF16)

    last = acol[q - 1:q, :]
    dx = dtcol * xs
    wj = (jnp.exp(last - acol) * dx).astype(BF16)
    ea = jnp.exp(acol)

    lane = lax.broadcasted_iota(jnp.int32, (q, D_INNER), 1)
    rowi = lax.broadcasted_iota(jnp.int32, (q, D_INNER), 0)
    arow = jnp.sum(jnp.where(lane % q == rowi, acol, 0.0), axis=0, keepdims=True)

    tri = (lax.broadcasted_iota(jnp.int32, (q, MXU_DIM), 0)
           >= lax.broadcasted_iota(jnp.int32, (q, MXU_DIM), 1) % q)
    br = lax.broadcasted_iota(jnp.int32, (MXU_DIM, MXU_DIM), 0) // SSM_HEAD_DIM
    bc = lax.broadcasted_iota(jnp.int32, (MXU_DIM, MXU_DIM), 1) // SSM_HEAD_DIM
    blockdiag = br == bc

    y_blocks = []
    for g in range(N_GROUPS):
        cg = cm[:, g * D_STATE:(g + 1) * D_STATE]
        bg = bm[:, g * D_STATE:(g + 1) * D_STATE].astype(BF16)
        bg4 = jnp.concatenate([bg] * HEADS_PER_MXU_TILE, axis=0)
        cb4 = lax.dot_general(cg, bg4, (((1,), (1,)), ((), ())), preferred_element_type=F32)
        sg = state[g].astype(BF16)
        for half in range(GROUP_WIDTH // MXU_DIM):
            lo = g * GROUP_WIDTH + half * MXU_DIM
            cols = slice(lo, lo + MXU_DIM)
            seg = acol[:, cols] - arow[:, cols]
            decay = jnp.where(tri, jnp.exp(jnp.where(tri, seg, 0.0)), 0.0)
            mp = (cb4 * decay).astype(BF16)
            dxb = dx[:, cols].astype(BF16)
            bd = jnp.where(blockdiag, jnp.concatenate([dxb] * HEADS_PER_MXU_TILE, axis=0),
                           jnp.zeros((), BF16))
            y = jnp.dot(mp, bd, preferred_element_type=F32)
            y = y + jnp.dot(cg, sg[:, half * MXU_DIM:(half + 1) * MXU_DIM],
                            preferred_element_type=F32) * ea[:, cols]
            y_blocks.append(y + dskip_ref[:, cols] * xs[:, cols])

    el = jnp.exp(last)
    for g in range(N_GROUPS):
        cols = slice(g * GROUP_WIDTH, (g + 1) * GROUP_WIDTH)
        bgt = bm[:, g * D_STATE:(g + 1) * D_STATE].T.astype(BF16)
        state[g] = state[g] * el[:, cols] + jnp.dot(bgt, wj[:, cols], preferred_element_type=F32)

    zf = z.astype(F32)
    yg = jnp.concatenate(y_blocks, axis=-1) * (zf * jax.nn.sigmoid(zf))
    return (yg * _rms_scale(yg) * gssm_ref[...]).astype(BF16)


def _ssd_scratch():
    return [pltpu.VMEM((N_GROUPS, D_STATE, GROUP_WIDTH), F32),
            pltpu.VMEM((SSM_CONV - 1, SUBLANES, CONV_DIM), F32)]


def _ssd_kernel(*refs, has_state, valid_rows, n_chunks):
    if has_state:
        xbc_ref, dt_ref, z_ref, ctx_ref, h0_ref = refs[:5]
        refs = refs[5:]
    else:
        xbc_ref, dt_ref, z_ref, ctx_ref = refs[:4]
        h0_ref = None
        refs = refs[4:]
    prm, (yb_out, st_out, ctx_out, state, heads) = refs[:7], refs[7:]
    c = pl.program_id(1)

    @pl.when(c == 0)
    def _load_carry():
        _ssd_load_carry(ctx_ref, h0_ref, state, heads)

    xc = _ssd_conv(xbc_ref[0], prm, heads, valid_rows)
    acol, dtcol = _ssd_steps(dt_ref[0], prm, valid_rows)
    yb_out[0] = _ssd_chunk(xc, acol, dtcol, z_ref[0], prm, state)

    @pl.when(c == n_chunks - 1)
    def _emit_carry():
        _ssd_emit_carry(st_out, ctx_out, state, heads)


def _ssd(xbc, dt_raw, z, ctx, h0, prm, valid_rows):
    b, seq, _ = xbc.shape
    n_chunks = seq // CHUNK
    has_state = h0 is not None
    chunk = lambda w: pl.BlockSpec((1, CHUNK, w), lambda bi, ci: (bi, ci, 0))
    per_batch = lambda r, w: pl.BlockSpec((1, r, w), lambda bi, ci: (bi, 0, 0))
    in_specs = [chunk(CONV_DIM), chunk(DT_PAD), chunk(D_INNER), per_batch((SSM_CONV - 1) * SUBLANES, CONV_DIM)]
    args = [xbc, dt_raw, z, _conv_heads(ctx, SSM_CONV)]
    if has_state:
        in_specs.append(per_batch(D_INNER, D_STATE))
        args.append(h0)
    in_specs += [_resident(p.shape) for p in prm]
    args += list(prm)
    return pl.pallas_call(
        functools.partial(_ssd_kernel, has_state=has_state, valid_rows=valid_rows, n_chunks=n_chunks),
        grid=(b, n_chunks),
        in_specs=in_specs,
        out_specs=[chunk(D_INNER), per_batch(D_INNER, D_STATE), per_batch(SUBLANES, CONV_DIM)],
        out_shape=[jax.ShapeDtypeStruct((b, seq, D_INNER), BF16),
                   jax.ShapeDtypeStruct((b, D_INNER, D_STATE), F32),
                   jax.ShapeDtypeStruct((b, SUBLANES, CONV_DIM), F32)],
        scratch_shapes=_ssd_scratch(),
        compiler_params=_params(("parallel", "arbitrary")),
        name="ssd",
    )(*args)


def _inproj_kernel(*refs, tiles_per_batch, tail_tiles, fuse_ssd, tm):
    x_ref, g_ref, w_ref, wgates_ref, wdt_ref, gq_ref, gk_ref, hsum_ref = refs[:8]
    if fuse_ssd:
        ctx_ref, prm = refs[8], refs[9:16]
        q_out, k_out, v_out, gates_out, ktail_out, vtail_out, yb_out, st_out, ctx_out, state, heads = refs[16:]
    else:
        q_out, k_out, v_out, gates_out, ktail_out, vtail_out, z_out, xbc_out, dt_out = refs[8:]
    tile_in_batch = pl.program_id(0) % tiles_per_batch

    if fuse_ssd:
        @pl.when(tile_in_batch == 0)
        def _load_carry():
            _ssd_load_carry(ctx_ref, None, state, heads)

    x = x_ref[...]
    h = (x * _rms_scale(x) * g_ref[...]).astype(BF16)

    def seg(bounds):
        return jnp.dot(h, w_ref[:, bounds[0]:bounds[1]], preferred_element_type=F32)

    def head_norm(t, g):
        sq = (t * t).astype(BF16)
        parts = []
        for blk in range(ATT_WIDTH // MXU_DIM):
            cols = slice(blk * MXU_DIM, (blk + 1) * MXU_DIM)
            ss = jnp.dot(sq[:, cols], hsum_ref[...], preferred_element_type=F32)
            parts.append(t[:, cols] * lax.rsqrt(ss * (1.0 / HEAD_DIM_A) + EPS))
        return jnp.concatenate(parts, axis=-1) * g

    xbc = seg(_SEG_XBC)
    zb = seg(_SEG_Z).astype(BF16)
    dt_raw = jnp.dot(h, wdt_ref[...], preferred_element_type=F32)
    if fuse_ssd:
        xc = _ssd_conv(xbc, prm, heads, tm)
        acol, dtcol = _ssd_steps(dt_raw, prm, tm)
        for c in range(tm // CHUNK):
            rows = slice(c * CHUNK, (c + 1) * CHUNK)
            yb_out[rows, :] = _ssd_chunk(xc[rows, :], acol[rows, :], dtcol[rows, :], zb[rows, :], prm, state)
    else:
        xbc_out[...] = xbc
        z_out[...] = zb
        dt_out[...] = dt_raw

    q_out[...] = (head_norm(seg(_SEG_Q), gq_ref[...]) * (HEAD_DIM_A ** -0.5)).astype(BF16)
    k = head_norm(seg(_SEG_K), gk_ref[...])
    v = seg(_SEG_V)
    k_out[...] = k.astype(BF16)
    v_out[...] = v.astype(BF16)
    gates_out[...] = jnp.dot(h, wgates_ref[...], preferred_element_type=F32).astype(BF16)

    @pl.when(tile_in_batch >= tiles_per_batch - tail_tiles)
    def _emit_tail():
        ktail_out[...] = k
        vtail_out[...] = v

    if fuse_ssd:
        @pl.when(tile_in_batch == tiles_per_batch - 1)
        def _emit_carry():
            _ssd_emit_carry(st_out, ctx_out, state, heads)


def _inproj(x2d, w, tm, seq, ssm_ctx=None):
    t = x2d.shape[0]
    fuse_ssd = ssm_ctx is not None
    tiles_per_batch = max(seq // tm, 1)
    tail_tiles = max(min(ATT_PAST, seq) // tm, 1)
    n_tiles = t // tm
    n_batches = n_tiles // tiles_per_batch
    row = lambda width: pl.BlockSpec((tm, width), lambda i: (i, 0))
    first_tail = tiles_per_batch - tail_tiles
    tail = pl.BlockSpec((tm, ATT_WIDTH), lambda i: (
        (i // tiles_per_batch) * tail_tiles + jnp.maximum(i % tiles_per_batch - first_tail, 0), 0))
    per_batch = lambda r, width: pl.BlockSpec((1, r, width), lambda i: (i // tiles_per_batch, 0, 0))
    rows_out = lambda width, dtype: (row(width), jax.ShapeDtypeStruct((t, width), dtype))
    tail_out = (tail, jax.ShapeDtypeStruct((n_batches * tail_tiles * tm, ATT_WIDTH), F32))

    in_specs = [row(D_MODEL), _resident((1, D_MODEL)), _resident((D_MODEL, _SEG_XBC[1])),
                _resident(w["w_gates"].shape), _resident(w["w_dt"].shape),
                _resident((1, ATT_WIDTH)), _resident((1, ATT_WIDTH)), _resident((MXU_DIM, MXU_DIM))]
    args = [x2d, w["g_mix"], w["w_in"], w["w_gates"], w["w_dt"], w["g_q"], w["g_k"], w["hsum"]]
    outs = [rows_out(ATT_WIDTH, BF16), rows_out(ATT_WIDTH, BF16), rows_out(ATT_WIDTH, BF16),
            rows_out(2 * D_MODEL, BF16), tail_out, tail_out]
    scratch = []
    if fuse_ssd:
        assert seq % tm == 0 and tm % CHUNK == 0
        prm = _ssd_params(w)
        in_specs += [per_batch((SSM_CONV - 1) * SUBLANES, CONV_DIM)] + [_resident(p.shape) for p in prm]
        args += [_conv_heads(ssm_ctx, SSM_CONV)] + list(prm)
        outs += [rows_out(D_INNER, BF16),
                 (per_batch(D_INNER, D_STATE), jax.ShapeDtypeStruct((n_batches, D_INNER, D_STATE), F32)),
                 (per_batch(SUBLANES, CONV_DIM), jax.ShapeDtypeStruct((n_batches, SUBLANES, CONV_DIM), F32))]
        scratch = _ssd_scratch()
    else:
        outs += [rows_out(D_INNER, BF16), rows_out(CONV_DIM, F32), rows_out(DT_PAD, F32)]
    return pl.pallas_call(
        functools.partial(_inproj_kernel, tiles_per_batch=tiles_per_batch, tail_tiles=tail_tiles,
                          fuse_ssd=fuse_ssd, tm=tm),
        grid=(n_tiles,),
        in_specs=in_specs,
        out_specs=[o[0] for o in outs],
        out_shape=[o[1] for o in outs],
        scratch_shapes=scratch,
        compiler_params=_params(("arbitrary",)),
        name="inproj_ssd" if fuse_ssd else "inproj",
    )(*args)


HEADS_PER_ATT_BLOCK = MXU_DIM // HEAD_DIM_A
HEADS_PER_LANE_TILE = LANES // HEAD_DIM_A


def _attn_kernel(*refs, has_cache, seq, qb):
    if has_cache:
        q_ref, k_ref, v_ref, bias_ref, ck_ref, cv_ref, o_ref, kp, vp = refs
    else:
        q_ref, k_ref, v_ref, bias_ref, o_ref, kp, vp = refs
    nk = ATT_PAST + qb

    if has_cache:
        kp[0:ATT_PAST, :] = ck_ref[0].astype(BF16)
        vp[0:ATT_PAST, :] = cv_ref[0].astype(BF16)
    kp[ATT_PAST:ATT_PAST + seq, :] = k_ref[0]
    vp[ATT_PAST:ATT_PAST + seq, :] = v_ref[0]

    lane = lax.broadcasted_iota(jnp.int32, (qb, LANES), 1)
    low_half = lane < HEAD_DIM_A

    def q_block(start, first_col):
        kb = kp[pl.ds(start + first_col, nk - first_col), :]
        vb = vp[pl.ds(start + first_col, nk - first_col), :]
        q = q_ref[0, pl.ds(start, qb), :]
        outs = []
        for pair in range(HEADS_PER_ATT_BLOCK // HEADS_PER_LANE_TILE):
            cols = slice(pair * LANES, (pair + 1) * LANES)
            qp, kpair, vpair = q[:, cols], kb[:, cols], vb[:, cols]
            zero = jnp.zeros((), BF16)
            q2 = jnp.concatenate([jnp.where(low_half, qp, zero), jnp.where(low_half, zero, qp)], axis=0)
            s = lax.dot_general(q2, kpair, (((1,), (1,)), ((), ())), preferred_element_type=F32)
            first_head = pair * HEADS_PER_LANE_TILE
            bias = bias_ref[first_head:first_head + HEADS_PER_LANE_TILE, :, first_col:]
            s = s + bias.reshape(HEADS_PER_LANE_TILE * qb, nk - first_col)
            m = jnp.max(s, axis=-1, keepdims=True)
            p = jnp.exp(s - m)
            denom = jnp.sum(p, axis=-1, keepdims=True)
            o2 = jnp.dot(p.astype(BF16), vpair, preferred_element_type=F32) / denom
            outs.append(jnp.where(low_half, o2[:qb], o2[qb:]))
        o_ref[0, pl.ds(start, qb), :] = jnp.concatenate(outs, axis=-1).astype(BF16)

    n_blocks = seq // qb
    n_peeled = 0 if has_cache else min(ATT_PAST // qb, n_blocks)
    for i in range(n_peeled):
        q_block(i * qb, ATT_PAST - i * qb)

    n_steady = n_blocks - n_peeled
    unroll = 2 if n_steady % 2 == 0 and n_steady > 0 else 1

    def steady(i, carry):
        for u in range(unroll):
            q_block(pl.multiple_of((n_peeled + i * unroll + u) * qb, qb), 0)
        return carry

    lax.fori_loop(0, n_steady // unroll, steady, 0)


def _attention(q, k, v, bias, cache, qb):
    b, seq, _ = q.shape
    nk = ATT_PAST + qb
    has_cache = cache is not None
    hb = ATT_WIDTH // MXU_DIM
    seq_spec = pl.BlockSpec((1, seq, MXU_DIM), lambda bi, hi: (bi, 0, hi))
    in_specs = [seq_spec, seq_spec, seq_spec,
                pl.BlockSpec((HEADS_PER_ATT_BLOCK, qb, nk), lambda bi, hi: (hi, 0, 0))]
    args = [q, k, v, bias]
    if has_cache:
        cspec = pl.BlockSpec((1, ATT_PAST, MXU_DIM), lambda bi, hi: (bi, 0, hi))
        in_specs += [cspec, cspec]
        args += list(cache)
    return pl.pallas_call(
        functools.partial(_attn_kernel, has_cache=has_cache, seq=seq, qb=qb),
        grid=(b, hb),
        in_specs=in_specs,
        out_specs=seq_spec,
        out_shape=jax.ShapeDtypeStruct((b, seq, ATT_WIDTH), BF16),
        scratch_shapes=[pltpu.VMEM((ATT_PAST + seq, MXU_DIM), BF16),
                        pltpu.VMEM((ATT_PAST + seq, MXU_DIM), BF16)],
        compiler_params=_params(("parallel", "parallel")),
        name="band_attention",
    )(*args)


def _attention_bias(table, qb):
    nk = ATT_PAST + qb
    n_diag = qb + nk - 1
    diag_rel = (ATT_PAST + qb - 1) - jnp.arange(n_diag)
    r = table[:, jnp.clip(diag_rel, -REL_CLIP, REL_CLIP) + REL_CLIP]
    h = table.shape[0]
    rp = jnp.pad(r, ((0, 0), (0, 1)))
    skew = jnp.tile(rp, (1, qb))[:, :qb * n_diag].reshape(h, qb, n_diag)
    bias = skew[:, :, qb - 1:qb - 1 + nk]
    qc = jnp.arange(qb)[:, None] // CHUNK
    kc = jnp.arange(nk)[None, :] // CHUNK
    visible = (kc >= qc) & (kc <= qc + ATT_PAST // CHUNK)
    return jnp.where(visible[None], bias, NEG_INF).astype(F32)


def _merge_kernel(x_ref, ya_ref, yb_ref, gates_ref, bg_ref, wa_ref, wb_ref, wo_ref, o_ref):
    gate = jax.nn.sigmoid(gates_ref[...].astype(F32) + bg_ref[...])
    pa = jnp.dot(ya_ref[...], wa_ref[...], preferred_element_type=F32)
    pb = jnp.dot(yb_ref[...], wb_ref[...], preferred_element_type=F32)
    t = gate[:, :D_MODEL] * pa + gate[:, D_MODEL:] * pb
    o_ref[...] = x_ref[...] + jnp.dot(t.astype(BF16), wo_ref[...], preferred_element_type=F32)


def _merge(x2d, ya, yb, gates, b_gate, wa, wb, wo, tm):
    t = x2d.shape[0]
    row = lambda w: pl.BlockSpec((tm, w), lambda i: (i, 0))
    return pl.pallas_call(
        _merge_kernel,
        grid=(t // tm,),
        in_specs=[row(D_MODEL), row(ATT_WIDTH), row(D_INNER), row(2 * D_MODEL),
                  _resident(b_gate.shape), _resident(wa.shape), _resident(wb.shape), _resident(wo.shape)],
        out_specs=row(D_MODEL),
        out_shape=jax.ShapeDtypeStruct((t, D_MODEL), F32),
        compiler_params=_params(("parallel",)),
        name="merge_outproj",
    )(x2d, ya, yb, gates, b_gate, wa, wb, wo)


def _ffn_kernel(x_ref, pe_ref, ctx_ref, gffn_ref, wup_ref, cw_ref, cb_ref, wdown_ref,
                gple_ref, wgate_ref, wple_ref, o_ref, ctx_out, rs, *, tm, tiles_per_batch):
    i = pl.program_id(0)

    @pl.when(i % tiles_per_batch == 0)
    def _load_context():
        for k in range(FFN_CONV - 1):
            rs[k, 0:SUBLANES, :] = ctx_ref[0, k * SUBLANES:(k + 1) * SUBLANES, :]

    x = x_ref[...]
    xn = (x * _rms_scale(x) * gffn_ref[...]).astype(BF16)
    n_slabs = tm // SUBLANES
    row_in_slab = lax.broadcasted_iota(jnp.int32, (tm, FFN_COL_BLOCK), 0) % SUBLANES

    def up(j, part):
        lo = part * D_FF + j * FFN_COL_BLOCK
        return jnp.dot(xn, wup_ref[:, lo:lo + FFN_COL_BLOCK], preferred_element_type=F32)

    def conv(u, j, part):
        lo = part * D_FF + j * FFN_COL_BLOCK
        cols = slice(lo, lo + FFN_COL_BLOCK)
        out = cb_ref[:, cols] + u * cw_ref[FFN_CONV - 1:FFN_CONV, cols]
        for k in range(1, FFN_CONV):
            rot = pltpu.roll(u.reshape(n_slabs, SUBLANES, FFN_COL_BLOCK), k, 1).reshape(tm, FFN_COL_BLOCK)
            rs[k - 1, SUBLANES:SUBLANES + tm, cols] = rot
            shifted = jnp.where(row_in_slab < k, rs[k - 1, 0:tm, cols], rot)
            out = out + shifted * cw_ref[FFN_CONV - 1 - k:FFN_CONV - k, cols]
        return out

    n_blocks = D_FF // FFN_COL_BLOCK
    acc = jnp.zeros((tm, D_MODEL), F32)
    pending = (up(0, 0), up(0, 1))
    for j in range(n_blocks):
        ug, uv = pending
        if j + 1 < n_blocks:
            pending = (up(j + 1, 0), up(j + 1, 1))
        hidden = (jax.nn.gelu(conv(ug, j, 0), approximate=True) * conv(uv, j, 1)).astype(BF16)
        acc = acc + jnp.dot(hidden, wdown_ref[j * FFN_COL_BLOCK:(j + 1) * FFN_COL_BLOCK, :],
                            preferred_element_type=F32)

    @pl.when(i % tiles_per_batch == tiles_per_batch - 1)
    def _emit_context():
        ctx_out[0] = rs[FFN_CONV - 2, tm:tm + SUBLANES, :]

    for k in range(FFN_CONV - 1):
        rs[k, 0:SUBLANES, :] = rs[k, tm:tm + SUBLANES, :]

    x2 = x + acc
    gate = jax.nn.sigmoid(jnp.dot((x2 * _rms_scale(x2) * gple_ref[...]).astype(BF16), wgate_ref[...],
                                  preferred_element_type=F32))
    o_ref[...] = x2 + gate * jnp.dot(pe_ref[...].astype(BF16), wple_ref[...], preferred_element_type=F32)


def _ffn(x2d, pe2d, ctx, prm, batch, tm):
    t = x2d.shape[0]
    tiles_per_batch = t // batch // tm
    row = lambda w: pl.BlockSpec((tm, w), lambda i: (i, 0))
    per_batch = lambda r: pl.BlockSpec((1, r, 2 * D_FF), lambda i: (i // tiles_per_batch, 0, 0))
    y, ctx_out = pl.pallas_call(
        functools.partial(_ffn_kernel, tm=tm, tiles_per_batch=tiles_per_batch),
        grid=(t // tm,),
        in_specs=[row(D_MODEL), row(PLE_DIM), per_batch((FFN_CONV - 1) * SUBLANES)]
        + [_resident(p.shape) for p in prm],
        out_specs=[row(D_MODEL), per_batch(SUBLANES)],
        out_shape=[jax.ShapeDtypeStruct((t, D_MODEL), F32),
                   jax.ShapeDtypeStruct((batch, SUBLANES, 2 * D_FF), F32)],
        scratch_shapes=[pltpu.VMEM((FFN_CONV - 1, SUBLANES + tm, 2 * D_FF), F32)],
        compiler_params=_params(("arbitrary",)),
        name="convffn_ple",
    )(x2d, pe2d, _conv_heads(ctx, FFN_CONV), *prm)
    return y, ctx_out[:, :FFN_CONV - 1]


def _ssd_params(w):
    return (w["conv_ssm_w"], w["conv_ssm_b"], w["dt_bias"], w["a_log"], w["d_skip"], w["g_ssm"], w["expand"])


def _layer(x, pe, kv_cache, ssm_ctx, h0, ffn_ctx, w, tm, tq):
    b, seq, _ = x.shape
    t = b * seq
    x2d = x.reshape(t, D_MODEL)
    as3 = lambda a: a.reshape(b, seq, -1)
    fuse_ssd = h0 is None and seq % tm == 0 and tm % CHUNK == 0
    if fuse_ssd:
        qn, kn, v, gates, k_tail, v_tail, yb, st, ctx_ssm = _inproj(x2d, w, tm, seq, ssm_ctx)
    else:
        qn, kn, v, gates, k_tail, v_tail, z, xbc, dt_raw = _inproj(x2d, w, tm, seq)
        seq_pad = -(-seq // CHUNK) * CHUNK
        valid_rows = CHUNK - (seq_pad - seq)
        pad = lambda a: jnp.pad(as3(a), ((0, 0), (0, seq_pad - seq), (0, 0)))
        yb, st, ctx_ssm = _ssd(pad(xbc), pad(dt_raw), pad(z), ssm_ctx, h0, _ssd_params(w), valid_rows)
        yb = yb[:, :seq].reshape(t, D_INNER)

    cache = None
    if kv_cache is not None:
        cache = tuple(ck.reshape(b, ATT_PAST, ATT_WIDTH) for ck in kv_cache)
    ya = _attention(as3(qn), as3(kn), as3(v), _attention_bias(w["rel_bias"], tq), cache, tq)

    x1 = _merge(x2d, ya.reshape(t, ATT_WIDTH), yb, gates, w["b_gate"], w["w_proj_a"], w["w_proj_b"], w["w_out"], tm)
    ffn_prm = (w["g_ffn"], w["w_up"], w["ffn_conv_w"], w["ffn_conv_b"], w["w_down"],
               w["g_ple"], w["w_ple_gate"], w["w_ple"])
    y, ctx_ffn = _ffn(x1, pe.reshape(t, PLE_DIM), ffn_ctx, ffn_prm, b, min(tm, seq))

    heads = lambda a: a.reshape(b, min(seq, ATT_PAST), N_HEADS_A, HEAD_DIM_A)
    state = st.reshape(b, N_HEADS_S, SSM_HEAD_DIM, D_STATE)
    return y.reshape(b, seq, D_MODEL), (heads(k_tail), heads(v_tail), state, ctx_ssm[:, :SSM_CONV - 1], ctx_ffn)


def _prep_weights(i, g_mix, w_in, b_gate, g_q, g_k, rel_bias, conv_ssm_w, conv_ssm_b, dt_bias, a_log, d_skip,
                  g_ssm, w_proj_a, w_proj_b, w_out, g_ffn, w_up, ffn_conv_w, ffn_conv_b, w_down, g_ple,
                  w_ple_gate, w_ple):
    row = lambda a: a[i].reshape(1, -1).astype(F32)
    pad_lanes = lambda a, n: jnp.pad(a, ((0, 0), (0, n - a.shape[1])))
    wi = w_in[i].astype(BF16)
    head_of_lane = jnp.arange(D_INNER) // SSM_HEAD_DIM
    expand1 = (jnp.arange(DT_PAD)[:, None] == head_of_lane[None, :]).astype(BF16)
    blk = jnp.arange(MXU_DIM) // HEAD_DIM_A
    return {
        "g_mix": row(g_mix), "w_in": wi, "w_gates": wi[:, _SEG_GATES[0]:_SEG_GATES[1]],
        "w_dt": pad_lanes(wi[:, _SEG_DT[0]:_SEG_DT[1]], DT_PAD), "b_gate": row(b_gate),
        "g_q": jnp.tile(row(g_q), (1, N_HEADS_A)), "g_k": jnp.tile(row(g_k), (1, N_HEADS_A)),
        "hsum": (blk[:, None] == blk[None, :]).astype(BF16),
        "rel_bias": rel_bias[i].astype(F32),
        "conv_ssm_w": jnp.pad(conv_ssm_w[i], ((0, SUBLANES - SSM_CONV), (0, 0))), "conv_ssm_b": row(conv_ssm_b),
        "dt_bias": pad_lanes(row(dt_bias), DT_PAD), "a_log": pad_lanes(row(a_log), DT_PAD),
        "d_skip": jnp.repeat(row(d_skip), SSM_HEAD_DIM, axis=1), "g_ssm": row(g_ssm),
        "expand": jnp.concatenate([expand1] * 3, axis=0),
        "w_proj_a": w_proj_a[i].astype(BF16), "w_proj_b": w_proj_b[i].astype(BF16), "w_out": w_out[i].astype(BF16),
        "g_ffn": row(g_ffn), "w_up": w_up[i].astype(BF16),
        "ffn_conv_w": jnp.pad(ffn_conv_w[i], ((0, SUBLANES - FFN_CONV), (0, 0))), "ffn_conv_b": row(ffn_conv_b),
        "w_down": w_down[i].astype(BF16), "g_ple": row(g_ple),
        "w_ple_gate": w_ple_gate[i].astype(BF16), "w_ple": w_ple[i].astype(BF16),
    }


def kernel(x_prompt, x_sample, cache_k, cache_v, state_ssm, state_conv_ssm, state_conv_ffn, p_prompt, p_sample, g_mix, w_in, b_gate, g_q, g_k, rel_bias, conv_ssm_w, conv_ssm_b, dt_bias, a_log, d_skip, g_ssm, w_proj_a, w_proj_b, w_out, g_ffn, w_up, ffn_conv_w, ffn_conv_b, w_down, g_ple, w_ple_gate, w_ple):
    depth = w_in.shape[0]
    bp = x_prompt.shape[0]
    bs, seq_s = x_sample.shape[:2]
    yp, ys = x_prompt, x_sample
    sp, ss = [], []
    for i in range(depth):
        w = _prep_weights(i, g_mix, w_in, b_gate, g_q, g_k, rel_bias, conv_ssm_w, conv_ssm_b, dt_bias, a_log,
                          d_skip, g_ssm, w_proj_a, w_proj_b, w_out, g_ffn, w_up, ffn_conv_w, ffn_conv_b, w_down,
                          g_ple, w_ple_gate, w_ple)
        yp, st_p = _layer(yp, p_prompt[i], None,
                          jnp.zeros((bp, SSM_CONV - 1, CONV_DIM), F32), None,
                          jnp.zeros((bp, FFN_CONV - 1, 2 * D_FF), F32), w, tm=256, tq=4 * CHUNK)
        h0 = state_ssm[i].reshape(bs, D_INNER, D_STATE)
        ys, st_s = _layer(ys, p_sample[i], (cache_k[i], cache_v[i]), state_conv_ssm[i], h0,
                          state_conv_ffn[i], w, tm=seq_s * bs, tq=seq_s)
        sp.append(st_p)
        ss.append(st_s)

    stack = lambda lst, j: jnp.stack([s[j] for s in lst], axis=0)
    return (yp, ys,
            stack(sp, 0), stack(sp, 1), stack(sp, 2), stack(sp, 3), stack(sp, 4),
            stack(ss, 0), stack(ss, 1), stack(ss, 2), stack(ss, 3), stack(ss, 4))
```

```python
import functools

import jax
import jax.numpy as jnp
from jax import lax
from jax.experimental import pallas as pl
from jax.experimental.pallas import tpu as pltpu

F32 = jnp.float32
BF16 = jnp.bfloat16

D_MODEL = 1024
CHUNK = 64
PLE_DIM = 256
EPS = 1e-6
NEG_INF = -1e30
N_HEADS_A = 16
HEAD_DIM_A = 64
ATT_WIDTH = N_HEADS_A * HEAD_DIM_A
ATT_PAST = 8 * CHUNK
REL_CLIP = 128
D_INNER = 2 * D_MODEL
SSM_HEAD_DIM = 64
N_HEADS_S = D_INNER // SSM_HEAD_DIM
N_GROUPS = 4
GROUP_WIDTH = D_INNER // N_GROUPS
D_STATE = 128
SSM_CONV = 4
CONV_DIM = D_INNER + 2 * N_GROUPS * D_STATE
D_FF = 3 * D_MODEL
FFN_CONV = 3

LANES = 128
SUBLANES = 8
MXU_DIM = 256
VMEM_LIMIT_BYTES = 56 * 1024 * 1024

DT_PAD = LANES
HEADS_PER_MXU_TILE = MXU_DIM // SSM_HEAD_DIM
FFN_COL_BLOCK = 1024

_SEG_Q = (0, ATT_WIDTH)
_SEG_K = (ATT_WIDTH, 2 * ATT_WIDTH)
_SEG_V = (2 * ATT_WIDTH, 3 * ATT_WIDTH)
_SEG_Z = (3 * ATT_WIDTH, 3 * ATT_WIDTH + D_INNER)
_SEG_XBC = (_SEG_Z[1], _SEG_Z[1] + CONV_DIM)
_SEG_DT = (_SEG_XBC[1], _SEG_XBC[1] + N_HEADS_S)
_SEG_GATES = (_SEG_DT[1], _SEG_DT[1] + 2 * D_MODEL)


def _resident(shape):
    nd = len(shape)
    return pl.BlockSpec(shape, lambda *_: (0,) * nd, pipeline_mode=pl.Buffered(1))


def _params(semantics):
    return pltpu.CompilerParams(dimension_semantics=semantics, vmem_limit_bytes=VMEM_LIMIT_BYTES)


def _rms_scale(x):
    return lax.rsqrt(jnp.mean(x * x, axis=-1, keepdims=True) + EPS)


def _conv_heads(ctx, width):
    heads = [jnp.pad(ctx[:, width - 1 - k:], ((0, 0), (0, SUBLANES - k), (0, 0))) for k in range(1, width)]
    return jnp.concatenate(heads, axis=1)


def _softplus(x):
    return jnp.maximum(x, 0.0) + jnp.log(1.0 + jnp.exp(-jnp.abs(x)))


def _split3(v):
    hi = v.astype(BF16)
    r1 = v - hi.astype(F32)
    mid = r1.astype(BF16)
    lo = (r1 - mid.astype(F32)).astype(BF16)
    return jnp.concatenate([hi, mid, lo], axis=-1)


def _ssd_load_carry(ctx_ref, h0_ref, state, heads):
    for k in range(SSM_CONV - 1):
        heads[k] = ctx_ref[0, k * SUBLANES:(k + 1) * SUBLANES, :]
    for g in range(N_GROUPS):
        if h0_ref is None:
            state[g] = jnp.zeros((D_STATE, GROUP_WIDTH), F32)
        else:
            state[g] = h0_ref[0, g * GROUP_WIDTH:(g + 1) * GROUP_WIDTH, :].T


def _ssd_emit_carry(st_out, ctx_out, state, heads):
    ctx_out[0] = heads[SSM_CONV - 2]
    for g in range(N_GROUPS):
        st_out[0, g * GROUP_WIDTH:(g + 1) * GROUP_WIDTH, :] = state[g].T


def _ssd_conv(x_raw, prm, heads, valid_rows):
    cw_ref, cb_ref = prm[:2]
    n_rows = x_raw.shape[0]
    row_in_slab = lax.broadcasted_iota(jnp.int32, (n_rows, CONV_DIM), 0) % SUBLANES
    acc = cb_ref[...] + x_raw * cw_ref[SSM_CONV - 1:SSM_CONV, :]
    for k in range(1, SSM_CONV):
        rot = pltpu.roll(x_raw.reshape(n_rows // SUBLANES, SUBLANES, CONV_DIM), k, 1).reshape(n_rows, CONV_DIM)
        above = jnp.concatenate([heads[k - 1], rot[:n_rows - SUBLANES]], axis=0)
        acc = acc + jnp.where(row_in_slab < k, above, rot) * cw_ref[SSM_CONV - 1 - k:SSM_CONV - k, :]
        heads[k - 1] = rot[valid_rows - SUBLANES:valid_rows]
    return acc * jax.nn.sigmoid(acc)


def _ssd_steps(dt_raw, prm, valid_rows):
    dtb_ref, alog_ref, expand_ref = prm[2], prm[3], prm[6]
    n_rows = dt_raw.shape[0]
    dt = _softplus(dt_raw + dtb_ref[...])
    if valid_rows < n_rows:
        rows = lax.broadcasted_iota(jnp.int32, (n_rows, DT_PAD), 0)
        dt = jnp.where(rows < valid_rows, dt, 0.0)
    ac = dt * (-jnp.exp(alog_ref[...]))
    ri = lax.broadcasted_iota(jnp.int32, (n_rows, n_rows), 0)
    ci = lax.broadcasted_iota(jnp.int32, (n_rows, n_rows), 1)
    same_chunk_causal = jnp.logical_and(ri // CHUNK == ci // CHUNK, ri >= ci)
    acs = jnp.dot(same_chunk_causal.astype(F32), ac, preferred_element_type=F32,
                  precision=lax.Precision.HIGHEST)
    expand = lambda v: jnp.dot(_split3(v), expand_ref[...], preferred_element_type=F32)
    return expand(acs), expand(dt)


def _ssd_chunk(xc, acol, dtcol, z, prm, state):
    dskip_ref, gssm_ref = prm[4], prm[5]
    q = CHUNK

    xs = xc[:, :D_INNER]
    bm = xc[:, D_INNER:D_INNER + N_GROUPS * D_STATE]
    cm = xc[:, D_INNER + N_GROUPS * D_STATE:].astype(BF16)

    last = acol[q - 1:q, :]
    dx = dtcol * xs
    wj = (jnp.exp(last - acol) * dx).astype(BF16)
    ea = jnp.exp(acol)

    lane = lax.broadcasted_iota(jnp.int32, (q, D_INNER), 1)
    rowi = lax.broadcasted_iota(jnp.int32, (q, D_INNER), 0)
    arow = jnp.sum(jnp.where(lane % q == rowi, acol, 0.0), axis=0, keepdims=True)

    tri = (lax.broadcasted_iota(jnp.int32, (q, MXU_DIM), 0)
           >= lax.broadcasted_iota(jnp.int32, (q, MXU_DIM), 1) % q)
    br = lax.broadcasted_iota(jnp.int32, (MXU_DIM, MXU_DIM), 0) // SSM_HEAD_DIM
    bc = lax.broadcasted_iota(jnp.int32, (MXU_DIM, MXU_DIM), 1) // SSM_HEAD_DIM
    blockdiag = br == bc

    y_blocks = []
    for g in range(N_GROUPS):
        cg = cm[:, g * D_STATE:(g + 1) * D_STATE]
        bg = bm[:, g * D_STATE:(g + 1) * D_STATE].astype(BF16)
        bg4 = jnp.concatenate([bg] * HEADS_PER_MXU_TILE, axis=0)
        cb4 = lax.dot_general(cg, bg4, (((1,), (1,)), ((), ())), preferred_element_type=F32)
        sg = state[g].astype(BF16)
        for half in range(GROUP_WIDTH // MXU_DIM):
            lo = g * GROUP_WIDTH + half * MXU_DIM
            cols = slice(lo, lo + MXU_DIM)
            seg = acol[:, cols] - arow[:, cols]
            decay = jnp.where(tri, jnp.exp(jnp.where(tri, seg, 0.0)), 0.0)
            mp = (cb4 * decay).astype(BF16)
            dxb = dx[:, cols].astype(BF16)
            bd = jnp.where(blockdiag, jnp.concatenate([dxb] * HEADS_PER_MXU_TILE, axis=0),
                           jnp.zeros((), BF16))
            y = jnp.dot(mp, bd, preferred_element_type=F32)
            y = y + jnp.dot(cg, sg[:, half * MXU_DIM:(half + 1) * MXU_DIM],
                            preferred_element_type=F32) * ea[:, cols]
            y_blocks.append(y + dskip_ref[:, cols] * xs[:, cols])

    el = jnp.exp(last)
    for g in range(N_GROUPS):
        cols = slice(g * GROUP_WIDTH, (g + 1) * GROUP_WIDTH)
        bgt = bm[:, g * D_STATE:(g + 1) * D_STATE].T.astype(BF16)
        state[g] = state[g] * el[:, cols] + jnp.dot(bgt, wj[:, cols], preferred_element_type=F32)

    zf = z.astype(F32)
    yg = jnp.concatenate(y_blocks, axis=-1) * (zf * jax.nn.sigmoid(zf))
    return (yg * _rms_scale(yg) * gssm_ref[...]).astype(BF16)


def _ssd_scratch():
    return [pltpu.VMEM((N_GROUPS, D_STATE, GROUP_WIDTH), F32),
            pltpu.VMEM((SSM_CONV - 1, SUBLANES, CONV_DIM), F32)]


def _ssd_kernel(*refs, has_state, valid_rows, n_chunks):
    if has_state:
        xbc_ref, dt_ref, z_ref, ctx_ref, h0_ref = refs[:5]
        refs = refs[5:]
    else:
        xbc_ref, dt_ref, z_ref, ctx_ref = refs[:4]
        h0_ref = None
        refs = refs[4:]
    prm, (yb_out, st_out, ctx_out, state, heads) = refs[:7], refs[7:]
    c = pl.program_id(1)

    @pl.when(c == 0)
    def _load_carry():
        _ssd_load_carry(ctx_ref, h0_ref, state, heads)

    xc = _ssd_conv(xbc_ref[0], prm, heads, valid_rows)
    acol, dtcol = _ssd_steps(dt_ref[0], prm, valid_rows)
    yb_out[0] = _ssd_chunk(xc, acol, dtcol, z_ref[0], prm, state)

    @pl.when(c == n_chunks - 1)
    def _emit_carry():
        _ssd_emit_carry(st_out, ctx_out, state, heads)


def _ssd(xbc, dt_raw, z, ctx, h0, prm, valid_rows):
    b, seq, _ = xbc.shape
    n_chunks = seq // CHUNK
    has_state = h0 is not None
    chunk = lambda w: pl.BlockSpec((1, CHUNK, w), lambda bi, ci: (bi, ci, 0))
    per_batch = lambda r, w: pl.BlockSpec((1, r, w), lambda bi, ci: (bi, 0, 0))
    in_specs = [chunk(CONV_DIM), chunk(DT_PAD), chunk(D_INNER), per_batch((SSM_CONV - 1) * SUBLANES, CONV_DIM)]
    args = [xbc, dt_raw, z, _conv_heads(ctx, SSM_CONV)]
    if has_state:
        in_specs.append(per_batch(D_INNER, D_STATE))
        args.append(h0)
    in_specs += [_resident(p.shape) for p in prm]
    args += list(prm)
    return pl.pallas_call(
        functools.partial(_ssd_kernel, has_state=has_state, valid_rows=valid_rows, n_chunks=n_chunks),
        grid=(b, n_chunks),
        in_specs=in_specs,
        out_specs=[chunk(D_INNER), per_batch(D_INNER, D_STATE), per_batch(SUBLANES, CONV_DIM)],
        out_shape=[jax.ShapeDtypeStruct((b, seq, D_INNER), BF16),
                   jax.ShapeDtypeStruct((b, D_INNER, D_STATE), F32),
                   jax.ShapeDtypeStruct((b, SUBLANES, CONV_DIM), F32)],
        scratch_shapes=_ssd_scratch(),
        compiler_params=_params(("parallel", "arbitrary")),
        name="ssd",
    )(*args)


def _inproj_kernel(*refs, tiles_per_batch, tail_tiles, fuse_ssd, tm):
    x_ref, g_ref, w_ref, wgates_ref, wdt_ref, gq_ref, gk_ref, hsum_ref = refs[:8]
    if fuse_ssd:
        ctx_ref, prm = refs[8], refs[9:16]
        q_out, k_out, v_out, gates_out, ktail_out, vtail_out, yb_out, st_out, ctx_out, state, heads = refs[16:]
    else:
        q_out, k_out, v_out, gates_out, ktail_out, vtail_out, z_out, xbc_out, dt_out = refs[8:]
    tile_in_batch = pl.program_id(0) % tiles_per_batch

    if fuse_ssd:
        @pl.when(tile_in_batch == 0)
        def _load_carry():
            _ssd_load_carry(ctx_ref, None, state, heads)

    x = x_ref[...]
    h = (x * _rms_scale(x) * g_ref[...]).astype(BF16)

    def seg(bounds):
        return jnp.dot(h, w_ref[:, bounds[0]:bounds[1]], preferred_element_type=F32)

    def head_norm(t, g):
        sq = (t * t).astype(BF16)
        parts = []
        for blk in range(ATT_WIDTH // MXU_DIM):
            cols = slice(blk * MXU_DIM, (blk + 1) * MXU_DIM)
            ss = jnp.dot(sq[:, cols], hsum_ref[...], preferred_element_type=F32)
            parts.append(t[:, cols] * lax.rsqrt(ss * (1.0 / HEAD_DIM_A) + EPS))
        return jnp.concatenate(parts, axis=-1) * g

    xbc = seg(_SEG_XBC)
    zb = seg(_SEG_Z).astype(BF16)
    dt_raw = jnp.dot(h, wdt_ref[...], preferred_element_type=F32)
    if fuse_ssd:
        xc = _ssd_conv(xbc, prm, heads, tm)
        acol, dtcol = _ssd_steps(dt_raw, prm, tm)
        for c in range(tm // CHUNK):
            rows = slice(c * CHUNK, (c + 1) * CHUNK)
            yb_out[rows, :] = _ssd_chunk(xc[rows, :], acol[rows, :], dtcol[rows, :], zb[rows, :], prm, state)
    else:
        xbc_out[...] = xbc
        z_out[...] = zb
        dt_out[...] = dt_raw

    q_out[...] = (head_norm(seg(_SEG_Q), gq_ref[...]) * (HEAD_DIM_A ** -0.5)).astype(BF16)
    k = head_norm(seg(_SEG_K), gk_ref[...])
    v = seg(_SEG_V)
    k_out[...] = k.astype(BF16)
    v_out[...] = v.astype(BF16)
    gates_out[...] = jnp.dot(h, wgates_ref[...], preferred_element_type=F32).astype(BF16)

    @pl.when(tile_in_batch >= tiles_per_batch - tail_tiles)
    def _emit_tail():
        ktail_out[...] = k
        vtail_out[...] = v

    if fuse_ssd:
        @pl.when(tile_in_batch == tiles_per_batch - 1)
        def _emit_carry():
            _ssd_emit_carry(st_out, ctx_out, state, heads)


def _inproj(x2d, w, tm, seq, ssm_ctx=None):
    t = x2d.shape[0]
    fuse_ssd = ssm_ctx is not None
    tiles_per_batch = max(seq // tm, 1)
    tail_tiles = max(min(ATT_PAST, seq) // tm, 1)
    n_tiles = t // tm
    n_batches = n_tiles // tiles_per_batch
    row = lambda width: pl.BlockSpec((tm, width), lambda i: (i, 0))
    first_tail = tiles_per_batch - tail_tiles
    tail = pl.BlockSpec((tm, ATT_WIDTH), lambda i: (
        (i // tiles_per_batch) * tail_tiles + jnp.maximum(i % tiles_per_batch - first_tail, 0), 0))
    per_batch = lambda r, width: pl.BlockSpec((1, r, width), lambda i: (i // tiles_per_batch, 0, 0))
    rows_out = lambda width, dtype: (row(width), jax.ShapeDtypeStruct((t, width), dtype))
    tail_out = (tail, jax.ShapeDtypeStruct((n_batches * tail_tiles * tm, ATT_WIDTH), F32))

    in_specs = [row(D_MODEL), _resident((1, D_MODEL)), _resident((D_MODEL, _SEG_XBC[1])),
                _resident(w["w_gates"].shape), _resident(w["w_dt"].shape),
                _resident((1, ATT_WIDTH)), _resident((1, ATT_WIDTH)), _resident((MXU_DIM, MXU_DIM))]
    args = [x2d, w["g_mix"], w["w_in"], w["w_gates"], w["w_dt"], w["g_q"], w["g_k"], w["hsum"]]
    outs = [rows_out(ATT_WIDTH, BF16), rows_out(ATT_WIDTH, BF16), rows_out(ATT_WIDTH, BF16),
            rows_out(2 * D_MODEL, BF16), tail_out, tail_out]
    scratch = []
    if fuse_ssd:
        assert seq % tm == 0 and tm % CHUNK == 0
        prm = _ssd_params(w)
        in_specs += [per_batch((SSM_CONV - 1) * SUBLANES, CONV_DIM)] + [_resident(p.shape) for p in prm]
        args += [_conv_heads(ssm_ctx, SSM_CONV)] + list(prm)
        outs += [rows_out(D_INNER, BF16),
                 (per_batch(D_INNER, D_STATE), jax.ShapeDtypeStruct((n_batches, D_INNER, D_STATE), F32)),
                 (per_batch(SUBLANES, CONV_DIM), jax.ShapeDtypeStruct((n_batches, SUBLANES, CONV_DIM), F32))]
        scratch = _ssd_scratch()
    else:
        outs += [rows_out(D_INNER, BF16), rows_out(CONV_DIM, F32), rows_out(DT_PAD, F32)]
    return pl.pallas_call(
        functools.partial(_inproj_kernel, tiles_per_batch=tiles_per_batch, tail_tiles=tail_tiles,
                          fuse_ssd=fuse_ssd, tm=tm),
        grid=(n_tiles,),
        in_specs=in_specs,
        out_specs=[o[0] for o in outs],
        out_shape=[o[1] for o in outs],
        scratch_shapes=scratch,
        compiler_params=_params(("arbitrary",)),
        name="inproj_ssd" if fuse_ssd else "inproj",
    )(*args)


HEADS_PER_ATT_BLOCK = MXU_DIM // HEAD_DIM_A
HEADS_PER_LANE_TILE = LANES // HEAD_DIM_A


def _attn_kernel(*refs, has_cache, seq, qb):
    if has_cache:
        q_ref, k_ref, v_ref, bias_ref, ck_ref, cv_ref, o_ref, kp, vp = refs
    else:
        q_ref, k_ref, v_ref, bias_ref, o_ref, kp, vp = refs
    nk = ATT_PAST + qb

    if has_cache:
        kp[0:ATT_PAST, :] = ck_ref[0].astype(BF16)
        vp[0:ATT_PAST, :] = cv_ref[0].astype(BF16)
    kp[ATT_PAST:ATT_PAST + seq, :] = k_ref[0]
    vp[ATT_PAST:ATT_PAST + seq, :] = v_ref[0]

    lane = lax.broadcasted_iota(jnp.int32, (qb, LANES), 1)
    low_half = lane < HEAD_DIM_A

    def q_block(start, first_col):
        kb = kp[pl.ds(start + first_col, nk - first_col), :]
        vb = vp[pl.ds(start + first_col, nk - first_col), :]
        q = q_ref[0, pl.ds(start, qb), :]
        outs = []
        for pair in range(HEADS_PER_ATT_BLOCK // HEADS_PER_LANE_TILE):
            cols = slice(pair * LANES, (pair + 1) * LANES)
            qp, kpair, vpair = q[:, cols], kb[:, cols], vb[:, cols]
            zero = jnp.zeros((), BF16)
            q2 = jnp.concatenate([jnp.where(low_half, qp, zero), jnp.where(low_half, zero, qp)], axis=0)
            s = lax.dot_general(q2, kpair, (((1,), (1,)), ((), ())), preferred_element_type=F32)
            first_head = pair * HEADS_PER_LANE_TILE
            bias = bias_ref[first_head:first_head + HEADS_PER_LANE_TILE, :, first_col:]
            s = s + bias.reshape(HEADS_PER_LANE_TILE * qb, nk - first_col)
            m = jnp.max(s, axis=-1, keepdims=True)
            p = jnp.exp(s - m)
            denom = jnp.sum(p, axis=-1, keepdims=True)
            o2 = jnp.dot(p.astype(BF16), vpair, preferred_element_type=F32) / denom
            outs.append(jnp.where(low_half, o2[:qb], o2[qb:]))
        o_ref[0, pl.ds(start, qb), :] = jnp.concatenate(outs, axis=-1).astype(BF16)

    n_blocks = seq // qb
    n_peeled = 0 if has_cache else min(ATT_PAST // qb, n_blocks)
    for i in range(n_peeled):
        q_block(i * qb, ATT_PAST - i * qb)

    n_steady = n_blocks - n_peeled
    unroll = 2 if n_steady % 2 == 0 and n_steady > 0 else 1

    def steady(i, carry):
        for u in range(unroll):
            q_block(pl.multiple_of((n_peeled + i * unroll + u) * qb, qb), 0)
        return carry

    lax.fori_loop(0, n_steady // unroll, steady, 0)


def _attention(q, k, v, bias, cache, qb):
    b, seq, _ = q.shape
    nk = ATT_PAST + qb
    has_cache = cache is not None
    hb = ATT_WIDTH // MXU_DIM
    seq_spec = pl.BlockSpec((1, seq, MXU_DIM), lambda bi, hi: (bi, 0, hi))
    in_specs = [seq_spec, seq_spec, seq_spec,
                pl.BlockSpec((HEADS_PER_ATT_BLOCK, qb, nk), lambda bi, hi: (hi, 0, 0))]
    args = [q, k, v, bias]
    if has_cache:
        cspec = pl.BlockSpec((1, ATT_PAST, MXU_DIM), lambda bi, hi: (bi, 0, hi))
        in_specs += [cspec, cspec]
        args += list(cache)
    return pl.pallas_call(
        functools.partial(_attn_kernel, has_cache=has_cache, seq=seq, qb=qb),
        grid=(b, hb),
        in_specs=in_specs,
        out_specs=seq_spec,
        out_shape=jax.ShapeDtypeStruct((b, seq, ATT_WIDTH), BF16),
        scratch_shapes=[pltpu.VMEM((ATT_PAST + seq, MXU_DIM), BF16),
                        pltpu.VMEM((ATT_PAST + seq, MXU_DIM), BF16)],
        compiler_params=_params(("parallel", "parallel")),
        name="band_attention",
    )(*args)


def _attention_bias(table, qb):
    nk = ATT_PAST + qb
    n_diag = qb + nk - 1
    diag_rel = (ATT_PAST + qb - 1) - jnp.arange(n_diag)
    r = table[:, jnp.clip(diag_rel, -REL_CLIP, REL_CLIP) + REL_CLIP]
    h = table.shape[0]
    row_len = -(-(n_diag + 1) // LANES) * LANES
    rp = jnp.pad(r, ((0, 0), (1, row_len - n_diag)))
    skew = jnp.tile(rp, (1, qb))[:, :qb * row_len].reshape(h, qb, row_len)
    bias = skew[:, :, qb:qb + nk]
    qc = jnp.arange(qb)[:, None] // CHUNK
    kc = jnp.arange(nk)[None, :] // CHUNK
    visible = (kc >= qc) & (kc <= qc + ATT_PAST // CHUNK)
    return jnp.where(visible[None], bias, NEG_INF).astype(F32)


def _merge_kernel(x_ref, ya_ref, yb_ref, gates_ref, bg_ref, wa_ref, wb_ref, wo_ref, o_ref):
    gate = jax.nn.sigmoid(gates_ref[...].astype(F32) + bg_ref[...])
    pa = jnp.dot(ya_ref[...], wa_ref[...], preferred_element_type=F32)
    pb = jnp.dot(yb_ref[...], wb_ref[...], preferred_element_type=F32)
    t = gate[:, :D_MODEL] * pa + gate[:, D_MODEL:] * pb
    o_ref[...] = x_ref[...] + jnp.dot(t.astype(BF16), wo_ref[...], preferred_element_type=F32)


def _merge(x2d, ya, yb, gates, b_gate, wa, wb, wo, tm):
    t = x2d.shape[0]
    row = lambda w: pl.BlockSpec((tm, w), lambda i: (i, 0))
    return pl.pallas_call(
        _merge_kernel,
        grid=(t // tm,),
        in_specs=[row(D_MODEL), row(ATT_WIDTH), row(D_INNER), row(2 * D_MODEL),
                  _resident(b_gate.shape), _resident(wa.shape), _resident(wb.shape), _resident(wo.shape)],
        out_specs=row(D_MODEL),
        out_shape=jax.ShapeDtypeStruct((t, D_MODEL), F32),
        compiler_params=_params(("parallel",)),
        name="merge_outproj",
    )(x2d, ya, yb, gates, b_gate, wa, wb, wo)


def _ffn_kernel(x_ref, pe_ref, ctx_ref, gffn_ref, wup_ref, cw_ref, cb_ref, wdown_ref,
                gple_ref, wgate_ref, wple_ref, o_ref, ctx_out, heads, *, tm, seq_rows, tiles_per_batch):
    i = pl.program_id(0)
    n_seqs = tm // seq_rows

    @pl.when(i % tiles_per_batch == 0)
    def _load_context():
        for k in range(FFN_CONV - 1):
            heads[k] = ctx_ref[:, k * SUBLANES:(k + 1) * SUBLANES, :]

    x = x_ref[...]
    xn = (x * _rms_scale(x) * gffn_ref[...]).astype(BF16)
    n_slabs = tm // SUBLANES
    row_in_slab = lax.broadcasted_iota(jnp.int32, (tm, FFN_COL_BLOCK), 0) % SUBLANES

    def up(j, part):
        lo = part * D_FF + j * FFN_COL_BLOCK
        return jnp.dot(xn, wup_ref[:, lo:lo + FFN_COL_BLOCK], preferred_element_type=F32)

    def conv(u, j, part):
        lo = part * D_FF + j * FFN_COL_BLOCK
        cols = slice(lo, lo + FFN_COL_BLOCK)
        out = cb_ref[:, cols] + u * cw_ref[FFN_CONV - 1:FFN_CONV, cols]
        for k in range(1, FFN_CONV):
            rot = pltpu.roll(u.reshape(n_slabs, SUBLANES, FFN_COL_BLOCK), k, 1).reshape(tm, FFN_COL_BLOCK)
            pieces = []
            for s in range(n_seqs):
                pieces += [heads[k - 1, s, :, cols], rot[s * seq_rows:(s + 1) * seq_rows - SUBLANES]]
            shifted = jnp.where(row_in_slab < k, jnp.concatenate(pieces, axis=0), rot)
            out = out + shifted * cw_ref[FFN_CONV - 1 - k:FFN_CONV - k, cols]
            for s in range(n_seqs):
                heads[k - 1, s, :, cols] = rot[(s + 1) * seq_rows - SUBLANES:(s + 1) * seq_rows]
        return out

    n_blocks = D_FF // FFN_COL_BLOCK
    acc = jnp.zeros((tm, D_MODEL), F32)
    pending = (up(0, 0), up(0, 1))
    for j in range(n_blocks):
        ug, uv = pending
        if j + 1 < n_blocks:
            pending = (up(j + 1, 0), up(j + 1, 1))
        hidden = (jax.nn.gelu(conv(ug, j, 0), approximate=True) * conv(uv, j, 1)).astype(BF16)
        acc = acc + jnp.dot(hidden, wdown_ref[j * FFN_COL_BLOCK:(j + 1) * FFN_COL_BLOCK, :],
                            preferred_element_type=F32)

    @pl.when(i % tiles_per_batch == tiles_per_batch - 1)
    def _emit_context():
        ctx_out[...] = heads[FFN_CONV - 2]

    x2 = x + acc
    gate = jax.nn.sigmoid(jnp.dot((x2 * _rms_scale(x2) * gple_ref[...]).astype(BF16), wgate_ref[...],
                                  preferred_element_type=F32))
    o_ref[...] = x2 + gate * jnp.dot(pe_ref[...].astype(BF16), wple_ref[...], preferred_element_type=F32)


def _ffn(x2d, pe2d, ctx, prm, seq, tm):
    t = x2d.shape[0]
    batch = t // seq
    seq_rows = min(seq, tm)
    seqs_per_tile = tm // seq_rows
    tiles_per_batch = seq // seq_rows
    row = lambda w: pl.BlockSpec((tm, w), lambda i: (i, 0))
    per_seq = lambda r: pl.BlockSpec((seqs_per_tile, r, 2 * D_FF), lambda i: (i // tiles_per_batch, 0, 0))
    y, ctx_out = pl.pallas_call(
        functools.partial(_ffn_kernel, tm=tm, seq_rows=seq_rows, tiles_per_batch=tiles_per_batch),
        grid=(t // tm,),
        in_specs=[row(D_MODEL), row(PLE_DIM), per_seq((FFN_CONV - 1) * SUBLANES)]
        + [_resident(p.shape) for p in prm],
        out_specs=[row(D_MODEL), per_seq(SUBLANES)],
        out_shape=[jax.ShapeDtypeStruct((t, D_MODEL), F32),
                   jax.ShapeDtypeStruct((batch, SUBLANES, 2 * D_FF), F32)],
        scratch_shapes=[pltpu.VMEM((FFN_CONV - 1, seqs_per_tile, SUBLANES, 2 * D_FF), F32)],
        compiler_params=_params(("arbitrary",)),
        name="convffn_ple",
    )(x2d, pe2d, _conv_heads(ctx, FFN_CONV), *prm)
    return y, ctx_out[:, :FFN_CONV - 1]


def _ssd_params(w):
    return (w["conv_ssm_w"], w["conv_ssm_b"], w["dt_bias"], w["a_log"], w["d_skip"], w["g_ssm"], w["expand"])


def _layer(x, pe, kv_cache, ssm_ctx, h0, ffn_ctx, w, tm, tq, tm_ffn):
    b, seq, _ = x.shape
    t = b * seq
    x2d = x.reshape(t, D_MODEL)
    as3 = lambda a: a.reshape(b, seq, -1)
    fuse_ssd = h0 is None and seq % tm == 0 and tm % CHUNK == 0
    if fuse_ssd:
        qn, kn, v, gates, k_tail, v_tail, yb, st, ctx_ssm = _inproj(x2d, w, tm, seq, ssm_ctx)
    else:
        qn, kn, v, gates, k_tail, v_tail, z, xbc, dt_raw = _inproj(x2d, w, tm, seq)
        seq_pad = -(-seq // CHUNK) * CHUNK
        valid_rows = CHUNK - (seq_pad - seq)
        pad = lambda a: jnp.pad(as3(a), ((0, 0), (0, seq_pad - seq), (0, 0)))
        yb, st, ctx_ssm = _ssd(pad(xbc), pad(dt_raw), pad(z), ssm_ctx, h0, _ssd_params(w), valid_rows)
        yb = yb[:, :seq].reshape(t, D_INNER)

    cache = None
    if kv_cache is not None:
        cache = tuple(ck.reshape(b, ATT_PAST, ATT_WIDTH) for ck in kv_cache)
    ya = _attention(as3(qn), as3(kn), as3(v), _attention_bias(w["rel_bias"], tq), cache, tq)

    x1 = _merge(x2d, ya.reshape(t, ATT_WIDTH), yb, gates, w["b_gate"], w["w_proj_a"], w["w_proj_b"], w["w_out"], tm)
    ffn_prm = (w["g_ffn"], w["w_up"], w["ffn_conv_w"], w["ffn_conv_b"], w["w_down"],
               w["g_ple"], w["w_ple_gate"], w["w_ple"])
    y, ctx_ffn = _ffn(x1, pe.reshape(t, PLE_DIM), ffn_ctx, ffn_prm, seq, tm_ffn)

    heads = lambda a: a.reshape(b, min(seq, ATT_PAST), N_HEADS_A, HEAD_DIM_A)
    state = st.reshape(b, N_HEADS_S, SSM_HEAD_DIM, D_STATE)
    return y.reshape(b, seq, D_MODEL), (heads(k_tail), heads(v_tail), state, ctx_ssm[:, :SSM_CONV - 1], ctx_ffn)


def _prep_weights(i, g_mix, w_in, b_gate, g_q, g_k, rel_bias, conv_ssm_w, conv_ssm_b, dt_bias, a_log, d_skip,
                  g_ssm, w_proj_a, w_proj_b, w_out, g_ffn, w_up, ffn_conv_w, ffn_conv_b, w_down, g_ple,
                  w_ple_gate, w_ple):
    row = lambda a: a[i].reshape(1, -1).astype(F32)
    pad_lanes = lambda a, n: jnp.pad(a, ((0, 0), (0, n - a.shape[1])))
    wi = w_in[i].astype(BF16)
    head_of_lane = jnp.arange(D_INNER) // SSM_HEAD_DIM
    expand1 = (jnp.arange(DT_PAD)[:, None] == head_of_lane[None, :]).astype(BF16)
    blk = jnp.arange(MXU_DIM) // HEAD_DIM_A
    return {
        "g_mix": row(g_mix), "w_in": wi, "w_gates": wi[:, _SEG_GATES[0]:_SEG_GATES[1]],
        "w_dt": pad_lanes(wi[:, _SEG_DT[0]:_SEG_DT[1]], DT_PAD), "b_gate": row(b_gate),
        "g_q": jnp.tile(row(g_q), (1, N_HEADS_A)), "g_k": jnp.tile(row(g_k), (1, N_HEADS_A)),
        "hsum": (blk[:, None] == blk[None, :]).astype(BF16),
        "rel_bias": rel_bias[i].astype(F32),
        "conv_ssm_w": jnp.pad(conv_ssm_w[i], ((0, SUBLANES - SSM_CONV), (0, 0))), "conv_ssm_b": row(conv_ssm_b),
        "dt_bias": pad_lanes(row(dt_bias), DT_PAD), "a_log": pad_lanes(row(a_log), DT_PAD),
        "d_skip": jnp.repeat(row(d_skip), SSM_HEAD_DIM, axis=1), "g_ssm": row(g_ssm),
        "expand": jnp.concatenate([expand1] * 3, axis=0),
        "w_proj_a": w_proj_a[i].astype(BF16), "w_proj_b": w_proj_b[i].astype(BF16), "w_out": w_out[i].astype(BF16),
        "g_ffn": row(g_ffn), "w_up": w_up[i].astype(BF16),
        "ffn_conv_w": jnp.pad(ffn_conv_w[i], ((0, SUBLANES - FFN_CONV), (0, 0))), "ffn_conv_b": row(ffn_conv_b),
        "w_down": w_down[i].astype(BF16), "g_ple": row(g_ple),
        "w_ple_gate": w_ple_gate[i].astype(BF16), "w_ple": w_ple[i].astype(BF16),
    }


def kernel(x_prompt, x_sample, cache_k, cache_v, state_ssm, state_conv_ssm, state_conv_ffn, p_prompt, p_sample, g_mix, w_in, b_gate, g_q, g_k, rel_bias, conv_ssm_w, conv_ssm_b, dt_bias, a_log, d_skip, g_ssm, w_proj_a, w_proj_b, w_out, g_ffn, w_up, ffn_conv_w, ffn_conv_b, w_down, g_ple, w_ple_gate, w_ple):
    depth = w_in.shape[0]
    bp = x_prompt.shape[0]
    bs, seq_s = x_sample.shape[:2]
    yp, ys = x_prompt, x_sample
    sp, ss = [], []
    for i in range(depth):
        w = _prep_weights(i, g_mix, w_in, b_gate, g_q, g_k, rel_bias, conv_ssm_w, conv_ssm_b, dt_bias, a_log,
                          d_skip, g_ssm, w_proj_a, w_proj_b, w_out, g_ffn, w_up, ffn_conv_w, ffn_conv_b, w_down,
                          g_ple, w_ple_gate, w_ple)
        yp, st_p = _layer(yp, p_prompt[i], None,
                          jnp.zeros((bp, SSM_CONV - 1, CONV_DIM), F32), None,
                          jnp.zeros((bp, FFN_CONV - 1, 2 * D_FF), F32), w, tm=256, tq=4 * CHUNK, tm_ffn=256)
        h0 = state_ssm[i].reshape(bs, D_INNER, D_STATE)
        ys, st_s = _layer(ys, p_sample[i], (cache_k[i], cache_v[i]), state_conv_ssm[i], h0,
                          state_conv_ffn[i], w, tm=seq_s * bs, tq=seq_s, tm_ffn=seq_s * bs)
        sp.append(st_p)
        ss.append(st_s)

    stack = lambda lst, j: jnp.stack([s[j] for s in lst], axis=0)
    return (yp, ys,
            stack(sp, 0), stack(sp, 1), stack(sp, 2), stack(sp, 3), stack(sp, 4),
            stack(ss, 0), stack(ss, 1), stack(ss, 2), stack(ss, 3), stack(ss, 4))
```

```python
import functools

import jax
import jax.numpy as jnp
from jax import lax
from jax.experimental import pallas as pl
from jax.experimental.pallas import tpu as pltpu

F32 = jnp.float32
BF16 = jnp.bfloat16

D_MODEL = 1024
CHUNK = 64
PLE_DIM = 256
EPS = 1e-6
NEG_INF = -1e30
N_HEADS_A = 16
HEAD_DIM_A = 64
ATT_WIDTH = N_HEADS_A * HEAD_DIM_A
ATT_PAST = 8 * CHUNK
REL_CLIP = 128
D_INNER = 2 * D_MODEL
SSM_HEAD_DIM = 64
N_HEADS_S = D_INNER // SSM_HEAD_DIM
N_GROUPS = 4
GROUP_WIDTH = D_INNER // N_GROUPS
D_STATE = 128
SSM_CONV = 4
CONV_DIM = D_INNER + 2 * N_GROUPS * D_STATE
D_FF = 3 * D_MODEL
FFN_CONV = 3

LANES = 128
SUBLANES = 8
MXU_DIM = 256
VMEM_LIMIT_BYTES = 56 * 1024 * 1024

DT_PAD = LANES
HEADS_PER_MXU_TILE = MXU_DIM // SSM_HEAD_DIM
FFN_COL_BLOCK = 1024

_SEG_Q = (0, ATT_WIDTH)
_SEG_K = (ATT_WIDTH, 2 * ATT_WIDTH)
_SEG_V = (2 * ATT_WIDTH, 3 * ATT_WIDTH)
_SEG_Z = (3 * ATT_WIDTH, 3 * ATT_WIDTH + D_INNER)
_SEG_XBC = (_SEG_Z[1], _SEG_Z[1] + CONV_DIM)
_SEG_DT = (_SEG_XBC[1], _SEG_XBC[1] + N_HEADS_S)
_SEG_GATES = (_SEG_DT[1], _SEG_DT[1] + 2 * D_MODEL)


def _resident(shape):
    nd = len(shape)
    return pl.BlockSpec(shape, lambda *_: (0,) * nd, pipeline_mode=pl.Buffered(1))


def _params(semantics):
    return pltpu.CompilerParams(dimension_semantics=semantics, vmem_limit_bytes=VMEM_LIMIT_BYTES)


def _rms_scale(x):
    return lax.rsqrt(jnp.mean(x * x, axis=-1, keepdims=True) + EPS)


def _conv_heads(ctx, width):
    heads = [jnp.pad(ctx[:, width - 1 - k:], ((0, 0), (0, SUBLANES - k), (0, 0))) for k in range(1, width)]
    return jnp.concatenate(heads, axis=1)


def _softplus(x):
    return jnp.maximum(x, 0.0) + jnp.log(1.0 + jnp.exp(-jnp.abs(x)))


def _split3(v):
    hi = v.astype(BF16)
    r1 = v - hi.astype(F32)
    mid = r1.astype(BF16)
    lo = (r1 - mid.astype(F32)).astype(BF16)
    return jnp.concatenate([hi, mid, lo], axis=-1)


def _ssd_load_carry(ctx_ref, h0_ref, state, heads):
    for k in range(SSM_CONV - 1):
        heads[k] = ctx_ref[0, k * SUBLANES:(k + 1) * SUBLANES, :]
    for g in range(N_GROUPS):
        if h0_ref is None:
            state[g] = jnp.zeros((D_STATE, GROUP_WIDTH), F32)
        else:
            state[g] = h0_ref[0, g * GROUP_WIDTH:(g + 1) * GROUP_WIDTH, :].T


def _ssd_emit_carry(st_out, ctx_out, state, heads):
    ctx_out[0] = heads[SSM_CONV - 2]
    for g in range(N_GROUPS):
        st_out[0, g * GROUP_WIDTH:(g + 1) * GROUP_WIDTH, :] = state[g].T


def _ssd_conv(x_raw, prm, heads, valid_rows):
    cw_ref, cb_ref = prm[:2]
    n_rows = x_raw.shape[0]
    row_in_slab = lax.broadcasted_iota(jnp.int32, (n_rows, CONV_DIM), 0) % SUBLANES
    acc = cb_ref[...] + x_raw * cw_ref[SSM_CONV - 1:SSM_CONV, :]
    for k in range(1, SSM_CONV):
        rot = pltpu.roll(x_raw.reshape(n_rows // SUBLANES, SUBLANES, CONV_DIM), k, 1).reshape(n_rows, CONV_DIM)
        above = jnp.concatenate([heads[k - 1], rot[:n_rows - SUBLANES]], axis=0)
        acc = acc + jnp.where(row_in_slab < k, above, rot) * cw_ref[SSM_CONV - 1 - k:SSM_CONV - k, :]
        heads[k - 1] = rot[valid_rows - SUBLANES:valid_rows]
    return acc * jax.nn.sigmoid(acc)


def _ssd_steps(dt_raw, prm, valid_rows):
    dtb_ref, alog_ref, expand_ref = prm[2], prm[3], prm[6]
    n_rows = dt_raw.shape[0]
    dt = _softplus(dt_raw + dtb_ref[...])
    if valid_rows < n_rows:
        rows = lax.broadcasted_iota(jnp.int32, (n_rows, DT_PAD), 0)
        dt = jnp.where(rows < valid_rows, dt, 0.0)
    ac = dt * (-jnp.exp(alog_ref[...]))
    ri = lax.broadcasted_iota(jnp.int32, (n_rows, n_rows), 0)
    ci = lax.broadcasted_iota(jnp.int32, (n_rows, n_rows), 1)
    same_chunk_causal = jnp.logical_and(ri // CHUNK == ci // CHUNK, ri >= ci)
    acs = jnp.dot(same_chunk_causal.astype(F32), ac, preferred_element_type=F32,
                  precision=lax.Precision.HIGHEST)
    expand = lambda v: jnp.dot(_split3(v), expand_ref[...], preferred_element_type=F32)
    return expand(acs), expand(dt)


def _ssd_chunk(xc, acol, dtcol, z, prm, state):
    dskip_ref, gssm_ref = prm[4], prm[5]
    q = CHUNK

    xs = xc[:, :D_INNER]
    bm = xc[:, D_INNER:D_INNER + N_GROUPS * D_STATE]
    cm = xc[:, D_INNER + N_GROUPS * D_STATE:].astype(BF16)

    last = acol[q - 1:q, :]
    dx = dtcol * xs
    wj = (jnp.exp(last - acol) * dx).astype(BF16)
    ea = jnp.exp(acol)

    lane = lax.broadcasted_iota(jnp.int32, (q, D_INNER), 1)
    rowi = lax.broadcasted_iota(jnp.int32, (q, D_INNER), 0)
    arow = jnp.sum(jnp.where(lane % q == rowi, acol, 0.0), axis=0, keepdims=True)

    tri = (lax.broadcasted_iota(jnp.int32, (q, MXU_DIM), 0)
           >= lax.broadcasted_iota(jnp.int32, (q, MXU_DIM), 1) % q)
    br = lax.broadcasted_iota(jnp.int32, (MXU_DIM, MXU_DIM), 0) // SSM_HEAD_DIM
    bc = lax.broadcasted_iota(jnp.int32, (MXU_DIM, MXU_DIM), 1) // SSM_HEAD_DIM
    blockdiag = br == bc

    y_blocks = []
    for g in range(N_GROUPS):
        cg = cm[:, g * D_STATE:(g + 1) * D_STATE]
        bg = bm[:, g * D_STATE:(g + 1) * D_STATE].astype(BF16)
        bg4 = jnp.concatenate([bg] * HEADS_PER_MXU_TILE, axis=0)
        cb4 = lax.dot_general(cg, bg4, (((1,), (1,)), ((), ())), preferred_element_type=F32)
        sg = state[g].astype(BF16)
        for half in range(GROUP_WIDTH // MXU_DIM):
            lo = g * GROUP_WIDTH + half * MXU_DIM
            cols = slice(lo, lo + MXU_DIM)
            seg = acol[:, cols] - arow[:, cols]
            decay = jnp.where(tri, jnp.exp(jnp.where(tri, seg, 0.0)), 0.0)
            mp = (cb4 * decay).astype(BF16)
            dxb = dx[:, cols].astype(BF16)
            bd = jnp.where(blockdiag, jnp.concatenate([dxb] * HEADS_PER_MXU_TILE, axis=0),
                           jnp.zeros((), BF16))
            y = jnp.dot(mp, bd, preferred_element_type=F32)
            y = y + jnp.dot(cg, sg[:, half * MXU_DIM:(half + 1) * MXU_DIM],
                            preferred_element_type=F32) * ea[:, cols]
            y_blocks.append(y + dskip_ref[:, cols] * xs[:, cols])

    el = jnp.exp(last)
    for g in range(N_GROUPS):
        cols = slice(g * GROUP_WIDTH, (g + 1) * GROUP_WIDTH)
        bgt = bm[:, g * D_STATE:(g + 1) * D_STATE].T.astype(BF16)
        state[g] = state[g] * el[:, cols] + jnp.dot(bgt, wj[:, cols], preferred_element_type=F32)

    zf = z.astype(F32)
    yg = jnp.concatenate(y_blocks, axis=-1) * (zf * jax.nn.sigmoid(zf))
    return (yg * _rms_scale(yg) * gssm_ref[...]).astype(BF16)


def _ssd_scratch():
    return [pltpu.VMEM((N_GROUPS, D_STATE, GROUP_WIDTH), F32),
            pltpu.VMEM((SSM_CONV - 1, SUBLANES, CONV_DIM), F32)]


def _ssd_kernel(*refs, has_state, valid_rows, n_chunks):
    if has_state:
        xbc_ref, dt_ref, z_ref, ctx_ref, h0_ref = refs[:5]
        refs = refs[5:]
    else:
        xbc_ref, dt_ref, z_ref, ctx_ref = refs[:4]
        h0_ref = None
        refs = refs[4:]
    prm, (yb_out, st_out, ctx_out, state, heads) = refs[:7], refs[7:]
    c = pl.program_id(1)

    @pl.when(c == 0)
    def _load_carry():
        _ssd_load_carry(ctx_ref, h0_ref, state, heads)

    xc = _ssd_conv(xbc_ref[0], prm, heads, valid_rows)
    acol, dtcol = _ssd_steps(dt_ref[0], prm, valid_rows)
    yb_out[0] = _ssd_chunk(xc, acol, dtcol, z_ref[0], prm, state)

    @pl.when(c == n_chunks - 1)
    def _emit_carry():
        _ssd_emit_carry(st_out, ctx_out, state, heads)


def _ssd(xbc, dt_raw, z, ctx, h0, prm, valid_rows):
    b, seq, _ = xbc.shape
    n_chunks = seq // CHUNK
    has_state = h0 is not None
    chunk = lambda w: pl.BlockSpec((1, CHUNK, w), lambda bi, ci: (bi, ci, 0))
    per_batch = lambda r, w: pl.BlockSpec((1, r, w), lambda bi, ci: (bi, 0, 0))
    in_specs = [chunk(CONV_DIM), chunk(DT_PAD), chunk(D_INNER), per_batch((SSM_CONV - 1) * SUBLANES, CONV_DIM)]
    args = [xbc, dt_raw, z, _conv_heads(ctx, SSM_CONV)]
    if has_state:
        in_specs.append(per_batch(D_INNER, D_STATE))
        args.append(h0)
    in_specs += [_resident(p.shape) for p in prm]
    args += list(prm)
    return pl.pallas_call(
        functools.partial(_ssd_kernel, has_state=has_state, valid_rows=valid_rows, n_chunks=n_chunks),
        grid=(b, n_chunks),
        in_specs=in_specs,
        out_specs=[chunk(D_INNER), per_batch(D_INNER, D_STATE), per_batch(SUBLANES, CONV_DIM)],
        out_shape=[jax.ShapeDtypeStruct((b, seq, D_INNER), BF16),
                   jax.ShapeDtypeStruct((b, D_INNER, D_STATE), F32),
                   jax.ShapeDtypeStruct((b, SUBLANES, CONV_DIM), F32)],
        scratch_shapes=_ssd_scratch(),
        compiler_params=_params(("parallel", "arbitrary")),
        name="ssd",
    )(*args)


def _inproj_kernel(*refs, tiles_per_batch, tail_tiles, fuse_ssd, tm):
    x_ref, g_ref, w_ref, wgates_ref, wdt_ref, gq_ref, gk_ref, hsum_ref = refs[:8]
    if fuse_ssd:
        ctx_ref, prm = refs[8], refs[9:16]
        q_out, k_out, v_out, gates_out, ktail_out, vtail_out, yb_out, st_out, ctx_out, state, heads = refs[16:]
    else:
        q_out, k_out, v_out, gates_out, ktail_out, vtail_out, z_out, xbc_out, dt_out = refs[8:]
    tile_in_batch = pl.program_id(0) % tiles_per_batch

    if fuse_ssd:
        @pl.when(tile_in_batch == 0)
        def _load_carry():
            _ssd_load_carry(ctx_ref, None, state, heads)

    x = x_ref[...]
    h = (x * _rms_scale(x) * g_ref[...]).astype(BF16)

    def seg(bounds):
        return jnp.dot(h, w_ref[:, bounds[0]:bounds[1]], preferred_element_type=F32)

    def head_norm(t, g):
        sq = (t * t).astype(BF16)
        parts = []
        for blk in range(ATT_WIDTH // MXU_DIM):
            cols = slice(blk * MXU_DIM, (blk + 1) * MXU_DIM)
            ss = jnp.dot(sq[:, cols], hsum_ref[...], preferred_element_type=F32)
            parts.append(t[:, cols] * lax.rsqrt(ss * (1.0 / HEAD_DIM_A) + EPS))
        return jnp.concatenate(parts, axis=-1) * g

    xbc = seg(_SEG_XBC)
    zb = seg(_SEG_Z).astype(BF16)
    dt_raw = jnp.dot(h, wdt_ref[...], preferred_element_type=F32)
    if fuse_ssd:
        xc = _ssd_conv(xbc, prm, heads, tm)
        acol, dtcol = _ssd_steps(dt_raw, prm, tm)
        for c in range(tm // CHUNK):
            rows = slice(c * CHUNK, (c + 1) * CHUNK)
            yb_out[rows, :] = _ssd_chunk(xc[rows, :], acol[rows, :], dtcol[rows, :], zb[rows, :], prm, state)
    else:
        xbc_out[...] = xbc
        z_out[...] = zb
        dt_out[...] = dt_raw

    q_out[...] = (head_norm(seg(_SEG_Q), gq_ref[...]) * (HEAD_DIM_A ** -0.5)).astype(BF16)
    k = head_norm(seg(_SEG_K), gk_ref[...])
    v = seg(_SEG_V)
    k_out[...] = k.astype(BF16)
    v_out[...] = v.astype(BF16)
    gates_out[...] = jnp.dot(h, wgates_ref[...], preferred_element_type=F32).astype(BF16)

    @pl.when(tile_in_batch >= tiles_per_batch - tail_tiles)
    def _emit_tail():
        ktail_out[...] = k
        vtail_out[...] = v

    if fuse_ssd:
        @pl.when(tile_in_batch == tiles_per_batch - 1)
        def _emit_carry():
            _ssd_emit_carry(st_out, ctx_out, state, heads)


def _inproj(x2d, w, tm, seq, ssm_ctx=None):
    t = x2d.shape[0]
    fuse_ssd = ssm_ctx is not None
    tiles_per_batch = max(seq // tm, 1)
    tail_tiles = max(min(ATT_PAST, seq) // tm, 1)
    n_tiles = t // tm
    n_batches = n_tiles // tiles_per_batch
    row = lambda width: pl.BlockSpec((tm, width), lambda i: (i, 0))
    first_tail = tiles_per_batch - tail_tiles
    tail = pl.BlockSpec((tm, ATT_WIDTH), lambda i: (
        (i // tiles_per_batch) * tail_tiles + jnp.maximum(i % tiles_per_batch - first_tail, 0), 0))
    per_batch = lambda r, width: pl.BlockSpec((1, r, width), lambda i: (i // tiles_per_batch, 0, 0))
    rows_out = lambda width, dtype: (row(width), jax.ShapeDtypeStruct((t, width), dtype))
    tail_out = (tail, jax.ShapeDtypeStruct((n_batches * tail_tiles * tm, ATT_WIDTH), F32))

    in_specs = [row(D_MODEL), _resident((1, D_MODEL)), _resident((D_MODEL, _SEG_XBC[1])),
                _resident(w["w_gates"].shape), _resident(w["w_dt"].shape),
                _resident((1, ATT_WIDTH)), _resident((1, ATT_WIDTH)), _resident((MXU_DIM, MXU_DIM))]
    args = [x2d, w["g_mix"], w["w_in"], w["w_gates"], w["w_dt"], w["g_q"], w["g_k"], w["hsum"]]
    outs = [rows_out(ATT_WIDTH, BF16), rows_out(ATT_WIDTH, BF16), rows_out(ATT_WIDTH, BF16),
            rows_out(2 * D_MODEL, BF16), tail_out, tail_out]
    scratch = []
    if fuse_ssd:
        assert seq % tm == 0 and tm % CHUNK == 0
        prm = _ssd_params(w)
        in_specs += [per_batch((SSM_CONV - 1) * SUBLANES, CONV_DIM)] + [_resident(p.shape) for p in prm]
        args += [_conv_heads(ssm_ctx, SSM_CONV)] + list(prm)
        outs += [rows_out(D_INNER, BF16),
                 (per_batch(D_INNER, D_STATE), jax.ShapeDtypeStruct((n_batches, D_INNER, D_STATE), F32)),
                 (per_batch(SUBLANES, CONV_DIM), jax.ShapeDtypeStruct((n_batches, SUBLANES, CONV_DIM), F32))]
        scratch = _ssd_scratch()
    else:
        outs += [rows_out(D_INNER, BF16), rows_out(CONV_DIM, F32), rows_out(DT_PAD, F32)]
    return pl.pallas_call(
        functools.partial(_inproj_kernel, tiles_per_batch=tiles_per_batch, tail_tiles=tail_tiles,
                          fuse_ssd=fuse_ssd, tm=tm),
        grid=(n_tiles,),
        in_specs=in_specs,
        out_specs=[o[0] for o in outs],
        out_shape=[o[1] for o in outs],
        scratch_shapes=scratch,
        compiler_params=_params(("arbitrary",)),
        name="inproj_ssd" if fuse_ssd else "inproj",
    )(*args)


HEADS_PER_LANE_TILE = LANES // HEAD_DIM_A


def _attn_kernel(*refs, has_cache, seq, qb, width):
    if has_cache:
        q_ref, k_ref, v_ref, bias_ref, ck_ref, cv_ref, o_ref, kp, vp = refs
    else:
        q_ref, k_ref, v_ref, bias_ref, o_ref, kp, vp = refs
    nk = ATT_PAST + qb

    if has_cache:
        kp[0:ATT_PAST, :] = ck_ref[0].astype(BF16)
        vp[0:ATT_PAST, :] = cv_ref[0].astype(BF16)
    kp[ATT_PAST:ATT_PAST + seq, :] = k_ref[0]
    vp[ATT_PAST:ATT_PAST + seq, :] = v_ref[0]

    lane = lax.broadcasted_iota(jnp.int32, (qb, LANES), 1)
    low_half = lane < HEAD_DIM_A

    def q_block(start, first_col):
        kb = kp[pl.ds(start + first_col, nk - first_col), :]
        vb = vp[pl.ds(start + first_col, nk - first_col), :]
        q = q_ref[0, pl.ds(start, qb), :]
        outs = []
        for pair in range(width // LANES):
            cols = slice(pair * LANES, (pair + 1) * LANES)
            qp, kpair, vpair = q[:, cols], kb[:, cols], vb[:, cols]
            zero = jnp.zeros((), BF16)
            q2 = jnp.concatenate([jnp.where(low_half, qp, zero), jnp.where(low_half, zero, qp)], axis=0)
            s = lax.dot_general(q2, kpair, (((1,), (1,)), ((), ())), preferred_element_type=F32)
            first_head = pair * HEADS_PER_LANE_TILE
            bias = bias_ref[first_head:first_head + HEADS_PER_LANE_TILE, :, first_col:]
            s = s + bias.reshape(HEADS_PER_LANE_TILE * qb, nk - first_col)
            m = jnp.max(s, axis=-1, keepdims=True)
            p = jnp.exp(s - m)
            denom = jnp.sum(p, axis=-1, keepdims=True)
            o2 = jnp.dot(p.astype(BF16), vpair, preferred_element_type=F32) / denom
            outs.append(jnp.where(low_half, o2[:qb], o2[qb:]))
        o_ref[0, pl.ds(start, qb), :] = jnp.concatenate(outs, axis=-1).astype(BF16)

    n_blocks = seq // qb
    n_peeled = 0 if has_cache else min(ATT_PAST // qb, n_blocks)
    for i in range(n_peeled):
        q_block(i * qb, ATT_PAST - i * qb)

    n_steady = n_blocks - n_peeled
    unroll = 2 if n_steady % 2 == 0 and n_steady > 0 else 1

    def steady(i, carry):
        for u in range(unroll):
            q_block(pl.multiple_of((n_peeled + i * unroll + u) * qb, qb), 0)
        return carry

    lax.fori_loop(0, n_steady // unroll, steady, 0)


def _attention(q, k, v, bias, cache, qb):
    b, seq, _ = q.shape
    nk = ATT_PAST + qb
    has_cache = cache is not None
    width = MXU_DIM if seq > ATT_PAST else ATT_WIDTH
    hb = ATT_WIDTH // width
    seq_spec = pl.BlockSpec((1, seq, width), lambda bi, hi: (bi, 0, hi))
    in_specs = [seq_spec, seq_spec, seq_spec,
                pl.BlockSpec((width // HEAD_DIM_A, qb, nk), lambda bi, hi: (hi, 0, 0))]
    args = [q, k, v, bias]
    if has_cache:
        cspec = pl.BlockSpec((1, ATT_PAST, width), lambda bi, hi: (bi, 0, hi))
        in_specs += [cspec, cspec]
        args += list(cache)
    return pl.pallas_call(
        functools.partial(_attn_kernel, has_cache=has_cache, seq=seq, qb=qb, width=width),
        grid=(b, hb),
        in_specs=in_specs,
        out_specs=seq_spec,
        out_shape=jax.ShapeDtypeStruct((b, seq, ATT_WIDTH), BF16),
        scratch_shapes=[pltpu.VMEM((ATT_PAST + seq, width), BF16),
                        pltpu.VMEM((ATT_PAST + seq, width), BF16)],
        compiler_params=_params(("parallel", "parallel")),
        name="band_attention",
    )(*args)


def _attention_bias(table, qb):
    nk = ATT_PAST + qb
    n_diag = qb + nk - 1
    diag_rel = (ATT_PAST + qb - 1) - jnp.arange(n_diag)
    r = table[:, jnp.clip(diag_rel, -REL_CLIP, REL_CLIP) + REL_CLIP]
    h = table.shape[0]
    row_len = -(-(n_diag + 1) // LANES) * LANES
    rp = jnp.pad(r, ((0, 0), (1, row_len - n_diag)))
    skew = jnp.tile(rp, (1, qb))[:, :qb * row_len].reshape(h, qb, row_len)
    bias = skew[:, :, qb:qb + nk]
    qc = jnp.arange(qb)[:, None] // CHUNK
    kc = jnp.arange(nk)[None, :] // CHUNK
    visible = (kc >= qc) & (kc <= qc + ATT_PAST // CHUNK)
    return jnp.where(visible[None], bias, NEG_INF).astype(F32)


def _ffn_kernel(x_ref, ya_ref, yb_ref, gates_ref, pe_ref, ctx_ref, bg_ref, wa_ref, wb_ref, wo_ref,
                gffn_ref, wup_ref, cw_ref, cb_ref, wdown_ref, gple_ref, wgate_ref, wple_ref,
                o_ref, ctx_out, heads, *, tm, seq_rows, tiles_per_batch):
    i = pl.program_id(0)
    n_seqs = tm // seq_rows

    @pl.when(i % tiles_per_batch == 0)
    def _load_context():
        for k in range(FFN_CONV - 1):
            heads[k] = ctx_ref[:, k * SUBLANES:(k + 1) * SUBLANES, :]

    mix_gate = jax.nn.sigmoid(gates_ref[...].astype(F32) + bg_ref[...])
    mixed = (mix_gate[:, :D_MODEL] * jnp.dot(ya_ref[...], wa_ref[...], preferred_element_type=F32)
             + mix_gate[:, D_MODEL:] * jnp.dot(yb_ref[...], wb_ref[...], preferred_element_type=F32))
    x = x_ref[...] + jnp.dot(mixed.astype(BF16), wo_ref[...], preferred_element_type=F32)
    xn = (x * _rms_scale(x) * gffn_ref[...]).astype(BF16)
    n_slabs = tm // SUBLANES
    row_in_slab = lax.broadcasted_iota(jnp.int32, (tm, FFN_COL_BLOCK), 0) % SUBLANES

    def up(j, part):
        lo = part * D_FF + j * FFN_COL_BLOCK
        return jnp.dot(xn, wup_ref[:, lo:lo + FFN_COL_BLOCK], preferred_element_type=F32)

    def conv(u, j, part):
        lo = part * D_FF + j * FFN_COL_BLOCK
        cols = slice(lo, lo + FFN_COL_BLOCK)
        out = cb_ref[:, cols] + u * cw_ref[FFN_CONV - 1:FFN_CONV, cols]
        for k in range(1, FFN_CONV):
            rot = pltpu.roll(u.reshape(n_slabs, SUBLANES, FFN_COL_BLOCK), k, 1).reshape(tm, FFN_COL_BLOCK)
            pieces = []
            for s in range(n_seqs):
                pieces += [heads[k - 1, s, :, cols], rot[s * seq_rows:(s + 1) * seq_rows - SUBLANES]]
            shifted = jnp.where(row_in_slab < k, jnp.concatenate(pieces, axis=0), rot)
            out = out + shifted * cw_ref[FFN_CONV - 1 - k:FFN_CONV - k, cols]
            for s in range(n_seqs):
                heads[k - 1, s, :, cols] = rot[(s + 1) * seq_rows - SUBLANES:(s + 1) * seq_rows]
        return out

    n_blocks = D_FF // FFN_COL_BLOCK
    acc = jnp.zeros((tm, D_MODEL), F32)
    pending = (up(0, 0), up(0, 1))
    for j in range(n_blocks):
        ug, uv = pending
        if j + 1 < n_blocks:
            pending = (up(j + 1, 0), up(j + 1, 1))
        hidden = (jax.nn.gelu(conv(ug, j, 0), approximate=True) * conv(uv, j, 1)).astype(BF16)
        acc = acc + jnp.dot(hidden, wdown_ref[j * FFN_COL_BLOCK:(j + 1) * FFN_COL_BLOCK, :],
                            preferred_element_type=F32)

    @pl.when(i % tiles_per_batch == tiles_per_batch - 1)
    def _emit_context():
        ctx_out[...] = heads[FFN_CONV - 2]

    x2 = x + acc
    gate = jax.nn.sigmoid(jnp.dot((x2 * _rms_scale(x2) * gple_ref[...]).astype(BF16), wgate_ref[...],
                                  preferred_element_type=F32))
    o_ref[...] = x2 + gate * jnp.dot(pe_ref[...].astype(BF16), wple_ref[...], preferred_element_type=F32)


def _merge_ffn(x2d, ya, yb, gates, pe2d, ctx, prm, seq, tm):
    t = x2d.shape[0]
    batch = t // seq
    seq_rows = min(seq, tm)
    seqs_per_tile = tm // seq_rows
    tiles_per_batch = seq // seq_rows
    row = lambda w: pl.BlockSpec((tm, w), lambda i: (i, 0))
    per_seq = lambda r: pl.BlockSpec((seqs_per_tile, r, 2 * D_FF), lambda i: (i // tiles_per_batch, 0, 0))
    y, ctx_out = pl.pallas_call(
        functools.partial(_ffn_kernel, tm=tm, seq_rows=seq_rows, tiles_per_batch=tiles_per_batch),
        grid=(t // tm,),
        in_specs=[row(D_MODEL), row(ATT_WIDTH), row(D_INNER), row(2 * D_MODEL), row(PLE_DIM),
                  per_seq((FFN_CONV - 1) * SUBLANES)] + [_resident(p.shape) for p in prm],
        out_specs=[row(D_MODEL), per_seq(SUBLANES)],
        out_shape=[jax.ShapeDtypeStruct((t, D_MODEL), F32),
                   jax.ShapeDtypeStruct((batch, SUBLANES, 2 * D_FF), F32)],
        scratch_shapes=[pltpu.VMEM((FFN_CONV - 1, seqs_per_tile, SUBLANES, 2 * D_FF), F32)],
        compiler_params=_params(("arbitrary",)),
        name="merge_convffn_ple",
    )(x2d, ya, yb, gates, pe2d, _conv_heads(ctx, FFN_CONV), *prm)
    return y, ctx_out[:, :FFN_CONV - 1]


def _ssd_params(w):
    return (w["conv_ssm_w"], w["conv_ssm_b"], w["dt_bias"], w["a_log"], w["d_skip"], w["g_ssm"], w["expand"])


def _layer(x, pe, kv_cache, ssm_ctx, h0, ffn_ctx, w, tm, tq, tm_ffn):
    b, seq, _ = x.shape
    t = b * seq
    x2d = x.reshape(t, D_MODEL)
    as3 = lambda a: a.reshape(b, seq, -1)
    fuse_ssd = h0 is None and seq % tm == 0 and tm % CHUNK == 0
    if fuse_ssd:
        qn, kn, v, gates, k_tail, v_tail, yb, st, ctx_ssm = _inproj(x2d, w, tm, seq, ssm_ctx)
    else:
        qn, kn, v, gates, k_tail, v_tail, z, xbc, dt_raw = _inproj(x2d, w, tm, seq)
        seq_pad = -(-seq // CHUNK) * CHUNK
        valid_rows = CHUNK - (seq_pad - seq)
        pad = lambda a: jnp.pad(as3(a), ((0, 0), (0, seq_pad - seq), (0, 0)))
        yb, st, ctx_ssm = _ssd(pad(xbc), pad(dt_raw), pad(z), ssm_ctx, h0, _ssd_params(w), valid_rows)
        yb = yb[:, :seq].reshape(t, D_INNER)

    cache = None
    if kv_cache is not None:
        cache = tuple(ck.reshape(b, ATT_PAST, ATT_WIDTH) for ck in kv_cache)
    ya = _attention(as3(qn), as3(kn), as3(v), _attention_bias(w["rel_bias"], tq), cache, tq)

    ffn_prm = (w["b_gate"], w["w_proj_a"], w["w_proj_b"], w["w_out"],
               w["g_ffn"], w["w_up"], w["ffn_conv_w"], w["ffn_conv_b"], w["w_down"],
               w["g_ple"], w["w_ple_gate"], w["w_ple"])
    y, ctx_ffn = _merge_ffn(x2d, ya.reshape(t, ATT_WIDTH), yb, gates, pe.reshape(t, PLE_DIM), ffn_ctx, ffn_prm,
                            seq, tm_ffn)

    heads = lambda a: a.reshape(b, min(seq, ATT_PAST), N_HEADS_A, HEAD_DIM_A)
    state = st.reshape(b, N_HEADS_S, SSM_HEAD_DIM, D_STATE)
    return y.reshape(b, seq, D_MODEL), (heads(k_tail), heads(v_tail), state, ctx_ssm[:, :SSM_CONV - 1], ctx_ffn)


def _prep_weights(i, g_mix, w_in, b_gate, g_q, g_k, rel_bias, conv_ssm_w, conv_ssm_b, dt_bias, a_log, d_skip,
                  g_ssm, w_proj_a, w_proj_b, w_out, g_ffn, w_up, ffn_conv_w, ffn_conv_b, w_down, g_ple,
                  w_ple_gate, w_ple):
    row = lambda a: a[i].reshape(1, -1).astype(F32)
    pad_lanes = lambda a, n: jnp.pad(a, ((0, 0), (0, n - a.shape[1])))
    wi = w_in[i].astype(BF16)
    head_of_lane = jnp.arange(D_INNER) // SSM_HEAD_DIM
    expand1 = (jnp.arange(DT_PAD)[:, None] == head_of_lane[None, :]).astype(BF16)
    blk = jnp.arange(MXU_DIM) // HEAD_DIM_A
    return {
        "g_mix": row(g_mix), "w_in": wi, "w_gates": wi[:, _SEG_GATES[0]:_SEG_GATES[1]],
        "w_dt": pad_lanes(wi[:, _SEG_DT[0]:_SEG_DT[1]], DT_PAD), "b_gate": row(b_gate),
        "g_q": jnp.tile(row(g_q), (1, N_HEADS_A)), "g_k": jnp.tile(row(g_k), (1, N_HEADS_A)),
        "hsum": (blk[:, None] == blk[None, :]).astype(BF16),
        "rel_bias": rel_bias[i].astype(F32),
        "conv_ssm_w": jnp.pad(conv_ssm_w[i], ((0, SUBLANES - SSM_CONV), (0, 0))), "conv_ssm_b": row(conv_ssm_b),
        "dt_bias": pad_lanes(row(dt_bias), DT_PAD), "a_log": pad_lanes(row(a_log), DT_PAD),
        "d_skip": jnp.repeat(row(d_skip), SSM_HEAD_DIM, axis=1), "g_ssm": row(g_ssm),
        "expand": jnp.concatenate([expand1] * 3, axis=0),
        "w_proj_a": w_proj_a[i].astype(BF16), "w_proj_b": w_proj_b[i].astype(BF16), "w_out": w_out[i].astype(BF16),
        "g_ffn": row(g_ffn), "w_up": w_up[i].astype(BF16),
        "ffn_conv_w": jnp.pad(ffn_conv_w[i], ((0, SUBLANES - FFN_CONV), (0, 0))), "ffn_conv_b": row(ffn_conv_b),
        "w_down": w_down[i].astype(BF16), "g_ple": row(g_ple),
        "w_ple_gate": w_ple_gate[i].astype(BF16), "w_ple": w_ple[i].astype(BF16),
    }


def kernel(x_prompt, x_sample, cache_k, cache_v, state_ssm, state_conv_ssm, state_conv_ffn, p_prompt, p_sample, g_mix, w_in, b_gate, g_q, g_k, rel_bias, conv_ssm_w, conv_ssm_b, dt_bias, a_log, d_skip, g_ssm, w_proj_a, w_proj_b, w_out, g_ffn, w_up, ffn_conv_w, ffn_conv_b, w_down, g_ple, w_ple_gate, w_ple):
    depth = w_in.shape[0]
    bp = x_prompt.shape[0]
    bs, seq_s = x_sample.shape[:2]
    yp, ys = x_prompt, x_sample
    sp, ss = [], []
    for i in range(depth):
        w = _prep_weights(i, g_mix, w_in, b_gate, g_q, g_k, rel_bias, conv_ssm_w, conv_ssm_b, dt_bias, a_log,
                          d_skip, g_ssm, w_proj_a, w_proj_b, w_out, g_ffn, w_up, ffn_conv_w, ffn_conv_b, w_down,
                          g_ple, w_ple_gate, w_ple)
        yp, st_p = _layer(yp, p_prompt[i], None,
                          jnp.zeros((bp, SSM_CONV - 1, CONV_DIM), F32), None,
                          jnp.zeros((bp, FFN_CONV - 1, 2 * D_FF), F32), w, tm=256, tq=4 * CHUNK, tm_ffn=256)
        h0 = state_ssm[i].reshape(bs, D_INNER, D_STATE)
        ys, st_s = _layer(ys, p_sample[i], (cache_k[i], cache_v[i]), state_conv_ssm[i], h0,
                          state_conv_ffn[i], w, tm=seq_s * bs, tq=seq_s, tm_ffn=seq_s * bs)
        sp.append(st_p)
        ss.append(st_s)

    stack = lambda lst, j: jnp.stack([s[j] for s in lst], axis=0)
    return (yp, ys,
            stack(sp, 0), stack(sp, 1), stack(sp, 2), stack(sp, 3), stack(sp, 4),
            stack(ss, 0), stack(ss, 1), stack(ss, 2), stack(ss, 3), stack(ss, 4))
```

```python
import functools

import jax
import jax.numpy as jnp
from jax import lax
from jax.experimental import pallas as pl
from jax.experimental.pallas import tpu as pltpu

F32 = jnp.float32
BF16 = jnp.bfloat16

D_MODEL = 1024
CHUNK = 64
PLE_DIM = 256
EPS = 1e-6
NEG_INF = -1e30
N_HEADS_A = 16
HEAD_DIM_A = 64
ATT_WIDTH = N_HEADS_A * HEAD_DIM_A
ATT_PAST = 8 * CHUNK
REL_CLIP = 128
D_INNER = 2 * D_MODEL
SSM_HEAD_DIM = 64
N_HEADS_S = D_INNER // SSM_HEAD_DIM
N_GROUPS = 4
GROUP_WIDTH = D_INNER // N_GROUPS
D_STATE = 128
SSM_CONV = 4
CONV_DIM = D_INNER + 2 * N_GROUPS * D_STATE
D_FF = 3 * D_MODEL
FFN_CONV = 3

LANES = 128
SUBLANES = 8
MXU_DIM = 256
VMEM_LIMIT_BYTES = 56 * 1024 * 1024

DT_PAD = LANES
HEADS_PER_MXU_TILE = MXU_DIM // SSM_HEAD_DIM
FFN_COL_BLOCK = 1024

_SEG_Q = (0, ATT_WIDTH)
_SEG_K = (ATT_WIDTH, 2 * ATT_WIDTH)
_SEG_V = (2 * ATT_WIDTH, 3 * ATT_WIDTH)
_SEG_Z = (3 * ATT_WIDTH, 3 * ATT_WIDTH + D_INNER)
_SEG_XBC = (_SEG_Z[1], _SEG_Z[1] + CONV_DIM)
_SEG_DT = (_SEG_XBC[1], _SEG_XBC[1] + N_HEADS_S)
_SEG_GATES = (_SEG_DT[1], _SEG_DT[1] + 2 * D_MODEL)


def _resident(shape):
    nd = len(shape)
    return pl.BlockSpec(shape, lambda *_: (0,) * nd, pipeline_mode=pl.Buffered(1))


def _params(semantics):
    return pltpu.CompilerParams(dimension_semantics=semantics, vmem_limit_bytes=VMEM_LIMIT_BYTES)


def _rms_scale(x):
    return lax.rsqrt(jnp.mean(x * x, axis=-1, keepdims=True) + EPS)


def _conv_heads(ctx, width):
    heads = [jnp.pad(ctx[:, width - 1 - k:], ((0, 0), (0, SUBLANES - k), (0, 0))) for k in range(1, width)]
    return jnp.concatenate(heads, axis=1)


def _softplus(x):
    return jnp.maximum(x, 0.0) + jnp.log(1.0 + jnp.exp(-jnp.abs(x)))


def _split3(v):
    hi = v.astype(BF16)
    r1 = v - hi.astype(F32)
    mid = r1.astype(BF16)
    lo = (r1 - mid.astype(F32)).astype(BF16)
    return jnp.concatenate([hi, mid, lo], axis=-1)


def _ssd_load_carry(ctx_ref, h0_ref, state, heads):
    for k in range(SSM_CONV - 1):
        heads[k] = ctx_ref[0, k * SUBLANES:(k + 1) * SUBLANES, :]
    for g in range(N_GROUPS):
        if h0_ref is None:
            state[g] = jnp.zeros((D_STATE, GROUP_WIDTH), F32)
        else:
            state[g] = h0_ref[0, g * GROUP_WIDTH:(g + 1) * GROUP_WIDTH, :].T


def _ssd_emit_carry(st_out, ctx_out, state, heads):
    ctx_out[0] = heads[SSM_CONV - 2]
    for g in range(N_GROUPS):
        st_out[0, g * GROUP_WIDTH:(g + 1) * GROUP_WIDTH, :] = state[g].T


def _ssd_conv(x_raw, prm, heads, valid_rows):
    cw_ref, cb_ref = prm[:2]
    n_rows = x_raw.shape[0]
    row_in_slab = lax.broadcasted_iota(jnp.int32, (n_rows, CONV_DIM), 0) % SUBLANES
    acc = cb_ref[...] + x_raw * cw_ref[SSM_CONV - 1:SSM_CONV, :]
    for k in range(1, SSM_CONV):
        rot = pltpu.roll(x_raw.reshape(n_rows // SUBLANES, SUBLANES, CONV_DIM), k, 1).reshape(n_rows, CONV_DIM)
        above = jnp.concatenate([heads[k - 1], rot[:n_rows - SUBLANES]], axis=0)
        acc = acc + jnp.where(row_in_slab < k, above, rot) * cw_ref[SSM_CONV - 1 - k:SSM_CONV - k, :]
        heads[k - 1] = rot[valid_rows - SUBLANES:valid_rows]
    return acc * jax.nn.sigmoid(acc)


def _ssd_steps(dt_raw, prm, valid_rows):
    dtb_ref, alog_ref, expand_ref = prm[2], prm[3], prm[6]
    n_rows = dt_raw.shape[0]
    dt = _softplus(dt_raw + dtb_ref[...])
    if valid_rows < n_rows:
        rows = lax.broadcasted_iota(jnp.int32, (n_rows, DT_PAD), 0)
        dt = jnp.where(rows < valid_rows, dt, 0.0)
    ac = dt * (-jnp.exp(alog_ref[...]))
    ri = lax.broadcasted_iota(jnp.int32, (n_rows, n_rows), 0)
    ci = lax.broadcasted_iota(jnp.int32, (n_rows, n_rows), 1)
    same_chunk_causal = jnp.logical_and(ri // CHUNK == ci // CHUNK, ri >= ci)
    acs = jnp.dot(same_chunk_causal.astype(F32), ac, preferred_element_type=F32,
                  precision=lax.Precision.HIGHEST)
    expand = lambda v: jnp.dot(_split3(v), expand_ref[...], preferred_element_type=F32)
    return expand(acs), expand(dt)


def _ssd_chunk(xc, acol, dtcol, z, prm, state):
    dskip_ref, gssm_ref = prm[4], prm[5]
    q = CHUNK

    xs = xc[:, :D_INNER]
    bm = xc[:, D_INNER:D_INNER + N_GROUPS * D_STATE]
    cm = xc[:, D_INNER + N_GROUPS * D_STATE:].astype(BF16)

    last = acol[q - 1:q, :]
    dx = dtcol * xs
    wj = (jnp.exp(last - acol) * dx).astype(BF16)
    ea = jnp.exp(acol)

    lane = lax.broadcasted_iota(jnp.int32, (q, D_INNER), 1)
    rowi = lax.broadcasted_iota(jnp.int32, (q, D_INNER), 0)
    arow = jnp.sum(jnp.where(lane % q == rowi, acol, 0.0), axis=0, keepdims=True)

    tri = (lax.broadcasted_iota(jnp.int32, (q, MXU_DIM), 0)
           >= lax.broadcasted_iota(jnp.int32, (q, MXU_DIM), 1) % q)
    br = lax.broadcasted_iota(jnp.int32, (MXU_DIM, MXU_DIM), 0) // SSM_HEAD_DIM
    bc = lax.broadcasted_iota(jnp.int32, (MXU_DIM, MXU_DIM), 1) // SSM_HEAD_DIM
    blockdiag = br == bc

    y_blocks = []
    for g in range(N_GROUPS):
        cg = cm[:, g * D_STATE:(g + 1) * D_STATE]
        bg = bm[:, g * D_STATE:(g + 1) * D_STATE].astype(BF16)
        bg4 = jnp.concatenate([bg] * HEADS_PER_MXU_TILE, axis=0)
        cb4 = lax.dot_general(cg, bg4, (((1,), (1,)), ((), ())), preferred_element_type=F32)
        sg = state[g].astype(BF16)
        for half in range(GROUP_WIDTH // MXU_DIM):
            lo = g * GROUP_WIDTH + half * MXU_DIM
            cols = slice(lo, lo + MXU_DIM)
            seg = acol[:, cols] - arow[:, cols]
            decay = jnp.where(tri, jnp.exp(jnp.where(tri, seg, 0.0)), 0.0)
            mp = (cb4 * decay).astype(BF16)
            dxb = dx[:, cols].astype(BF16)
            bd = jnp.where(blockdiag, jnp.concatenate([dxb] * HEADS_PER_MXU_TILE, axis=0),
                           jnp.zeros((), BF16))
            y = jnp.dot(mp, bd, preferred_element_type=F32)
            y = y + jnp.dot(cg, sg[:, half * MXU_DIM:(half + 1) * MXU_DIM],
                            preferred_element_type=F32) * ea[:, cols]
            y_blocks.append(y + dskip_ref[:, cols] * xs[:, cols])

    el = jnp.exp(last)
    for g in range(N_GROUPS):
        cols = slice(g * GROUP_WIDTH, (g + 1) * GROUP_WIDTH)
        bgt = bm[:, g * D_STATE:(g + 1) * D_STATE].T.astype(BF16)
        state[g] = state[g] * el[:, cols] + jnp.dot(bgt, wj[:, cols], preferred_element_type=F32)

    zf = z.astype(F32)
    yg = jnp.concatenate(y_blocks, axis=-1) * (zf * jax.nn.sigmoid(zf))
    return (yg * _rms_scale(yg) * gssm_ref[...]).astype(BF16)


def _ssd_scratch():
    return [pltpu.VMEM((N_GROUPS, D_STATE, GROUP_WIDTH), F32),
            pltpu.VMEM((SSM_CONV - 1, SUBLANES, CONV_DIM), F32)]


def _ssd_kernel(*refs, has_state, valid_rows, n_chunks):
    if has_state:
        xbc_ref, dt_ref, z_ref, ctx_ref, h0_ref = refs[:5]
        refs = refs[5:]
    else:
        xbc_ref, dt_ref, z_ref, ctx_ref = refs[:4]
        h0_ref = None
        refs = refs[4:]
    prm, (yb_out, st_out, ctx_out, state, heads) = refs[:7], refs[7:]
    c = pl.program_id(1)

    @pl.when(c == 0)
    def _load_carry():
        _ssd_load_carry(ctx_ref, h0_ref, state, heads)

    xc = _ssd_conv(xbc_ref[0], prm, heads, valid_rows)
    acol, dtcol = _ssd_steps(dt_ref[0], prm, valid_rows)
    yb_out[0] = _ssd_chunk(xc, acol, dtcol, z_ref[0], prm, state)

    @pl.when(c == n_chunks - 1)
    def _emit_carry():
        _ssd_emit_carry(st_out, ctx_out, state, heads)


def _ssd(xbc, dt_raw, z, ctx, h0, prm, valid_rows):
    b, seq, _ = xbc.shape
    n_chunks = seq // CHUNK
    has_state = h0 is not None
    chunk = lambda w: pl.BlockSpec((1, CHUNK, w), lambda bi, ci: (bi, ci, 0))
    per_batch = lambda r, w: pl.BlockSpec((1, r, w), lambda bi, ci: (bi, 0, 0))
    in_specs = [chunk(CONV_DIM), chunk(DT_PAD), chunk(D_INNER), per_batch((SSM_CONV - 1) * SUBLANES, CONV_DIM)]
    args = [xbc, dt_raw, z, _conv_heads(ctx, SSM_CONV)]
    if has_state:
        in_specs.append(per_batch(D_INNER, D_STATE))
        args.append(h0)
    in_specs += [_resident(p.shape) for p in prm]
    args += list(prm)
    return pl.pallas_call(
        functools.partial(_ssd_kernel, has_state=has_state, valid_rows=valid_rows, n_chunks=n_chunks),
        grid=(b, n_chunks),
        in_specs=in_specs,
        out_specs=[chunk(D_INNER), per_batch(D_INNER, D_STATE), per_batch(SUBLANES, CONV_DIM)],
        out_shape=[jax.ShapeDtypeStruct((b, seq, D_INNER), BF16),
                   jax.ShapeDtypeStruct((b, D_INNER, D_STATE), F32),
                   jax.ShapeDtypeStruct((b, SUBLANES, CONV_DIM), F32)],
        scratch_shapes=_ssd_scratch(),
        compiler_params=_params(("parallel", "arbitrary")),
        name="ssd",
    )(*args)


def _inproj_kernel(*refs, tiles_per_batch, tail_tiles, fuse_ssd, tm):
    x_ref, g_ref, w_ref, wgates_ref, wdt_ref, gq_ref, gk_ref, hsum_ref = refs[:8]
    if fuse_ssd:
        ctx_ref, prm = refs[8], refs[9:16]
        q_out, k_out, v_out, gates_out, ktail_out, vtail_out, yb_out, st_out, ctx_out, state, heads = refs[16:]
    else:
        q_out, k_out, v_out, gates_out, ktail_out, vtail_out, z_out, xbc_out, dt_out = refs[8:]
    tile_in_batch = pl.program_id(0) % tiles_per_batch

    if fuse_ssd:
        @pl.when(tile_in_batch == 0)
        def _load_carry():
            _ssd_load_carry(ctx_ref, None, state, heads)

    x = x_ref[...]
    h = (x * _rms_scale(x) * g_ref[...]).astype(BF16)

    def seg(bounds):
        return jnp.dot(h, w_ref[:, bounds[0]:bounds[1]], preferred_element_type=F32)

    def head_norm(t, g):
        sq = (t * t).astype(BF16)
        parts = []
        for blk in range(ATT_WIDTH // MXU_DIM):
            cols = slice(blk * MXU_DIM, (blk + 1) * MXU_DIM)
            ss = jnp.dot(sq[:, cols], hsum_ref[...], preferred_element_type=F32)
            parts.append(t[:, cols] * lax.rsqrt(ss * (1.0 / HEAD_DIM_A) + EPS))
        return jnp.concatenate(parts, axis=-1) * g

    xbc = seg(_SEG_XBC)
    zb = seg(_SEG_Z).astype(BF16)
    dt_raw = jnp.dot(h, wdt_ref[...], preferred_element_type=F32)
    if fuse_ssd:
        xc = _ssd_conv(xbc, prm, heads, tm)
        acol, dtcol = _ssd_steps(dt_raw, prm, tm)
        for c in range(tm // CHUNK):
            rows = slice(c * CHUNK, (c + 1) * CHUNK)
            yb_out[rows, :] = _ssd_chunk(xc[rows, :], acol[rows, :], dtcol[rows, :], zb[rows, :], prm, state)
    else:
        xbc_out[...] = xbc
        z_out[...] = zb
        dt_out[...] = dt_raw

    q_out[...] = (head_norm(seg(_SEG_Q), gq_ref[...]) * (HEAD_DIM_A ** -0.5)).astype(BF16)
    k = head_norm(seg(_SEG_K), gk_ref[...])
    v = seg(_SEG_V)
    k_out[...] = k.astype(BF16)
    v_out[...] = v.astype(BF16)
    gates_out[...] = jnp.dot(h, wgates_ref[...], preferred_element_type=F32).astype(BF16)

    @pl.when(tile_in_batch >= tiles_per_batch - tail_tiles)
    def _emit_tail():
        ktail_out[...] = k.reshape(tm, N_HEADS_A, HEAD_DIM_A)
        vtail_out[...] = v.reshape(tm, N_HEADS_A, HEAD_DIM_A)

    if fuse_ssd:
        @pl.when(tile_in_batch == tiles_per_batch - 1)
        def _emit_carry():
            _ssd_emit_carry(st_out, ctx_out, state, heads)


def _inproj(x2d, w, tm, seq, ssm_ctx=None):
    t = x2d.shape[0]
    fuse_ssd = ssm_ctx is not None
    tiles_per_batch = max(seq // tm, 1)
    tail_tiles = max(min(ATT_PAST, seq) // tm, 1)
    n_tiles = t // tm
    n_batches = n_tiles // tiles_per_batch
    row = lambda width: pl.BlockSpec((tm, width), lambda i: (i, 0))
    first_tail = tiles_per_batch - tail_tiles
    tail = pl.BlockSpec((tm, N_HEADS_A, HEAD_DIM_A), lambda i: (
        (i // tiles_per_batch) * tail_tiles + jnp.maximum(i % tiles_per_batch - first_tail, 0), 0, 0))
    per_batch = lambda r, width: pl.BlockSpec((1, r, width), lambda i: (i // tiles_per_batch, 0, 0))
    rows_out = lambda width, dtype: (row(width), jax.ShapeDtypeStruct((t, width), dtype))
    tail_out = (tail, jax.ShapeDtypeStruct((n_batches * tail_tiles * tm, N_HEADS_A, HEAD_DIM_A), F32))

    in_specs = [row(D_MODEL), _resident((1, D_MODEL)), _resident((D_MODEL, _SEG_XBC[1])),
                _resident(w["w_gates"].shape), _resident(w["w_dt"].shape),
                _resident((1, ATT_WIDTH)), _resident((1, ATT_WIDTH)), _resident((MXU_DIM, MXU_DIM))]
    args = [x2d, w["g_mix"], w["w_in"], w["w_gates"], w["w_dt"], w["g_q"], w["g_k"], w["hsum"]]
    outs = [rows_out(ATT_WIDTH, BF16), rows_out(ATT_WIDTH, BF16), rows_out(ATT_WIDTH, BF16),
            rows_out(2 * D_MODEL, BF16), tail_out, tail_out]
    scratch = []
    if fuse_ssd:
        assert seq % tm == 0 and tm % CHUNK == 0
        prm = _ssd_params(w)
        in_specs += [per_batch((SSM_CONV - 1) * SUBLANES, CONV_DIM)] + [_resident(p.shape) for p in prm]
        args += [_conv_heads(ssm_ctx, SSM_CONV)] + list(prm)
        outs += [rows_out(D_INNER, BF16),
                 (per_batch(D_INNER, D_STATE), jax.ShapeDtypeStruct((n_batches, D_INNER, D_STATE), F32)),
                 (per_batch(SUBLANES, CONV_DIM), jax.ShapeDtypeStruct((n_batches, SUBLANES, CONV_DIM), F32))]
        scratch = _ssd_scratch()
    else:
        outs += [rows_out(D_INNER, BF16), rows_out(CONV_DIM, F32), rows_out(DT_PAD, F32)]
    return pl.pallas_call(
        functools.partial(_inproj_kernel, tiles_per_batch=tiles_per_batch, tail_tiles=tail_tiles,
                          fuse_ssd=fuse_ssd, tm=tm),
        grid=(n_tiles,),
        in_specs=in_specs,
        out_specs=[o[0] for o in outs],
        out_shape=[o[1] for o in outs],
        scratch_shapes=scratch,
        compiler_params=_params(("arbitrary",)),
        name="inproj_ssd" if fuse_ssd else "inproj",
    )(*args)


HEADS_PER_LANE_TILE = LANES // HEAD_DIM_A


def _attn_kernel(*refs, has_cache, seq, qb, width):
    if has_cache:
        q_ref, k_ref, v_ref, bias_ref, ck_ref, cv_ref, o_ref, kp, vp = refs
    else:
        q_ref, k_ref, v_ref, bias_ref, o_ref, kp, vp = refs
    nk = ATT_PAST + qb

    if has_cache:
        kp[0:ATT_PAST, :] = ck_ref[0].reshape(ATT_PAST, width).astype(BF16)
        vp[0:ATT_PAST, :] = cv_ref[0].reshape(ATT_PAST, width).astype(BF16)
    kp[ATT_PAST:ATT_PAST + seq, :] = k_ref[0]
    vp[ATT_PAST:ATT_PAST + seq, :] = v_ref[0]

    lane = lax.broadcasted_iota(jnp.int32, (qb, LANES), 1)
    low_half = lane < HEAD_DIM_A

    def q_block(start, first_col):
        kb = kp[pl.ds(start + first_col, nk - first_col), :]
        vb = vp[pl.ds(start + first_col, nk - first_col), :]
        q = q_ref[0, pl.ds(start, qb), :]
        outs = []
        for pair in range(width // LANES):
            cols = slice(pair * LANES, (pair + 1) * LANES)
            qp, kpair, vpair = q[:, cols], kb[:, cols], vb[:, cols]
            zero = jnp.zeros((), BF16)
            q2 = jnp.concatenate([jnp.where(low_half, qp, zero), jnp.where(low_half, zero, qp)], axis=0)
            s = lax.dot_general(q2, kpair, (((1,), (1,)), ((), ())), preferred_element_type=F32)
            first_head = pair * HEADS_PER_LANE_TILE
            bias = bias_ref[first_head:first_head + HEADS_PER_LANE_TILE, :, first_col:]
            s = s + bias.reshape(HEADS_PER_LANE_TILE * qb, nk - first_col)
            m = jnp.max(s, axis=-1, keepdims=True)
            p = jnp.exp(s - m)
            denom = jnp.sum(p, axis=-1, keepdims=True)
            o2 = jnp.dot(p.astype(BF16), vpair, preferred_element_type=F32) / denom
            outs.append(jnp.where(low_half, o2[:qb], o2[qb:]))
        o_ref[0, pl.ds(start, qb), :] = jnp.concatenate(outs, axis=-1).astype(BF16)

    n_blocks = seq // qb
    n_peeled = 0 if has_cache else min(ATT_PAST // qb, n_blocks)
    for i in range(n_peeled):
        q_block(i * qb, ATT_PAST - i * qb)

    n_steady = n_blocks - n_peeled
    unroll = 2 if n_steady % 2 == 0 and n_steady > 0 else 1

    def steady(i, carry):
        for u in range(unroll):
            q_block(pl.multiple_of((n_peeled + i * unroll + u) * qb, qb), 0)
        return carry

    lax.fori_loop(0, n_steady // unroll, steady, 0)


def _attention(q, k, v, bias, cache, qb):
    b, seq, _ = q.shape
    nk = ATT_PAST + qb
    has_cache = cache is not None
    width = MXU_DIM if seq > ATT_PAST else ATT_WIDTH
    hb = ATT_WIDTH // width
    seq_spec = pl.BlockSpec((1, seq, width), lambda bi, hi: (bi, 0, hi))
    in_specs = [seq_spec, seq_spec, seq_spec,
                pl.BlockSpec((width // HEAD_DIM_A, qb, nk), lambda bi, hi: (hi, 0, 0))]
    args = [q, k, v, bias]
    if has_cache:
        cspec = pl.BlockSpec((1, ATT_PAST, width // HEAD_DIM_A, HEAD_DIM_A), lambda bi, hi: (bi, 0, hi, 0))
        in_specs += [cspec, cspec]
        args += list(cache)
    return pl.pallas_call(
        functools.partial(_attn_kernel, has_cache=has_cache, seq=seq, qb=qb, width=width),
        grid=(b, hb),
        in_specs=in_specs,
        out_specs=seq_spec,
        out_shape=jax.ShapeDtypeStruct((b, seq, ATT_WIDTH), BF16),
        scratch_shapes=[pltpu.VMEM((ATT_PAST + seq, width), BF16),
                        pltpu.VMEM((ATT_PAST + seq, width), BF16)],
        compiler_params=_params(("parallel", "parallel")),
        name="band_attention",
    )(*args)


def _attention_bias(table, qb):
    nk = ATT_PAST + qb
    n_diag = qb + nk - 1
    diag_rel = (ATT_PAST + qb - 1) - jnp.arange(n_diag)
    r = table[:, jnp.clip(diag_rel, -REL_CLIP, REL_CLIP) + REL_CLIP]
    h = table.shape[0]
    row_len = -(-(n_diag + 1) // LANES) * LANES
    rp = jnp.pad(r, ((0, 0), (1, row_len - n_diag)))
    skew = jnp.tile(rp, (1, qb))[:, :qb * row_len].reshape(h, qb, row_len)
    bias = skew[:, :, qb:qb + nk]
    qc = jnp.arange(qb)[:, None] // CHUNK
    kc = jnp.arange(nk)[None, :] // CHUNK
    visible = (kc >= qc) & (kc <= qc + ATT_PAST // CHUNK)
    return jnp.where(visible[None], bias, NEG_INF).astype(F32)


def _ffn_kernel(x_ref, ya_ref, yb_ref, gates_ref, pe_ref, ctx_ref, bg_ref, wa_ref, wb_ref, wo_ref,
                gffn_ref, wup_ref, cw_ref, cb_ref, wdown_ref, gple_ref, wgate_ref, wple_ref,
                o_ref, ctx_out, heads, *, tm, seq_rows, tiles_per_batch):
    i = pl.program_id(0)
    n_seqs = tm // seq_rows

    @pl.when(i % tiles_per_batch == 0)
    def _load_context():
        for k in range(FFN_CONV - 1):
            heads[k] = ctx_ref[:, k * SUBLANES:(k + 1) * SUBLANES, :]

    mix_gate = jax.nn.sigmoid(gates_ref[...].astype(F32) + bg_ref[...])
    mixed = (mix_gate[:, :D_MODEL] * jnp.dot(ya_ref[...], wa_ref[...], preferred_element_type=F32)
             + mix_gate[:, D_MODEL:] * jnp.dot(yb_ref[...], wb_ref[...], preferred_element_type=F32))
    x = x_ref[...] + jnp.dot(mixed.astype(BF16), wo_ref[...], preferred_element_type=F32)
    xn = (x * _rms_scale(x) * gffn_ref[...]).astype(BF16)
    n_slabs = tm // SUBLANES
    row_in_slab = lax.broadcasted_iota(jnp.int32, (tm, FFN_COL_BLOCK), 0) % SUBLANES

    def up(j, part):
        lo = part * D_FF + j * FFN_COL_BLOCK
        return jnp.dot(xn, wup_ref[:, lo:lo + FFN_COL_BLOCK], preferred_element_type=F32)

    def conv(u, j, part):
        lo = part * D_FF + j * FFN_COL_BLOCK
        cols = slice(lo, lo + FFN_COL_BLOCK)
        out = cb_ref[:, cols] + u * cw_ref[FFN_CONV - 1:FFN_CONV, cols]
        for k in range(1, FFN_CONV):
            rot = pltpu.roll(u.reshape(n_slabs, SUBLANES, FFN_COL_BLOCK), k, 1).reshape(tm, FFN_COL_BLOCK)
            pieces = []
            for s in range(n_seqs):
                pieces += [heads[k - 1, s, :, cols], rot[s * seq_rows:(s + 1) * seq_rows - SUBLANES]]
            shifted = jnp.where(row_in_slab < k, jnp.concatenate(pieces, axis=0), rot)
            out = out + shifted * cw_ref[FFN_CONV - 1 - k:FFN_CONV - k, cols]
            for s in range(n_seqs):
                heads[k - 1, s, :, cols] = rot[(s + 1) * seq_rows - SUBLANES:(s + 1) * seq_rows]
        return out

    n_blocks = D_FF // FFN_COL_BLOCK
    acc = jnp.zeros((tm, D_MODEL), F32)
    pending = (up(0, 0), up(0, 1))
    for j in range(n_blocks):
        ug, uv = pending
        if j + 1 < n_blocks:
            pending = (up(j + 1, 0), up(j + 1, 1))
        hidden = (jax.nn.gelu(conv(ug, j, 0), approximate=True) * conv(uv, j, 1)).astype(BF16)
        acc = acc + jnp.dot(hidden, wdown_ref[j * FFN_COL_BLOCK:(j + 1) * FFN_COL_BLOCK, :],
                            preferred_element_type=F32)

    @pl.when(i % tiles_per_batch == tiles_per_batch - 1)
    def _emit_context():
        ctx_out[...] = heads[FFN_CONV - 2]

    x2 = x + acc
    gate = jax.nn.sigmoid(jnp.dot((x2 * _rms_scale(x2) * gple_ref[...]).astype(BF16), wgate_ref[...],
                                  preferred_element_type=F32))
    o_ref[...] = x2 + gate * jnp.dot(pe_ref[...].astype(BF16), wple_ref[...], preferred_element_type=F32)


def _merge_ffn(x2d, ya, yb, gates, pe2d, ctx, prm, seq, tm):
    t = x2d.shape[0]
    batch = t // seq
    seq_rows = min(seq, tm)
    seqs_per_tile = tm // seq_rows
    tiles_per_batch = seq // seq_rows
    row = lambda w: pl.BlockSpec((tm, w), lambda i: (i, 0))
    per_seq = lambda r: pl.BlockSpec((seqs_per_tile, r, 2 * D_FF), lambda i: (i // tiles_per_batch, 0, 0))
    y, ctx_out = pl.pallas_call(
        functools.partial(_ffn_kernel, tm=tm, seq_rows=seq_rows, tiles_per_batch=tiles_per_batch),
        grid=(t // tm,),
        in_specs=[row(D_MODEL), row(ATT_WIDTH), row(D_INNER), row(2 * D_MODEL), row(PLE_DIM),
                  per_seq((FFN_CONV - 1) * SUBLANES)] + [_resident(p.shape) for p in prm],
        out_specs=[row(D_MODEL), per_seq(SUBLANES)],
        out_shape=[jax.ShapeDtypeStruct((t, D_MODEL), F32),
                   jax.ShapeDtypeStruct((batch, SUBLANES, 2 * D_FF), F32)],
        scratch_shapes=[pltpu.VMEM((FFN_CONV - 1, seqs_per_tile, SUBLANES, 2 * D_FF), F32)],
        compiler_params=_params(("arbitrary",)),
        name="merge_convffn_ple",
    )(x2d, ya, yb, gates, pe2d, _conv_heads(ctx, FFN_CONV), *prm)
    return y, ctx_out[:, :FFN_CONV - 1]


def _ssd_params(w):
    return (w["conv_ssm_w"], w["conv_ssm_b"], w["dt_bias"], w["a_log"], w["d_skip"], w["g_ssm"], w["expand"])


def _layer(x, pe, kv_cache, ssm_ctx, h0, ffn_ctx, w, tm, tq, tm_ffn):
    b, seq, _ = x.shape
    t = b * seq
    x2d = x.reshape(t, D_MODEL)
    as3 = lambda a: a.reshape(b, seq, -1)
    fuse_ssd = h0 is None and seq % tm == 0 and tm % CHUNK == 0
    if fuse_ssd:
        qn, kn, v, gates, k_tail, v_tail, yb, st, ctx_ssm = _inproj(x2d, w, tm, seq, ssm_ctx)
    else:
        qn, kn, v, gates, k_tail, v_tail, z, xbc, dt_raw = _inproj(x2d, w, tm, seq)
        seq_pad = -(-seq // CHUNK) * CHUNK
        valid_rows = CHUNK - (seq_pad - seq)
        pad = lambda a: jnp.pad(as3(a), ((0, 0), (0, seq_pad - seq), (0, 0)))
        yb, st, ctx_ssm = _ssd(pad(xbc), pad(dt_raw), pad(z), ssm_ctx, h0, _ssd_params(w), valid_rows)
        yb = yb[:, :seq].reshape(t, D_INNER)

    ya = _attention(as3(qn), as3(kn), as3(v), _attention_bias(w["rel_bias"], tq), kv_cache, tq)

    ffn_prm = (w["b_gate"], w["w_proj_a"], w["w_proj_b"], w["w_out"],
               w["g_ffn"], w["w_up"], w["ffn_conv_w"], w["ffn_conv_b"], w["w_down"],
               w["g_ple"], w["w_ple_gate"], w["w_ple"])
    y, ctx_ffn = _merge_ffn(x2d, ya.reshape(t, ATT_WIDTH), yb, gates, pe.reshape(t, PLE_DIM), ffn_ctx, ffn_prm,
                            seq, tm_ffn)

    heads = lambda a: a.reshape(b, -1, N_HEADS_A, HEAD_DIM_A)
    state = st.reshape(b, N_HEADS_S, SSM_HEAD_DIM, D_STATE)
    return y.reshape(b, seq, D_MODEL), (heads(k_tail), heads(v_tail), state, ctx_ssm[:, :SSM_CONV - 1], ctx_ffn)


def _prep_weights(i, g_mix, w_in, b_gate, g_q, g_k, rel_bias, conv_ssm_w, conv_ssm_b, dt_bias, a_log, d_skip,
                  g_ssm, w_proj_a, w_proj_b, w_out, g_ffn, w_up, ffn_conv_w, ffn_conv_b, w_down, g_ple,
                  w_ple_gate, w_ple):
    row = lambda a: a[i].reshape(1, -1).astype(F32)
    pad_lanes = lambda a, n: jnp.pad(a, ((0, 0), (0, n - a.shape[1])))
    wi = w_in[i].astype(BF16)
    head_of_lane = jnp.arange(D_INNER) // SSM_HEAD_DIM
    expand1 = (jnp.arange(DT_PAD)[:, None] == head_of_lane[None, :]).astype(BF16)
    blk = jnp.arange(MXU_DIM) // HEAD_DIM_A
    return {
        "g_mix": row(g_mix), "w_in": wi, "w_gates": wi[:, _SEG_GATES[0]:_SEG_GATES[1]],
        "w_dt": pad_lanes(wi[:, _SEG_DT[0]:_SEG_DT[1]], DT_PAD), "b_gate": row(b_gate),
        "g_q": jnp.tile(row(g_q), (1, N_HEADS_A)), "g_k": jnp.tile(row(g_k), (1, N_HEADS_A)),
        "hsum": (blk[:, None] == blk[None, :]).astype(BF16),
        "rel_bias": rel_bias[i].astype(F32),
        "conv_ssm_w": jnp.pad(conv_ssm_w[i], ((0, SUBLANES - SSM_CONV), (0, 0))), "conv_ssm_b": row(conv_ssm_b),
        "dt_bias": pad_lanes(row(dt_bias), DT_PAD), "a_log": pad_lanes(row(a_log), DT_PAD),
        "d_skip": jnp.repeat(row(d_skip), SSM_HEAD_DIM, axis=1), "g_ssm": row(g_ssm),
        "expand": jnp.concatenate([expand1] * 3, axis=0),
        "w_proj_a": w_proj_a[i].astype(BF16), "w_proj_b": w_proj_b[i].astype(BF16), "w_out": w_out[i].astype(BF16),
        "g_ffn": row(g_ffn), "w_up": w_up[i].astype(BF16),
        "ffn_conv_w": jnp.pad(ffn_conv_w[i], ((0, SUBLANES - FFN_CONV), (0, 0))), "ffn_conv_b": row(ffn_conv_b),
        "w_down": w_down[i].astype(BF16), "g_ple": row(g_ple),
        "w_ple_gate": w_ple_gate[i].astype(BF16), "w_ple": w_ple[i].astype(BF16),
    }


def kernel(x_prompt, x_sample, cache_k, cache_v, state_ssm, state_conv_ssm, state_conv_ffn, p_prompt, p_sample, g_mix, w_in, b_gate, g_q, g_k, rel_bias, conv_ssm_w, conv_ssm_b, dt_bias, a_log, d_skip, g_ssm, w_proj_a, w_proj_b, w_out, g_ffn, w_up, ffn_conv_w, ffn_conv_b, w_down, g_ple, w_ple_gate, w_ple):
    depth = w_in.shape[0]
    bp = x_prompt.shape[0]
    bs, seq_s = x_sample.shape[:2]
    yp, ys = x_prompt, x_sample
    sp, ss = [], []
    for i in range(depth):
        w = _prep_weights(i, g_mix, w_in, b_gate, g_q, g_k, rel_bias, conv_ssm_w, conv_ssm_b, dt_bias, a_log,
                          d_skip, g_ssm, w_proj_a, w_proj_b, w_out, g_ffn, w_up, ffn_conv_w, ffn_conv_b, w_down,
                          g_ple, w_ple_gate, w_ple)
        yp, st_p = _layer(yp, p_prompt[i], None,
                          jnp.zeros((bp, SSM_CONV - 1, CONV_DIM), F32), None,
                          jnp.zeros((bp, FFN_CONV - 1, 2 * D_FF), F32), w, tm=256, tq=4 * CHUNK, tm_ffn=256)
        h0 = state_ssm[i].reshape(bs, D_INNER, D_STATE)
        ys, st_s = _layer(ys, p_sample[i], (cache_k[i], cache_v[i]), state_conv_ssm[i], h0,
                          state_conv_ffn[i], w, tm=seq_s * bs, tq=seq_s, tm_ffn=seq_s * bs)
        sp.append(st_p)
        ss.append(st_s)

    stack = lambda lst, j: jnp.stack([s[j] for s in lst], axis=0)
    return (yp, ys,
            stack(sp, 0), stack(sp, 1), stack(sp, 2), stack(sp, 3), stack(sp, 4),
            stack(ss, 0), stack(ss, 1), stack(ss, 2), stack(ss, 3), stack(ss, 4))
```

```python
import functools

import jax
import jax.numpy as jnp
from jax import lax
from jax.experimental import pallas as pl
from jax.experimental.pallas import tpu as pltpu

F32 = jnp.float32
BF16 = jnp.bfloat16

D_MODEL = 1024
CHUNK = 64
PLE_DIM = 256
EPS = 1e-6
NEG_INF = -1e30
N_HEADS_A = 16
HEAD_DIM_A = 64
ATT_WIDTH = N_HEADS_A * HEAD_DIM_A
ATT_PAST = 8 * CHUNK
REL_CLIP = 128
D_INNER = 2 * D_MODEL
SSM_HEAD_DIM = 64
N_HEADS_S = D_INNER // SSM_HEAD_DIM
N_GROUPS = 4
GROUP_WIDTH = D_INNER // N_GROUPS
D_STATE = 128
SSM_CONV = 4
CONV_DIM = D_INNER + 2 * N_GROUPS * D_STATE
D_FF = 3 * D_MODEL
FFN_CONV = 3

LANES = 128
SUBLANES = 8
MXU_DIM = 256
VMEM_LIMIT_BYTES = 56 * 1024 * 1024

DT_PAD = LANES
HEADS_PER_MXU_TILE = MXU_DIM // SSM_HEAD_DIM
FFN_COL_BLOCK = 1024

_SEG_Q = (0, ATT_WIDTH)
_SEG_K = (ATT_WIDTH, 2 * ATT_WIDTH)
_SEG_V = (2 * ATT_WIDTH, 3 * ATT_WIDTH)
_SEG_Z = (3 * ATT_WIDTH, 3 * ATT_WIDTH + D_INNER)
_SEG_XBC = (_SEG_Z[1], _SEG_Z[1] + CONV_DIM)
_SEG_DT = (_SEG_XBC[1], _SEG_XBC[1] + N_HEADS_S)
_SEG_GATES = (_SEG_DT[1], _SEG_DT[1] + 2 * D_MODEL)


def _resident(shape):
    nd = len(shape)
    return pl.BlockSpec(shape, lambda *_: (0,) * nd, pipeline_mode=pl.Buffered(1))


def _params(semantics):
    return pltpu.CompilerParams(dimension_semantics=semantics, vmem_limit_bytes=VMEM_LIMIT_BYTES)


def _rms_scale(x):
    return lax.rsqrt(jnp.mean(x * x, axis=-1, keepdims=True) + EPS)


def _conv_heads(ctx, width):
    heads = [jnp.pad(ctx[:, width - 1 - k:], ((0, 0), (0, SUBLANES - k), (0, 0))) for k in range(1, width)]
    return jnp.concatenate(heads, axis=1)


def _softplus(x):
    return jnp.maximum(x, 0.0) + jnp.log(1.0 + jnp.exp(-jnp.abs(x)))


def _split3(v):
    hi = v.astype(BF16)
    r1 = v - hi.astype(F32)
    mid = r1.astype(BF16)
    lo = (r1 - mid.astype(F32)).astype(BF16)
    return jnp.concatenate([hi, mid, lo], axis=-1)


def _ssd_load_carry(ctx_ref, h0_ref, state, heads):
    for k in range(SSM_CONV - 1):
        heads[k] = ctx_ref[0, k * SUBLANES:(k + 1) * SUBLANES, :]
    for g in range(N_GROUPS):
        if h0_ref is None:
            state[g] = jnp.zeros((D_STATE, GROUP_WIDTH), F32)
        else:
            state[g] = h0_ref[0, g * GROUP_WIDTH:(g + 1) * GROUP_WIDTH, :].T


def _ssd_emit_carry(st_out, ctx_out, state, heads):
    ctx_out[0] = heads[SSM_CONV - 2]
    for g in range(N_GROUPS):
        st_out[0, g * GROUP_WIDTH:(g + 1) * GROUP_WIDTH, :] = state[g].T


def _ssd_conv(x_raw, prm, heads, valid_rows):
    cw_ref, cb_ref = prm[:2]
    n_rows = x_raw.shape[0]
    row_in_slab = lax.broadcasted_iota(jnp.int32, (n_rows, CONV_DIM), 0) % SUBLANES
    acc = cb_ref[...] + x_raw * cw_ref[SSM_CONV - 1:SSM_CONV, :]
    for k in range(1, SSM_CONV):
        rot = pltpu.roll(x_raw.reshape(n_rows // SUBLANES, SUBLANES, CONV_DIM), k, 1).reshape(n_rows, CONV_DIM)
        above = jnp.concatenate([heads[k - 1], rot[:n_rows - SUBLANES]], axis=0)
        acc = acc + jnp.where(row_in_slab < k, above, rot) * cw_ref[SSM_CONV - 1 - k:SSM_CONV - k, :]
        heads[k - 1] = rot[valid_rows - SUBLANES:valid_rows]
    return acc * jax.nn.sigmoid(acc)


def _ssd_steps(dt_raw, prm, valid_rows):
    dtb_ref, alog_ref, expand_ref = prm[2], prm[3], prm[6]
    n_rows = dt_raw.shape[0]
    dt = _softplus(dt_raw + dtb_ref[...])
    if valid_rows < n_rows:
        rows = lax.broadcasted_iota(jnp.int32, (n_rows, DT_PAD), 0)
        dt = jnp.where(rows < valid_rows, dt, 0.0)
    ac = dt * (-jnp.exp(alog_ref[...]))
    ri = lax.broadcasted_iota(jnp.int32, (n_rows, n_rows), 0)
    ci = lax.broadcasted_iota(jnp.int32, (n_rows, n_rows), 1)
    same_chunk_causal = jnp.logical_and(ri // CHUNK == ci // CHUNK, ri >= ci)
    acs = jnp.dot(same_chunk_causal.astype(F32), ac, preferred_element_type=F32,
                  precision=lax.Precision.HIGHEST)
    expand = lambda v: jnp.dot(_split3(v), expand_ref[...], preferred_element_type=F32)
    return expand(acs), expand(dt)


def _ssd_chunk(xc, acol, dtcol, z, prm, state):
    dskip_ref, gssm_ref = prm[4], prm[5]
    q = CHUNK

    xs = xc[:, :D_INNER]
    bm = xc[:, D_INNER:D_INNER + N_GROUPS * D_STATE]
    cm = xc[:, D_INNER + N_GROUPS * D_STATE:].astype(BF16)

    last = acol[q - 1:q, :]
    dx = dtcol * xs
    wj = (jnp.exp(last - acol) * dx).astype(BF16)
    ea = jnp.exp(acol)

    lane = lax.broadcasted_iota(jnp.int32, (q, D_INNER), 1)
    rowi = lax.broadcasted_iota(jnp.int32, (q, D_INNER), 0)
    arow = jnp.sum(jnp.where(lane % q == rowi, acol, 0.0), axis=0, keepdims=True)

    tri = (lax.broadcasted_iota(jnp.int32, (q, MXU_DIM), 0)
           >= lax.broadcasted_iota(jnp.int32, (q, MXU_DIM), 1) % q)
    br = lax.broadcasted_iota(jnp.int32, (MXU_DIM, MXU_DIM), 0) // SSM_HEAD_DIM
    bc = lax.broadcasted_iota(jnp.int32, (MXU_DIM, MXU_DIM), 1) // SSM_HEAD_DIM
    blockdiag = br == bc

    y_blocks = []
    for g in range(N_GROUPS):
        cg = cm[:, g * D_STATE:(g + 1) * D_STATE]
        bg = bm[:, g * D_STATE:(g + 1) * D_STATE].astype(BF16)
        bg4 = jnp.concatenate([bg] * HEADS_PER_MXU_TILE, axis=0)
        cb4 = lax.dot_general(cg, bg4, (((1,), (1,)), ((), ())), preferred_element_type=F32)
        sg = state[g].astype(BF16)
        for half in range(GROUP_WIDTH // MXU_DIM):
            lo = g * GROUP_WIDTH + half * MXU_DIM
            cols = slice(lo, lo + MXU_DIM)
            seg = acol[:, cols] - arow[:, cols]
            decay = jnp.where(tri, jnp.exp(jnp.where(tri, seg, 0.0)), 0.0)
            mp = (cb4 * decay).astype(BF16)
            dxb = dx[:, cols].astype(BF16)
            bd = jnp.where(blockdiag, jnp.concatenate([dxb] * HEADS_PER_MXU_TILE, axis=0),
                           jnp.zeros((), BF16))
            y = jnp.dot(mp, bd, preferred_element_type=F32)
            y = y + jnp.dot(cg, sg[:, half * MXU_DIM:(half + 1) * MXU_DIM],
                            preferred_element_type=F32) * ea[:, cols]
            y_blocks.append(y + dskip_ref[:, cols] * xs[:, cols])

    el = jnp.exp(last)
    for g in range(N_GROUPS):
        cols = slice(g * GROUP_WIDTH, (g + 1) * GROUP_WIDTH)
        bgt = bm[:, g * D_STATE:(g + 1) * D_STATE].T.astype(BF16)
        state[g] = state[g] * el[:, cols] + jnp.dot(bgt, wj[:, cols], preferred_element_type=F32)

    zf = z.astype(F32)
    yg = jnp.concatenate(y_blocks, axis=-1) * (zf * jax.nn.sigmoid(zf))
    return (yg * _rms_scale(yg) * gssm_ref[...]).astype(BF16)


def _ssd_scratch():
    return [pltpu.VMEM((N_GROUPS, D_STATE, GROUP_WIDTH), F32),
            pltpu.VMEM((SSM_CONV - 1, SUBLANES, CONV_DIM), F32)]


def _ssd_kernel(*refs, has_state, valid_rows, n_chunks):
    if has_state:
        xbc_ref, dt_ref, z_ref, ctx_ref, h0_ref = refs[:5]
        refs = refs[5:]
    else:
        xbc_ref, dt_ref, z_ref, ctx_ref = refs[:4]
        h0_ref = None
        refs = refs[4:]
    prm, (yb_out, st_out, ctx_out, state, heads) = refs[:7], refs[7:]
    c = pl.program_id(1)

    @pl.when(c == 0)
    def _load_carry():
        _ssd_load_carry(ctx_ref, h0_ref, state, heads)

    xc = _ssd_conv(xbc_ref[0], prm, heads, valid_rows)
    acol, dtcol = _ssd_steps(dt_ref[0], prm, valid_rows)
    yb_out[0] = _ssd_chunk(xc, acol, dtcol, z_ref[0], prm, state)

    @pl.when(c == n_chunks - 1)
    def _emit_carry():
        _ssd_emit_carry(st_out, ctx_out, state, heads)


def _ssd(xbc, dt_raw, z, ctx, h0, prm, valid_rows):
    b, seq, _ = xbc.shape
    n_chunks = seq // CHUNK
    has_state = h0 is not None
    chunk = lambda w: pl.BlockSpec((1, CHUNK, w), lambda bi, ci: (bi, ci, 0))
    per_batch = lambda r, w: pl.BlockSpec((1, r, w), lambda bi, ci: (bi, 0, 0))
    in_specs = [chunk(CONV_DIM), chunk(DT_PAD), chunk(D_INNER), per_batch((SSM_CONV - 1) * SUBLANES, CONV_DIM)]
    args = [xbc, dt_raw, z, _conv_heads(ctx, SSM_CONV)]
    if has_state:
        in_specs.append(per_batch(D_INNER, D_STATE))
        args.append(h0)
    in_specs += [_resident(p.shape) for p in prm]
    args += list(prm)
    return pl.pallas_call(
        functools.partial(_ssd_kernel, has_state=has_state, valid_rows=valid_rows, n_chunks=n_chunks),
        grid=(b, n_chunks),
        in_specs=in_specs,
        out_specs=[chunk(D_INNER), per_batch(D_INNER, D_STATE), per_batch(SUBLANES, CONV_DIM)],
        out_shape=[jax.ShapeDtypeStruct((b, seq, D_INNER), BF16),
                   jax.ShapeDtypeStruct((b, D_INNER, D_STATE), F32),
                   jax.ShapeDtypeStruct((b, SUBLANES, CONV_DIM), F32)],
        scratch_shapes=_ssd_scratch(),
        compiler_params=_params(("parallel", "arbitrary")),
        name="ssd",
    )(*args)


def _inproj_kernel(*refs, tiles_per_batch, tail_tiles, fuse_ssd, tm):
    x_ref, g_ref, w_ref, wgates_ref, wdt_ref, gq_ref, gk_ref, hsum_ref = refs[:8]
    if fuse_ssd:
        ctx_ref, prm = refs[8], refs[9:16]
        q_out, k_out, v_out, gates_out, ktail_out, vtail_out, yb_out, st_out, ctx_out, state, heads = refs[16:]
    else:
        q_out, k_out, v_out, gates_out, ktail_out, vtail_out, z_out, xbc_out, dt_out = refs[8:]
    tile_in_batch = pl.program_id(0) % tiles_per_batch

    if fuse_ssd:
        @pl.when(tile_in_batch == 0)
        def _load_carry():
            _ssd_load_carry(ctx_ref, None, state, heads)

    x = x_ref[...]
    h = (x * _rms_scale(x) * g_ref[...]).astype(BF16)

    def seg(bounds):
        return jnp.dot(h, w_ref[:, bounds[0]:bounds[1]], preferred_element_type=F32)

    def head_norm(t, g):
        sq = (t * t).astype(BF16)
        parts = []
        for blk in range(ATT_WIDTH // MXU_DIM):
            cols = slice(blk * MXU_DIM, (blk + 1) * MXU_DIM)
            ss = jnp.dot(sq[:, cols], hsum_ref[...], preferred_element_type=F32)
            parts.append(t[:, cols] * lax.rsqrt(ss * (1.0 / HEAD_DIM_A) + EPS))
        return jnp.concatenate(parts, axis=-1) * g

    xbc = seg(_SEG_XBC)
    zb = seg(_SEG_Z).astype(BF16)
    dt_raw = jnp.dot(h, wdt_ref[...], preferred_element_type=F32)
    if fuse_ssd:
        xc = _ssd_conv(xbc, prm, heads, tm)
        acol, dtcol = _ssd_steps(dt_raw, prm, tm)
        for c in range(tm // CHUNK):
            rows = slice(c * CHUNK, (c + 1) * CHUNK)
            yb_out[rows, :] = _ssd_chunk(xc[rows, :], acol[rows, :], dtcol[rows, :], zb[rows, :], prm, state)
    else:
        xbc_out[...] = xbc
        z_out[...] = zb
        dt_out[...] = dt_raw

    q_out[...] = (head_norm(seg(_SEG_Q), gq_ref[...]) * (HEAD_DIM_A ** -0.5)).astype(BF16)
    k = head_norm(seg(_SEG_K), gk_ref[...])
    v = seg(_SEG_V)
    k_out[...] = k.astype(BF16)
    v_out[...] = v.astype(BF16)
    gates_out[...] = jnp.dot(h, wgates_ref[...], preferred_element_type=F32).astype(BF16)

    @pl.when(tile_in_batch >= tiles_per_batch - tail_tiles)
    def _emit_tail():
        ktail_out[0] = k.T
        vtail_out[0] = v.T

    if fuse_ssd:
        @pl.when(tile_in_batch == tiles_per_batch - 1)
        def _emit_carry():
            _ssd_emit_carry(st_out, ctx_out, state, heads)


def _inproj(x2d, w, tm, seq, ssm_ctx=None):
    t = x2d.shape[0]
    fuse_ssd = ssm_ctx is not None
    tiles_per_batch = max(seq // tm, 1)
    tail_tiles = max(min(ATT_PAST, seq) // tm, 1)
    n_tiles = t // tm
    n_batches = n_tiles // tiles_per_batch
    row = lambda width: pl.BlockSpec((tm, width), lambda i: (i, 0))
    first_tail = tiles_per_batch - tail_tiles
    tail = pl.BlockSpec((1, ATT_WIDTH, tm), lambda i: (
        i // tiles_per_batch, 0, jnp.maximum(i % tiles_per_batch - first_tail, 0)))
    per_batch = lambda r, width: pl.BlockSpec((1, r, width), lambda i: (i // tiles_per_batch, 0, 0))
    rows_out = lambda width, dtype: (row(width), jax.ShapeDtypeStruct((t, width), dtype))
    tail_out = (tail, jax.ShapeDtypeStruct((n_batches, ATT_WIDTH, tail_tiles * tm), F32))

    in_specs = [row(D_MODEL), _resident((1, D_MODEL)), _resident((D_MODEL, _SEG_XBC[1])),
                _resident(w["w_gates"].shape), _resident(w["w_dt"].shape),
                _resident((1, ATT_WIDTH)), _resident((1, ATT_WIDTH)), _resident((MXU_DIM, MXU_DIM))]
    args = [x2d, w["g_mix"], w["w_in"], w["w_gates"], w["w_dt"], w["g_q"], w["g_k"], w["hsum"]]
    outs = [rows_out(ATT_WIDTH, BF16), rows_out(ATT_WIDTH, BF16), rows_out(ATT_WIDTH, BF16),
            rows_out(2 * D_MODEL, BF16), tail_out, tail_out]
    scratch = []
    if fuse_ssd:
        assert seq % tm == 0 and tm % CHUNK == 0
        prm = _ssd_params(w)
        in_specs += [per_batch((SSM_CONV - 1) * SUBLANES, CONV_DIM)] + [_resident(p.shape) for p in prm]
        args += [_conv_heads(ssm_ctx, SSM_CONV)] + list(prm)
        outs += [rows_out(D_INNER, BF16),
                 (per_batch(D_INNER, D_STATE), jax.ShapeDtypeStruct((n_batches, D_INNER, D_STATE), F32)),
                 (per_batch(SUBLANES, CONV_DIM), jax.ShapeDtypeStruct((n_batches, SUBLANES, CONV_DIM), F32))]
        scratch = _ssd_scratch()
    else:
        outs += [rows_out(D_INNER, BF16), rows_out(CONV_DIM, F32), rows_out(DT_PAD, F32)]
    return pl.pallas_call(
        functools.partial(_inproj_kernel, tiles_per_batch=tiles_per_batch, tail_tiles=tail_tiles,
                          fuse_ssd=fuse_ssd, tm=tm),
        grid=(n_tiles,),
        in_specs=in_specs,
        out_specs=[o[0] for o in outs],
        out_shape=[o[1] for o in outs],
        scratch_shapes=scratch,
        compiler_params=_params(("arbitrary",)),
        name="inproj_ssd" if fuse_ssd else "inproj",
    )(*args)


HEADS_PER_LANE_TILE = LANES // HEAD_DIM_A


def _attn_kernel(*refs, has_cache, seq, qb, width):
    if has_cache:
        q_ref, k_ref, v_ref, bias_ref, ck_ref, cv_ref, o_ref, kp, vp = refs
    else:
        q_ref, k_ref, v_ref, bias_ref, o_ref, kp, vp = refs
    nk = ATT_PAST + qb

    if has_cache:
        kp[0:ATT_PAST, :] = ck_ref[0].T.astype(BF16)
        vp[0:ATT_PAST, :] = cv_ref[0].T.astype(BF16)
    kp[ATT_PAST:ATT_PAST + seq, :] = k_ref[0]
    vp[ATT_PAST:ATT_PAST + seq, :] = v_ref[0]

    lane = lax.broadcasted_iota(jnp.int32, (qb, LANES), 1)
    low_half = lane < HEAD_DIM_A

    def q_block(start, first_col):
        kb = kp[pl.ds(start + first_col, nk - first_col), :]
        vb = vp[pl.ds(start + first_col, nk - first_col), :]
        q = q_ref[0, pl.ds(start, qb), :]
        outs = []
        for pair in range(width // LANES):
            cols = slice(pair * LANES, (pair + 1) * LANES)
            qp, kpair, vpair = q[:, cols], kb[:, cols], vb[:, cols]
            zero = jnp.zeros((), BF16)
            q2 = jnp.concatenate([jnp.where(low_half, qp, zero), jnp.where(low_half, zero, qp)], axis=0)
            s = lax.dot_general(q2, kpair, (((1,), (1,)), ((), ())), preferred_element_type=F32)
            first_head = pair * HEADS_PER_LANE_TILE
            bias = bias_ref[first_head:first_head + HEADS_PER_LANE_TILE, :, first_col:]
            s = s + bias.reshape(HEADS_PER_LANE_TILE * qb, nk - first_col)
            m = jnp.max(s, axis=-1, keepdims=True)
            p = jnp.exp(s - m)
            denom = jnp.sum(p, axis=-1, keepdims=True)
            o2 = jnp.dot(p.astype(BF16), vpair, preferred_element_type=F32) / denom
            outs.append(jnp.where(low_half, o2[:qb], o2[qb:]))
        o_ref[0, pl.ds(start, qb), :] = jnp.concatenate(outs, axis=-1).astype(BF16)

    n_blocks = seq // qb
    n_peeled = 0 if has_cache else min(ATT_PAST // qb, n_blocks)
    for i in range(n_peeled):
        q_block(i * qb, ATT_PAST - i * qb)

    n_steady = n_blocks - n_peeled
    unroll = 2 if n_steady % 2 == 0 and n_steady > 0 else 1

    def steady(i, carry):
        for u in range(unroll):
            q_block(pl.multiple_of((n_peeled + i * unroll + u) * qb, qb), 0)
        return carry

    lax.fori_loop(0, n_steady // unroll, steady, 0)


def _attention(q, k, v, bias, cache, qb):
    b, seq, _ = q.shape
    nk = ATT_PAST + qb
    has_cache = cache is not None
    width = MXU_DIM if seq > ATT_PAST else ATT_WIDTH
    hb = ATT_WIDTH // width
    seq_spec = pl.BlockSpec((1, seq, width), lambda bi, hi: (bi, 0, hi))
    in_specs = [seq_spec, seq_spec, seq_spec,
                pl.BlockSpec((width // HEAD_DIM_A, qb, nk), lambda bi, hi: (hi, 0, 0))]
    args = [q, k, v, bias]
    if has_cache:
        cspec = pl.BlockSpec((1, width, ATT_PAST), lambda bi, hi: (bi, hi, 0))
        in_specs += [cspec, cspec]
        args += list(cache)
    return pl.pallas_call(
        functools.partial(_attn_kernel, has_cache=has_cache, seq=seq, qb=qb, width=width),
        grid=(b, hb),
        in_specs=in_specs,
        out_specs=seq_spec,
        out_shape=jax.ShapeDtypeStruct((b, seq, ATT_WIDTH), BF16),
        scratch_shapes=[pltpu.VMEM((ATT_PAST + seq, width), BF16),
                        pltpu.VMEM((ATT_PAST + seq, width), BF16)],
        compiler_params=_params(("parallel", "parallel")),
        name="band_attention",
    )(*args)


def _attention_bias(table, qb):
    nk = ATT_PAST + qb
    n_diag = qb + nk - 1
    diag_rel = (ATT_PAST + qb - 1) - jnp.arange(n_diag)
    r = table[:, jnp.clip(diag_rel, -REL_CLIP, REL_CLIP) + REL_CLIP]
    h = table.shape[0]
    row_len = -(-(n_diag + 1) // LANES) * LANES
    rp = jnp.pad(r, ((0, 0), (1, row_len - n_diag)))
    skew = jnp.tile(rp, (1, qb))[:, :qb * row_len].reshape(h, qb, row_len)
    bias = skew[:, :, qb:qb + nk]
    qc = jnp.arange(qb)[:, None] // CHUNK
    kc = jnp.arange(nk)[None, :] // CHUNK
    visible = (kc >= qc) & (kc <= qc + ATT_PAST // CHUNK)
    return jnp.where(visible[None], bias, NEG_INF).astype(F32)


def _ffn_kernel(x_ref, ya_ref, yb_ref, gates_ref, pe_ref, ctx_ref, bg_ref, wa_ref, wb_ref, wo_ref,
                gffn_ref, wup_ref, cw_ref, cb_ref, wdown_ref, gple_ref, wgate_ref, wple_ref,
                o_ref, ctx_out, heads, *, tm, seq_rows, tiles_per_batch):
    i = pl.program_id(0)
    n_seqs = tm // seq_rows

    @pl.when(i % tiles_per_batch == 0)
    def _load_context():
        for k in range(FFN_CONV - 1):
            heads[k] = ctx_ref[:, k * SUBLANES:(k + 1) * SUBLANES, :]

    mix_gate = jax.nn.sigmoid(gates_ref[...].astype(F32) + bg_ref[...])
    mixed = (mix_gate[:, :D_MODEL] * jnp.dot(ya_ref[...], wa_ref[...], preferred_element_type=F32)
             + mix_gate[:, D_MODEL:] * jnp.dot(yb_ref[...], wb_ref[...], preferred_element_type=F32))
    x = x_ref[...] + jnp.dot(mixed.astype(BF16), wo_ref[...], preferred_element_type=F32)
    xn = (x * _rms_scale(x) * gffn_ref[...]).astype(BF16)
    n_slabs = tm // SUBLANES
    row_in_slab = lax.broadcasted_iota(jnp.int32, (tm, FFN_COL_BLOCK), 0) % SUBLANES

    def up(j, part):
        lo = part * D_FF + j * FFN_COL_BLOCK
        return jnp.dot(xn, wup_ref[:, lo:lo + FFN_COL_BLOCK], preferred_element_type=F32)

    def conv(u, j, part):
        lo = part * D_FF + j * FFN_COL_BLOCK
        cols = slice(lo, lo + FFN_COL_BLOCK)
        out = cb_ref[:, cols] + u * cw_ref[FFN_CONV - 1:FFN_CONV, cols]
        for k in range(1, FFN_CONV):
            rot = pltpu.roll(u.reshape(n_slabs, SUBLANES, FFN_COL_BLOCK), k, 1).reshape(tm, FFN_COL_BLOCK)
            pieces = []
            for s in range(n_seqs):
                pieces += [heads[k - 1, s, :, cols], rot[s * seq_rows:(s + 1) * seq_rows - SUBLANES]]
            shifted = jnp.where(row_in_slab < k, jnp.concatenate(pieces, axis=0), rot)
            out = out + shifted * cw_ref[FFN_CONV - 1 - k:FFN_CONV - k, cols]
            for s in range(n_seqs):
                heads[k - 1, s, :, cols] = rot[(s + 1) * seq_rows - SUBLANES:(s + 1) * seq_rows]
        return out

    n_blocks = D_FF // FFN_COL_BLOCK
    acc = jnp.zeros((tm, D_MODEL), F32)
    pending = (up(0, 0), up(0, 1))
    for j in range(n_blocks):
        ug, uv = pending
        if j + 1 < n_blocks:
            pending = (up(j + 1, 0), up(j + 1, 1))
        hidden = (jax.nn.gelu(conv(ug, j, 0), approximate=True) * conv(uv, j, 1)).astype(BF16)
        acc = acc + jnp.dot(hidden, wdown_ref[j * FFN_COL_BLOCK:(j + 1) * FFN_COL_BLOCK, :],
                            preferred_element_type=F32)

    @pl.when(i % tiles_per_batch == tiles_per_batch - 1)
    def _emit_context():
        ctx_out[...] = heads[FFN_CONV - 2]

    x2 = x + acc
    gate = jax.nn.sigmoid(jnp.dot((x2 * _rms_scale(x2) * gple_ref[...]).astype(BF16), wgate_ref[...],
                                  preferred_element_type=F32))
    o_ref[...] = x2 + gate * jnp.dot(pe_ref[...].astype(BF16), wple_ref[...], preferred_element_type=F32)


def _merge_ffn(x2d, ya, yb, gates, pe2d, ctx, prm, seq, tm):
    t = x2d.shape[0]
    batch = t // seq
    seq_rows = min(seq, tm)
    seqs_per_tile = tm // seq_rows
    tiles_per_batch = seq // seq_rows
    row = lambda w: pl.BlockSpec((tm, w), lambda i: (i, 0))
    per_seq = lambda r: pl.BlockSpec((seqs_per_tile, r, 2 * D_FF), lambda i: (i // tiles_per_batch, 0, 0))
    y, ctx_out = pl.pallas_call(
        functools.partial(_ffn_kernel, tm=tm, seq_rows=seq_rows, tiles_per_batch=tiles_per_batch),
        grid=(t // tm,),
        in_specs=[row(D_MODEL), row(ATT_WIDTH), row(D_INNER), row(2 * D_MODEL), row(PLE_DIM),
                  per_seq((FFN_CONV - 1) * SUBLANES)] + [_resident(p.shape) for p in prm],
        out_specs=[row(D_MODEL), per_seq(SUBLANES)],
        out_shape=[jax.ShapeDtypeStruct((t, D_MODEL), F32),
                   jax.ShapeDtypeStruct((batch, SUBLANES, 2 * D_FF), F32)],
        scratch_shapes=[pltpu.VMEM((FFN_CONV - 1, seqs_per_tile, SUBLANES, 2 * D_FF), F32)],
        compiler_params=_params(("arbitrary",)),
        name="merge_convffn_ple",
    )(x2d, ya, yb, gates, pe2d, _conv_heads(ctx, FFN_CONV), *prm)
    return y, ctx_out[:, :FFN_CONV - 1]


def _ssd_params(w):
    return (w["conv_ssm_w"], w["conv_ssm_b"], w["dt_bias"], w["a_log"], w["d_skip"], w["g_ssm"], w["expand"])


def _layer(x, pe, kv_cache, ssm_ctx, h0, ffn_ctx, w, tm, tq, tm_ffn):
    b, seq, _ = x.shape
    t = b * seq
    x2d = x.reshape(t, D_MODEL)
    as3 = lambda a: a.reshape(b, seq, -1)
    fuse_ssd = h0 is None and seq % tm == 0 and tm % CHUNK == 0
    if fuse_ssd:
        qn, kn, v, gates, k_tail, v_tail, yb, st, ctx_ssm = _inproj(x2d, w, tm, seq, ssm_ctx)
    else:
        qn, kn, v, gates, k_tail, v_tail, z, xbc, dt_raw = _inproj(x2d, w, tm, seq)
        seq_pad = -(-seq // CHUNK) * CHUNK
        valid_rows = CHUNK - (seq_pad - seq)
        pad = lambda a: jnp.pad(as3(a), ((0, 0), (0, seq_pad - seq), (0, 0)))
        yb, st, ctx_ssm = _ssd(pad(xbc), pad(dt_raw), pad(z), ssm_ctx, h0, _ssd_params(w), valid_rows)
        yb = yb[:, :seq].reshape(t, D_INNER)

    cache = None
    if kv_cache is not None:
        cache = tuple(jnp.transpose(c, (0, 2, 3, 1)).reshape(b, ATT_WIDTH, ATT_PAST) for c in kv_cache)
    ya = _attention(as3(qn), as3(kn), as3(v), _attention_bias(w["rel_bias"], tq), cache, tq)

    ffn_prm = (w["b_gate"], w["w_proj_a"], w["w_proj_b"], w["w_out"],
               w["g_ffn"], w["w_up"], w["ffn_conv_w"], w["ffn_conv_b"], w["w_down"],
               w["g_ple"], w["w_ple_gate"], w["w_ple"])
    y, ctx_ffn = _merge_ffn(x2d, ya.reshape(t, ATT_WIDTH), yb, gates, pe.reshape(t, PLE_DIM), ffn_ctx, ffn_prm,
                            seq, tm_ffn)

    def heads(tail_t):
        if seq >= tm:
            return jnp.transpose(tail_t.reshape(b, N_HEADS_A, HEAD_DIM_A, -1), (0, 3, 1, 2))
        return jnp.transpose(tail_t.reshape(N_HEADS_A, HEAD_DIM_A, b, seq), (2, 3, 0, 1))

    state = st.reshape(b, N_HEADS_S, SSM_HEAD_DIM, D_STATE)
    return y.reshape(b, seq, D_MODEL), (heads(k_tail), heads(v_tail), state, ctx_ssm[:, :SSM_CONV - 1], ctx_ffn)


def _prep_weights(i, g_mix, w_in, b_gate, g_q, g_k, rel_bias, conv_ssm_w, conv_ssm_b, dt_bias, a_log, d_skip,
                  g_ssm, w_proj_a, w_proj_b, w_out, g_ffn, w_up, ffn_conv_w, ffn_conv_b, w_down, g_ple,
                  w_ple_gate, w_ple):
    row = lambda a: a[i].reshape(1, -1).astype(F32)
    pad_lanes = lambda a, n: jnp.pad(a, ((0, 0), (0, n - a.shape[1])))
    wi = w_in[i].astype(BF16)
    head_of_lane = jnp.arange(D_INNER) // SSM_HEAD_DIM
    expand1 = (jnp.arange(DT_PAD)[:, None] == head_of_lane[None, :]).astype(BF16)
    blk = jnp.arange(MXU_DIM) // HEAD_DIM_A
    return {
        "g_mix": row(g_mix), "w_in": wi, "w_gates": wi[:, _SEG_GATES[0]:_SEG_GATES[1]],
        "w_dt": pad_lanes(wi[:, _SEG_DT[0]:_SEG_DT[1]], DT_PAD), "b_gate": row(b_gate),
        "g_q": jnp.tile(row(g_q), (1, N_HEADS_A)), "g_k": jnp.tile(row(g_k), (1, N_HEADS_A)),
        "hsum": (blk[:, None] == blk[None, :]).astype(BF16),
        "rel_bias": rel_bias[i].astype(F32),
        "conv_ssm_w": jnp.pad(conv_ssm_w[i], ((0, SUBLANES - SSM_CONV), (0, 0))), "conv_ssm_b": row(conv_ssm_b),
        "dt_bias": pad_lanes(row(dt_bias), DT_PAD), "a_log": pad_lanes(row(a_log), DT_PAD),
        "d_skip": jnp.repeat(row(d_skip), SSM_HEAD_DIM, axis=1), "g_ssm": row(g_ssm),
        "expand": jnp.concatenate([expand1] * 3, axis=0),
        "w_proj_a": w_proj_a[i].astype(BF16), "w_proj_b": w_proj_b[i].astype(BF16), "w_out": w_out[i].astype(BF16),
        "g_ffn": row(g_ffn), "w_up": w_up[i].astype(BF16),
        "ffn_conv_w": jnp.pad(ffn_conv_w[i], ((0, SUBLANES - FFN_CONV), (0, 0))), "ffn_conv_b": row(ffn_conv_b),
        "w_down": w_down[i].astype(BF16), "g_ple": row(g_ple),
        "w_ple_gate": w_ple_gate[i].astype(BF16), "w_ple": w_ple[i].astype(BF16),
    }


def kernel(x_prompt, x_sample, cache_k, cache_v, state_ssm, state_conv_ssm, state_conv_ffn, p_prompt, p_sample, g_mix, w_in, b_gate, g_q, g_k, rel_bias, conv_ssm_w, conv_ssm_b, dt_bias, a_log, d_skip, g_ssm, w_proj_a, w_proj_b, w_out, g_ffn, w_up, ffn_conv_w, ffn_conv_b, w_down, g_ple, w_ple_gate, w_ple):
    depth = w_in.shape[0]
    bp = x_prompt.shape[0]
    bs, seq_s = x_sample.shape[:2]
    yp, ys = x_prompt, x_sample
    sp, ss = [], []
    for i in range(depth):
        w = _prep_weights(i, g_mix, w_in, b_gate, g_q, g_k, rel_bias, conv_ssm_w, conv_ssm_b, dt_bias, a_log,
                          d_skip, g_ssm, w_proj_a, w_proj_b, w_out, g_ffn, w_up, ffn_conv_w, ffn_conv_b, w_down,
                          g_ple, w_ple_gate, w_ple)
        yp, st_p = _layer(yp, p_prompt[i], None,
                          jnp.zeros((bp, SSM_CONV - 1, CONV_DIM), F32), None,
                          jnp.zeros((bp, FFN_CONV - 1, 2 * D_FF), F32), w, tm=256, tq=4 * CHUNK, tm_ffn=256)
        h0 = state_ssm[i].reshape(bs, D_INNER, D_STATE)
        ys, st_s = _layer(ys, p_sample[i], (cache_k[i], cache_v[i]), state_conv_ssm[i], h0,
                          state_conv_ffn[i], w, tm=seq_s * bs, tq=seq_s, tm_ffn=seq_s * bs)
        sp.append(st_p)
        ss.append(st_s)

    stack = lambda lst, j: jnp.stack([s[j] for s in lst], axis=0)
    return (yp, ys,
            stack(sp, 0), stack(sp, 1), stack(sp, 2), stack(sp, 3), stack(sp, 4),
            stack(ss, 0), stack(ss, 1), stack(ss, 2), stack(ss, 3), stack(ss, 4))
```

```python
import functools

import jax
import jax.numpy as jnp
from jax import lax
from jax.experimental import pallas as pl
from jax.experimental.pallas import tpu as pltpu

F32 = jnp.float32
BF16 = jnp.bfloat16

D_MODEL = 1024
CHUNK = 64
PLE_DIM = 256
EPS = 1e-6
NEG_INF = -1e30
N_HEADS_A = 16
HEAD_DIM_A = 64
ATT_WIDTH = N_HEADS_A * HEAD_DIM_A
ATT_PAST = 8 * CHUNK
REL_CLIP = 128
D_INNER = 2 * D_MODEL
SSM_HEAD_DIM = 64
N_HEADS_S = D_INNER // SSM_HEAD_DIM
N_GROUPS = 4
GROUP_WIDTH = D_INNER // N_GROUPS
D_STATE = 128
SSM_CONV = 4
CONV_DIM = D_INNER + 2 * N_GROUPS * D_STATE
D_FF = 3 * D_MODEL
FFN_CONV = 3

LANES = 128
SUBLANES = 8
MXU_DIM = 256
VMEM_LIMIT_BYTES = 56 * 1024 * 1024

DT_PAD = LANES
HEADS_PER_MXU_TILE = MXU_DIM // SSM_HEAD_DIM
FFN_COL_BLOCK = 1024

_SEG_Q = (0, ATT_WIDTH)
_SEG_K = (ATT_WIDTH, 2 * ATT_WIDTH)
_SEG_V = (2 * ATT_WIDTH, 3 * ATT_WIDTH)
_SEG_Z = (3 * ATT_WIDTH, 3 * ATT_WIDTH + D_INNER)
_SEG_XBC = (_SEG_Z[1], _SEG_Z[1] + CONV_DIM)
_SEG_DT = (_SEG_XBC[1], _SEG_XBC[1] + N_HEADS_S)
_SEG_GATES = (_SEG_DT[1], _SEG_DT[1] + 2 * D_MODEL)


def _resident(shape):
    nd = len(shape)
    return pl.BlockSpec(shape, lambda *_: (0,) * nd, pipeline_mode=pl.Buffered(1))


def _params(semantics):
    return pltpu.CompilerParams(dimension_semantics=semantics, vmem_limit_bytes=VMEM_LIMIT_BYTES)


def _rms_scale(x):
    return lax.rsqrt(jnp.mean(x * x, axis=-1, keepdims=True) + EPS)


def _conv_heads(ctx, width):
    heads = [jnp.pad(ctx[:, width - 1 - k:], ((0, 0), (0, SUBLANES - k), (0, 0))) for k in range(1, width)]
    return jnp.concatenate(heads, axis=1)


def _softplus(x):
    return jnp.maximum(x, 0.0) + jnp.log(1.0 + jnp.exp(-jnp.abs(x)))


def _split3(v):
    hi = v.astype(BF16)
    r1 = v - hi.astype(F32)
    mid = r1.astype(BF16)
    lo = (r1 - mid.astype(F32)).astype(BF16)
    return jnp.concatenate([hi, mid, lo], axis=-1)


def _ssd_load_carry(ctx_ref, h0_ref, state, heads):
    for k in range(SSM_CONV - 1):
        heads[k] = ctx_ref[0, k * SUBLANES:(k + 1) * SUBLANES, :]
    for g in range(N_GROUPS):
        if h0_ref is None:
            state[g] = jnp.zeros((D_STATE, GROUP_WIDTH), F32)
        else:
            state[g] = h0_ref[0, g * GROUP_WIDTH:(g + 1) * GROUP_WIDTH, :].T


def _ssd_emit_carry(st_out, ctx_out, state, heads):
    ctx_out[0] = heads[SSM_CONV - 2]
    for g in range(N_GROUPS):
        st_out[0, g * GROUP_WIDTH:(g + 1) * GROUP_WIDTH, :] = state[g].T


def _ssd_conv(x_raw, prm, heads, valid_rows):
    cw_ref, cb_ref = prm[:2]
    n_rows = x_raw.shape[0]
    row_in_slab = lax.broadcasted_iota(jnp.int32, (n_rows, CONV_DIM), 0) % SUBLANES
    acc = cb_ref[...] + x_raw * cw_ref[SSM_CONV - 1:SSM_CONV, :]
    for k in range(1, SSM_CONV):
        rot = pltpu.roll(x_raw.reshape(n_rows // SUBLANES, SUBLANES, CONV_DIM), k, 1).reshape(n_rows, CONV_DIM)
        above = jnp.concatenate([heads[k - 1], rot[:n_rows - SUBLANES]], axis=0)
        acc = acc + jnp.where(row_in_slab < k, above, rot) * cw_ref[SSM_CONV - 1 - k:SSM_CONV - k, :]
        heads[k - 1] = rot[valid_rows - SUBLANES:valid_rows]
    return acc * jax.nn.sigmoid(acc)


def _ssd_steps(dt_raw, prm, valid_rows):
    dtb_ref, alog_ref, expand_ref = prm[2], prm[3], prm[6]
    n_rows = dt_raw.shape[0]
    dt = _softplus(dt_raw + dtb_ref[...])
    if valid_rows < n_rows:
        rows = lax.broadcasted_iota(jnp.int32, (n_rows, DT_PAD), 0)
        dt = jnp.where(rows < valid_rows, dt, 0.0)
    ac = dt * (-jnp.exp(alog_ref[...]))
    ri = lax.broadcasted_iota(jnp.int32, (n_rows, n_rows), 0)
    ci = lax.broadcasted_iota(jnp.int32, (n_rows, n_rows), 1)
    same_chunk_causal = jnp.logical_and(ri // CHUNK == ci // CHUNK, ri >= ci)
    acs = jnp.dot(same_chunk_causal.astype(F32), ac, preferred_element_type=F32,
                  precision=lax.Precision.HIGHEST)
    expand = lambda v: jnp.dot(_split3(v), expand_ref[...], preferred_element_type=F32)
    return expand(acs), expand(dt)


def _ssd_chunk(xc, acol, dtcol, z, prm, state):
    dskip_ref, gssm_ref = prm[4], prm[5]
    q = CHUNK

    xs = xc[:, :D_INNER]
    bm = xc[:, D_INNER:D_INNER + N_GROUPS * D_STATE]
    cm = xc[:, D_INNER + N_GROUPS * D_STATE:].astype(BF16)

    last = acol[q - 1:q, :]
    dx = dtcol * xs
    wj = (jnp.exp(last - acol) * dx).astype(BF16)
    ea = jnp.exp(acol)

    lane = lax.broadcasted_iota(jnp.int32, (q, D_INNER), 1)
    rowi = lax.broadcasted_iota(jnp.int32, (q, D_INNER), 0)
    arow = jnp.sum(jnp.where(lane % q == rowi, acol, 0.0), axis=0, keepdims=True)

    tri = (lax.broadcasted_iota(jnp.int32, (q, MXU_DIM), 0)
           >= lax.broadcasted_iota(jnp.int32, (q, MXU_DIM), 1) % q)
    br = lax.broadcasted_iota(jnp.int32, (MXU_DIM, MXU_DIM), 0) // SSM_HEAD_DIM
    bc = lax.broadcasted_iota(jnp.int32, (MXU_DIM, MXU_DIM), 1) // SSM_HEAD_DIM
    blockdiag = br == bc

    y_blocks = []
    for g in range(N_GROUPS):
        cg = cm[:, g * D_STATE:(g + 1) * D_STATE]
        bg = bm[:, g * D_STATE:(g + 1) * D_STATE].astype(BF16)
        bg4 = jnp.concatenate([bg] * HEADS_PER_MXU_TILE, axis=0)
        cb4 = lax.dot_general(cg, bg4, (((1,), (1,)), ((), ())), preferred_element_type=F32)
        sg = state[g].astype(BF16)
        for half in range(GROUP_WIDTH // MXU_DIM):
            lo = g * GROUP_WIDTH + half * MXU_DIM
            cols = slice(lo, lo + MXU_DIM)
            seg = acol[:, cols] - arow[:, cols]
            decay = jnp.where(tri, jnp.exp(jnp.where(tri, seg, 0.0)), 0.0)
            mp = (cb4 * decay).astype(BF16)
            dxb = dx[:, cols].astype(BF16)
            bd = jnp.where(blockdiag, jnp.concatenate([dxb] * HEADS_PER_MXU_TILE, axis=0),
                           jnp.zeros((), BF16))
            y = jnp.dot(mp, bd, preferred_element_type=F32)
            y = y + jnp.dot(cg, sg[:, half * MXU_DIM:(half + 1) * MXU_DIM],
                            preferred_element_type=F32) * ea[:, cols]
            y_blocks.append(y + dskip_ref[:, cols] * xs[:, cols])

    el = jnp.exp(last)
    for g in range(N_GROUPS):
        cols = slice(g * GROUP_WIDTH, (g + 1) * GROUP_WIDTH)
        bgt = bm[:, g * D_STATE:(g + 1) * D_STATE].T.astype(BF16)
        state[g] = state[g] * el[:, cols] + jnp.dot(bgt, wj[:, cols], preferred_element_type=F32)

    zf = z.astype(F32)
    yg = jnp.concatenate(y_blocks, axis=-1) * (zf * jax.nn.sigmoid(zf))
    return (yg * _rms_scale(yg) * gssm_ref[...]).astype(BF16)


def _ssd_scratch():
    return [pltpu.VMEM((N_GROUPS, D_STATE, GROUP_WIDTH), F32),
            pltpu.VMEM((SSM_CONV - 1, SUBLANES, CONV_DIM), F32)]


def _ssd_kernel(*refs, has_state, valid_rows, n_chunks):
    if has_state:
        xbc_ref, dt_ref, z_ref, ctx_ref, h0_ref = refs[:5]
        refs = refs[5:]
    else:
        xbc_ref, dt_ref, z_ref, ctx_ref = refs[:4]
        h0_ref = None
        refs = refs[4:]
    prm, (yb_out, st_out, ctx_out, state, heads) = refs[:7], refs[7:]
    c = pl.program_id(1)

    @pl.when(c == 0)
    def _load_carry():
        _ssd_load_carry(ctx_ref, h0_ref, state, heads)

    xc = _ssd_conv(xbc_ref[0], prm, heads, valid_rows)
    acol, dtcol = _ssd_steps(dt_ref[0], prm, valid_rows)
    yb_out[0] = _ssd_chunk(xc, acol, dtcol, z_ref[0], prm, state)

    @pl.when(c == n_chunks - 1)
    def _emit_carry():
        _ssd_emit_carry(st_out, ctx_out, state, heads)


def _ssd(xbc, dt_raw, z, ctx, h0, prm, valid_rows):
    b, seq, _ = xbc.shape
    n_chunks = seq // CHUNK
    has_state = h0 is not None
    chunk = lambda w: pl.BlockSpec((1, CHUNK, w), lambda bi, ci: (bi, ci, 0))
    per_batch = lambda r, w: pl.BlockSpec((1, r, w), lambda bi, ci: (bi, 0, 0))
    in_specs = [chunk(CONV_DIM), chunk(DT_PAD), chunk(D_INNER), per_batch((SSM_CONV - 1) * SUBLANES, CONV_DIM)]
    args = [xbc, dt_raw, z, _conv_heads(ctx, SSM_CONV)]
    if has_state:
        in_specs.append(per_batch(D_INNER, D_STATE))
        args.append(h0)
    in_specs += [_resident(p.shape) for p in prm]
    args += list(prm)
    return pl.pallas_call(
        functools.partial(_ssd_kernel, has_state=has_state, valid_rows=valid_rows, n_chunks=n_chunks),
        grid=(b, n_chunks),
        in_specs=in_specs,
        out_specs=[chunk(D_INNER), per_batch(D_INNER, D_STATE), per_batch(SUBLANES, CONV_DIM)],
        out_shape=[jax.ShapeDtypeStruct((b, seq, D_INNER), BF16),
                   jax.ShapeDtypeStruct((b, D_INNER, D_STATE), F32),
                   jax.ShapeDtypeStruct((b, SUBLANES, CONV_DIM), F32)],
        scratch_shapes=_ssd_scratch(),
        compiler_params=_params(("parallel", "arbitrary")),
        name="ssd",
    )(*args)


def _inproj_kernel(*refs, tiles_per_batch, tail_tiles, fuse_ssd, tm):
    x_ref, g_ref, w_ref, wgates_ref, wdt_ref, gq_ref, gk_ref, hsum_ref = refs[:8]
    if fuse_ssd:
        ctx_ref, prm = refs[8], refs[9:16]
        q_out, k_out, v_out, gates_out, ktail_out, vtail_out, yb_out, st_out, ctx_out, state, heads = refs[16:]
    else:
        q_out, k_out, v_out, gates_out, ktail_out, vtail_out, z_out, xbc_out, dt_out = refs[8:]
    tile_in_batch = pl.program_id(0) % tiles_per_batch

    if fuse_ssd:
        @pl.when(tile_in_batch == 0)
        def _load_carry():
            _ssd_load_carry(ctx_ref, None, state, heads)

    x = x_ref[...]
    h = (x * _rms_scale(x) * g_ref[...]).astype(BF16)

    def seg(bounds):
        return jnp.dot(h, w_ref[:, bounds[0]:bounds[1]], preferred_element_type=F32)

    def head_norm(t, g):
        sq = (t * t).astype(BF16)
        parts = []
        for blk in range(ATT_WIDTH // MXU_DIM):
            cols = slice(blk * MXU_DIM, (blk + 1) * MXU_DIM)
            ss = jnp.dot(sq[:, cols], hsum_ref[...], preferred_element_type=F32)
            parts.append(t[:, cols] * lax.rsqrt(ss * (1.0 / HEAD_DIM_A) + EPS))
        return jnp.concatenate(parts, axis=-1) * g

    xbc = seg(_SEG_XBC)
    zb = seg(_SEG_Z).astype(BF16)
    dt_raw = jnp.dot(h, wdt_ref[...], preferred_element_type=F32)
    if fuse_ssd:
        xc = _ssd_conv(xbc, prm, heads, tm)
        acol, dtcol = _ssd_steps(dt_raw, prm, tm)
        for c in range(tm // CHUNK):
            rows = slice(c * CHUNK, (c + 1) * CHUNK)
            yb_out[rows, :] = _ssd_chunk(xc[rows, :], acol[rows, :], dtcol[rows, :], zb[rows, :], prm, state)
    else:
        xbc_out[...] = xbc
        z_out[...] = zb
        dt_out[...] = dt_raw

    q_out[...] = (head_norm(seg(_SEG_Q), gq_ref[...]) * (HEAD_DIM_A ** -0.5)).astype(BF16)
    k = head_norm(seg(_SEG_K), gk_ref[...])
    v = seg(_SEG_V)
    k_out[...] = k.astype(BF16)
    v_out[...] = v.astype(BF16)
    gates_out[...] = jnp.dot(h, wgates_ref[...], preferred_element_type=F32).astype(BF16)

    @pl.when(tile_in_batch >= tiles_per_batch - tail_tiles)
    def _emit_tail():
        ktail_out[0] = k.T
        vtail_out[0] = v.T

    if fuse_ssd:
        @pl.when(tile_in_batch == tiles_per_batch - 1)
        def _emit_carry():
            _ssd_emit_carry(st_out, ctx_out, state, heads)


def _inproj(x2d, w, tm, seq, ssm_ctx=None):
    t = x2d.shape[0]
    fuse_ssd = ssm_ctx is not None
    tiles_per_batch = max(seq // tm, 1)
    tail_tiles = max(min(ATT_PAST, seq) // tm, 1)
    n_tiles = t // tm
    n_batches = n_tiles // tiles_per_batch
    row = lambda width: pl.BlockSpec((tm, width), lambda i: (i, 0))
    first_tail = tiles_per_batch - tail_tiles
    tail = pl.BlockSpec((1, ATT_WIDTH, tm), lambda i: (
        i // tiles_per_batch, 0, jnp.maximum(i % tiles_per_batch - first_tail, 0)))
    per_batch = lambda r, width: pl.BlockSpec((1, r, width), lambda i: (i // tiles_per_batch, 0, 0))
    rows_out = lambda width, dtype: (row(width), jax.ShapeDtypeStruct((t, width), dtype))
    tail_out = (tail, jax.ShapeDtypeStruct((n_batches, ATT_WIDTH, tail_tiles * tm), F32))

    in_specs = [row(D_MODEL), _resident((1, D_MODEL)), _resident((D_MODEL, _SEG_XBC[1])),
                _resident(w["w_gates"].shape), _resident(w["w_dt"].shape),
                _resident((1, ATT_WIDTH)), _resident((1, ATT_WIDTH)), _resident((MXU_DIM, MXU_DIM))]
    args = [x2d, w["g_mix"], w["w_in"], w["w_gates"], w["w_dt"], w["g_q"], w["g_k"], w["hsum"]]
    outs = [rows_out(ATT_WIDTH, BF16), rows_out(ATT_WIDTH, BF16), rows_out(ATT_WIDTH, BF16),
            rows_out(2 * D_MODEL, BF16), tail_out, tail_out]
    scratch = []
    if fuse_ssd:
        assert seq % tm == 0 and tm % CHUNK == 0
        prm = _ssd_params(w)
        in_specs += [per_batch((SSM_CONV - 1) * SUBLANES, CONV_DIM)] + [_resident(p.shape) for p in prm]
        args += [_conv_heads(ssm_ctx, SSM_CONV)] + list(prm)
        outs += [rows_out(D_INNER, BF16),
                 (per_batch(D_INNER, D_STATE), jax.ShapeDtypeStruct((n_batches, D_INNER, D_STATE), F32)),
                 (per_batch(SUBLANES, CONV_DIM), jax.ShapeDtypeStruct((n_batches, SUBLANES, CONV_DIM), F32))]
        scratch = _ssd_scratch()
    else:
        outs += [rows_out(D_INNER, BF16), rows_out(CONV_DIM, F32), rows_out(DT_PAD, F32)]
    return pl.pallas_call(
        functools.partial(_inproj_kernel, tiles_per_batch=tiles_per_batch, tail_tiles=tail_tiles,
                          fuse_ssd=fuse_ssd, tm=tm),
        grid=(n_tiles,),
        in_specs=in_specs,
        out_specs=[o[0] for o in outs],
        out_shape=[o[1] for o in outs],
        scratch_shapes=scratch,
        compiler_params=_params(("arbitrary",)),
        name="inproj_ssd" if fuse_ssd else "inproj",
    )(*args)


HEADS_PER_LANE_TILE = LANES // HEAD_DIM_A


def _attn_kernel(*refs, has_cache, seq, qb, width):
    if has_cache:
        q_ref, k_ref, v_ref, bias_ref, ck_ref, cv_ref, o_ref, kp, vp = refs
    else:
        q_ref, k_ref, v_ref, bias_ref, o_ref, kp, vp = refs
    nk = ATT_PAST + qb

    if has_cache:
        kp[0:ATT_PAST, :] = ck_ref[0].T.astype(BF16)
        vp[0:ATT_PAST, :] = cv_ref[0].T.astype(BF16)
    kp[ATT_PAST:ATT_PAST + seq, :] = k_ref[0]
    vp[ATT_PAST:ATT_PAST + seq, :] = v_ref[0]

    lane = lax.broadcasted_iota(jnp.int32, (qb, LANES), 1)
    low_half = lane < HEAD_DIM_A

    def q_block(start, first_col):
        kb = kp[pl.ds(start + first_col, nk - first_col), :]
        vb = vp[pl.ds(start + first_col, nk - first_col), :]
        q = q_ref[0, pl.ds(start, qb), :]
        outs = []
        for pair in range(width // LANES):
            cols = slice(pair * LANES, (pair + 1) * LANES)
            qp, kpair, vpair = q[:, cols], kb[:, cols], vb[:, cols]
            zero = jnp.zeros((), BF16)
            q2 = jnp.concatenate([jnp.where(low_half, qp, zero), jnp.where(low_half, zero, qp)], axis=0)
            s = lax.dot_general(q2, kpair, (((1,), (1,)), ((), ())), preferred_element_type=F32)
            first_head = pair * HEADS_PER_LANE_TILE
            bias = bias_ref[first_head:first_head + HEADS_PER_LANE_TILE, :, first_col:]
            s = s + bias.reshape(HEADS_PER_LANE_TILE * qb, nk - first_col)
            m = jnp.max(s, axis=-1, keepdims=True)
            p = jnp.exp(s - m)
            denom = jnp.sum(p, axis=-1, keepdims=True)
            o2 = jnp.dot(p.astype(BF16), vpair, preferred_element_type=F32) / denom
            outs.append(jnp.where(low_half, o2[:qb], o2[qb:]))
        o_ref[0, pl.ds(start, qb), :] = jnp.concatenate(outs, axis=-1).astype(BF16)

    n_blocks = seq // qb
    n_peeled = 0 if has_cache else min(ATT_PAST // qb, n_blocks)
    for i in range(n_peeled):
        q_block(i * qb, ATT_PAST - i * qb)

    n_steady = n_blocks - n_peeled
    unroll = 6 if n_steady % 6 == 0 and n_steady > 0 else 1

    def steady(i, carry):
        for u in range(unroll):
            q_block(pl.multiple_of((n_peeled + i * unroll + u) * qb, qb), 0)
        return carry

    lax.fori_loop(0, n_steady // unroll, steady, 0)


def _attention(q, k, v, bias, cache, qb):
    b, seq, _ = q.shape
    nk = ATT_PAST + qb
    has_cache = cache is not None
    width = MXU_DIM if seq > ATT_PAST else ATT_WIDTH
    hb = ATT_WIDTH // width
    seq_spec = pl.BlockSpec((1, seq, width), lambda bi, hi: (bi, 0, hi))
    in_specs = [seq_spec, seq_spec, seq_spec,
                pl.BlockSpec((width // HEAD_DIM_A, qb, nk), lambda bi, hi: (hi, 0, 0))]
    args = [q, k, v, bias]
    if has_cache:
        cspec = pl.BlockSpec((1, width, ATT_PAST), lambda bi, hi: (bi, hi, 0))
        in_specs += [cspec, cspec]
        args += list(cache)
    return pl.pallas_call(
        functools.partial(_attn_kernel, has_cache=has_cache, seq=seq, qb=qb, width=width),
        grid=(b, hb),
        in_specs=in_specs,
        out_specs=seq_spec,
        out_shape=jax.ShapeDtypeStruct((b, seq, ATT_WIDTH), BF16),
        scratch_shapes=[pltpu.VMEM((ATT_PAST + seq, width), BF16),
                        pltpu.VMEM((ATT_PAST + seq, width), BF16)],
        compiler_params=_params(("parallel", "parallel")),
        name="band_attention",
    )(*args)


def _attention_bias(table, qb):
    nk = ATT_PAST + qb
    n_diag = qb + nk - 1
    diag_rel = (ATT_PAST + qb - 1) - jnp.arange(n_diag)
    r = table[:, jnp.clip(diag_rel, -REL_CLIP, REL_CLIP) + REL_CLIP]
    h = table.shape[0]
    row_len = 1 << n_diag.bit_length()
    rp = jnp.roll(jnp.pad(r, ((0, 0), (0, row_len - n_diag))), -(qb - 1), axis=1).reshape(h, 1, row_len)

    def bias_kernel(rp_ref, o_ref):
        rolled = pltpu.roll(jnp.broadcast_to(rp_ref[0], (qb, row_len)), 0, 1, stride=1, stride_axis=0)
        qc = lax.broadcasted_iota(jnp.int32, (qb, nk), 0) // CHUNK
        kc = lax.broadcasted_iota(jnp.int32, (qb, nk), 1) // CHUNK
        visible = jnp.logical_and(kc >= qc, kc <= qc + ATT_PAST // CHUNK)
        o_ref[0] = jnp.where(visible, rolled[:, :nk], NEG_INF)

    return pl.pallas_call(
        bias_kernel,
        grid=(h,),
        in_specs=[pl.BlockSpec((1, 1, row_len), lambda hi: (hi, 0, 0))],
        out_specs=pl.BlockSpec((1, qb, nk), lambda hi: (hi, 0, 0)),
        out_shape=jax.ShapeDtypeStruct((h, qb, nk), F32),
        compiler_params=_params(("parallel",)),
        name="attention_bias",
    )(rp.astype(F32))


def _ffn_kernel(x_ref, ya_ref, yb_ref, gates_ref, pe_ref, ctx_ref, bg_ref, wa_ref, wb_ref, wo_ref,
                gffn_ref, wup_ref, cw_ref, cb_ref, wdown_ref, gple_ref, wgate_ref, wple_ref,
                o_ref, ctx_out, heads, *, tm, seq_rows, tiles_per_batch):
    i = pl.program_id(0)
    n_seqs = tm // seq_rows

    @pl.when(i % tiles_per_batch == 0)
    def _load_context():
        for k in range(FFN_CONV - 1):
            heads[k] = ctx_ref[:, k * SUBLANES:(k + 1) * SUBLANES, :]

    mix_gate = jax.nn.sigmoid(gates_ref[...].astype(F32) + bg_ref[...])
    mixed = (mix_gate[:, :D_MODEL] * jnp.dot(ya_ref[...], wa_ref[...], preferred_element_type=F32)
             + mix_gate[:, D_MODEL:] * jnp.dot(yb_ref[...], wb_ref[...], preferred_element_type=F32))
    x = x_ref[...] + jnp.dot(mixed.astype(BF16), wo_ref[...], preferred_element_type=F32)
    xn = (x * _rms_scale(x) * gffn_ref[...]).astype(BF16)
    n_slabs = tm // SUBLANES
    row_in_slab = lax.broadcasted_iota(jnp.int32, (tm, FFN_COL_BLOCK), 0) % SUBLANES

    def up(j, part):
        lo = part * D_FF + j * FFN_COL_BLOCK
        return jnp.dot(xn, wup_ref[:, lo:lo + FFN_COL_BLOCK], preferred_element_type=F32)

    def conv(u, j, part):
        lo = part * D_FF + j * FFN_COL_BLOCK
        cols = slice(lo, lo + FFN_COL_BLOCK)
        out = cb_ref[:, cols] + u * cw_ref[FFN_CONV - 1:FFN_CONV, cols]
        for k in range(1, FFN_CONV):
            rot = pltpu.roll(u.reshape(n_slabs, SUBLANES, FFN_COL_BLOCK), k, 1).reshape(tm, FFN_COL_BLOCK)
            pieces = []
            for s in range(n_seqs):
                pieces += [heads[k - 1, s, :, cols], rot[s * seq_rows:(s + 1) * seq_rows - SUBLANES]]
            shifted = jnp.where(row_in_slab < k, jnp.concatenate(pieces, axis=0), rot)
            out = out + shifted * cw_ref[FFN_CONV - 1 - k:FFN_CONV - k, cols]
            for s in range(n_seqs):
                heads[k - 1, s, :, cols] = rot[(s + 1) * seq_rows - SUBLANES:(s + 1) * seq_rows]
        return out

    n_blocks = D_FF // FFN_COL_BLOCK
    acc = jnp.zeros((tm, D_MODEL), F32)
    pending = (up(0, 0), up(0, 1))
    for j in range(n_blocks):
        ug, uv = pending
        if j + 1 < n_blocks:
            pending = (up(j + 1, 0), up(j + 1, 1))
        hidden = (jax.nn.gelu(conv(ug, j, 0), approximate=True) * conv(uv, j, 1)).astype(BF16)
        acc = acc + jnp.dot(hidden, wdown_ref[j * FFN_COL_BLOCK:(j + 1) * FFN_COL_BLOCK, :],
                            preferred_element_type=F32)

    @pl.when(i % tiles_per_batch == tiles_per_batch - 1)
    def _emit_context():
        ctx_out[...] = heads[FFN_CONV - 2]

    x2 = x + acc
    gate = jax.nn.sigmoid(jnp.dot((x2 * _rms_scale(x2) * gple_ref[...]).astype(BF16), wgate_ref[...],
                                  preferred_element_type=F32))
    o_ref[...] = x2 + gate * jnp.dot(pe_ref[...].astype(BF16), wple_ref[...], preferred_element_type=F32)


def _merge_ffn(x2d, ya, yb, gates, pe2d, ctx, prm, seq, tm):
    t = x2d.shape[0]
    batch = t // seq
    seq_rows = min(seq, tm)
    seqs_per_tile = tm // seq_rows
    tiles_per_batch = seq // seq_rows
    row = lambda w: pl.BlockSpec((tm, w), lambda i: (i, 0))
    per_seq = lambda r: pl.BlockSpec((seqs_per_tile, r, 2 * D_FF), lambda i: (i // tiles_per_batch, 0, 0))
    y, ctx_out = pl.pallas_call(
        functools.partial(_ffn_kernel, tm=tm, seq_rows=seq_rows, tiles_per_batch=tiles_per_batch),
        grid=(t // tm,),
        in_specs=[row(D_MODEL), row(ATT_WIDTH), row(D_INNER), row(2 * D_MODEL), row(PLE_DIM),
                  per_seq((FFN_CONV - 1) * SUBLANES)] + [_resident(p.shape) for p in prm],
        out_specs=[row(D_MODEL), per_seq(SUBLANES)],
        out_shape=[jax.ShapeDtypeStruct((t, D_MODEL), F32),
                   jax.ShapeDtypeStruct((batch, SUBLANES, 2 * D_FF), F32)],
        scratch_shapes=[pltpu.VMEM((FFN_CONV - 1, seqs_per_tile, SUBLANES, 2 * D_FF), F32)],
        compiler_params=_params(("arbitrary",)),
        name="merge_convffn_ple",
    )(x2d, ya, yb, gates, pe2d, _conv_heads(ctx, FFN_CONV), *prm)
    return y, ctx_out[:, :FFN_CONV - 1]


def _ssd_params(w):
    return (w["conv_ssm_w"], w["conv_ssm_b"], w["dt_bias"], w["a_log"], w["d_skip"], w["g_ssm"], w["expand"])


def _layer(x, pe, kv_cache, ssm_ctx, h0, ffn_ctx, w, tm, tq, tm_ffn):
    b, seq, _ = x.shape
    t = b * seq
    x2d = x.reshape(t, D_MODEL)
    as3 = lambda a: a.reshape(b, seq, -1)
    fuse_ssd = h0 is None and seq % tm == 0 and tm % CHUNK == 0
    if fuse_ssd:
        qn, kn, v, gates, k_tail, v_tail, yb, st, ctx_ssm = _inproj(x2d, w, tm, seq, ssm_ctx)
    else:
        qn, kn, v, gates, k_tail, v_tail, z, xbc, dt_raw = _inproj(x2d, w, tm, seq)
        seq_pad = -(-seq // CHUNK) * CHUNK
        valid_rows = CHUNK - (seq_pad - seq)
        pad = lambda a: jnp.pad(as3(a), ((0, 0), (0, seq_pad - seq), (0, 0)))
        yb, st, ctx_ssm = _ssd(pad(xbc), pad(dt_raw), pad(z), ssm_ctx, h0, _ssd_params(w), valid_rows)
        yb = yb[:, :seq].reshape(t, D_INNER)

    cache = None
    if kv_cache is not None:
        cache = tuple(jnp.transpose(c, (0, 2, 3, 1)).reshape(b, ATT_WIDTH, ATT_PAST) for c in kv_cache)
    ya = _attention(as3(qn), as3(kn), as3(v), _attention_bias(w["rel_bias"], tq), cache, tq)

    ffn_prm = (w["b_gate"], w["w_proj_a"], w["w_proj_b"], w["w_out"],
               w["g_ffn"], w["w_up"], w["ffn_conv_w"], w["ffn_conv_b"], w["w_down"],
               w["g_ple"], w["w_ple_gate"], w["w_ple"])
    y, ctx_ffn = _merge_ffn(x2d, ya.reshape(t, ATT_WIDTH), yb, gates, pe.reshape(t, PLE_DIM), ffn_ctx, ffn_prm,
                            seq, tm_ffn)

    def heads(tail_t):
        if seq >= tm:
            return jnp.transpose(tail_t.reshape(b, N_HEADS_A, HEAD_DIM_A, -1), (0, 3, 1, 2))
        return jnp.transpose(tail_t.reshape(N_HEADS_A, HEAD_DIM_A, b, seq), (2, 3, 0, 1))

    state = st.reshape(b, N_HEADS_S, SSM_HEAD_DIM, D_STATE)
    return y.reshape(b, seq, D_MODEL), (heads(k_tail), heads(v_tail), state, ctx_ssm[:, :SSM_CONV - 1], ctx_ffn)


def _prep_weights(i, g_mix, w_in, b_gate, g_q, g_k, rel_bias, conv_ssm_w, conv_ssm_b, dt_bias, a_log, d_skip,
                  g_ssm, w_proj_a, w_proj_b, w_out, g_ffn, w_up, ffn_conv_w, ffn_conv_b, w_down, g_ple,
                  w_ple_gate, w_ple):
    row = lambda a: a[i].reshape(1, -1).astype(F32)
    pad_lanes = lambda a, n: jnp.pad(a, ((0, 0), (0, n - a.shape[1])))
    wi = w_in[i].astype(BF16)
    head_of_lane = jnp.arange(D_INNER) // SSM_HEAD_DIM
    expand1 = (jnp.arange(DT_PAD)[:, None] == head_of_lane[None, :]).astype(BF16)
    blk = jnp.arange(MXU_DIM) // HEAD_DIM_A
    return {
        "g_mix": row(g_mix), "w_in": wi, "w_gates": wi[:, _SEG_GATES[0]:_SEG_GATES[1]],
        "w_dt": pad_lanes(wi[:, _SEG_DT[0]:_SEG_DT[1]], DT_PAD), "b_gate": row(b_gate),
        "g_q": jnp.tile(row(g_q), (1, N_HEADS_A)), "g_k": jnp.tile(row(g_k), (1, N_HEADS_A)),
        "hsum": (blk[:, None] == blk[None, :]).astype(BF16),
        "rel_bias": rel_bias[i].astype(F32),
        "conv_ssm_w": jnp.pad(conv_ssm_w[i], ((0, SUBLANES - SSM_CONV), (0, 0))), "conv_ssm_b": row(conv_ssm_b),
        "dt_bias": pad_lanes(row(dt_bias), DT_PAD), "a_log": pad_lanes(row(a_log), DT_PAD),
        "d_skip": jnp.repeat(row(d_skip), SSM_HEAD_DIM, axis=1), "g_ssm": row(g_ssm),
        "expand": jnp.concatenate([expand1] * 3, axis=0),
        "w_proj_a": w_proj_a[i].astype(BF16), "w_proj_b": w_proj_b[i].astype(BF16), "w_out": w_out[i].astype(BF16),
        "g_ffn": row(g_ffn), "w_up": w_up[i].astype(BF16),
        "ffn_conv_w": jnp.pad(ffn_conv_w[i], ((0, SUBLANES - FFN_CONV), (0, 0))), "ffn_conv_b": row(ffn_conv_b),
        "w_down": w_down[i].astype(BF16), "g_ple": row(g_ple),
        "w_ple_gate": w_ple_gate[i].astype(BF16), "w_ple": w_ple[i].astype(BF16),
    }


def kernel(x_prompt, x_sample, cache_k, cache_v, state_ssm, state_conv_ssm, state_conv_ffn, p_prompt, p_sample, g_mix, w_in, b_gate, g_q, g_k, rel_bias, conv_ssm_w, conv_ssm_b, dt_bias, a_log, d_skip, g_ssm, w_proj_a, w_proj_b, w_out, g_ffn, w_up, ffn_conv_w, ffn_conv_b, w_down, g_ple, w_ple_gate, w_ple):
    depth = w_in.shape[0]
    bp = x_prompt.shape[0]
    bs, seq_s = x_sample.shape[:2]
    yp, ys = x_prompt, x_sample
    sp, ss = [], []
    for i in range(depth):
        w = _prep_weights(i, g_mix, w_in, b_gate, g_q, g_k, rel_bias, conv_ssm_w, conv_ssm_b, dt_bias, a_log,
                          d_skip, g_ssm, w_proj_a, w_proj_b, w_out, g_ffn, w_up, ffn_conv_w, ffn_conv_b, w_down,
                          g_ple, w_ple_gate, w_ple)
        yp, st_p = _layer(yp, p_prompt[i], None,
                          jnp.zeros((bp, SSM_CONV - 1, CONV_DIM), F32), None,
                          jnp.zeros((bp, FFN_CONV - 1, 2 * D_FF), F32), w, tm=256, tq=4 * CHUNK, tm_ffn=256)
        h0 = state_ssm[i].reshape(bs, D_INNER, D_STATE)
        ys, st_s = _layer(ys, p_sample[i], (cache_k[i], cache_v[i]), state_conv_ssm[i], h0,
                          state_conv_ffn[i], w, tm=seq_s * bs, tq=seq_s, tm_ffn=seq_s * bs)
        sp.append(st_p)
        ss.append(st_s)

    stack = lambda lst, j: jnp.stack([s[j] for s in lst], axis=0)
    return (yp, ys,
            stack(sp, 0), stack(sp, 1), stack(sp, 2), stack(sp, 3), stack(sp, 4),
            stack(ss, 0), stack(ss, 1), stack(ss, 2), stack(ss, 3), stack(ss, 4))
```

```python
import functools

import jax
import jax.numpy as jnp
from jax import lax
from jax.experimental import pallas as pl
from jax.experimental.pallas import tpu as pltpu

F32 = jnp.float32
BF16 = jnp.bfloat16

D_MODEL = 1024
CHUNK = 64
PLE_DIM = 256
EPS = 1e-6
NEG_INF = -1e30
N_HEADS_A = 16
HEAD_DIM_A = 64
ATT_WIDTH = N_HEADS_A * HEAD_DIM_A
ATT_PAST = 8 * CHUNK
REL_CLIP = 128
D_INNER = 2 * D_MODEL
SSM_HEAD_DIM = 64
N_HEADS_S = D_INNER // SSM_HEAD_DIM
N_GROUPS = 4
GROUP_WIDTH = D_INNER // N_GROUPS
D_STATE = 128
SSM_CONV = 4
CONV_DIM = D_INNER + 2 * N_GROUPS * D_STATE
D_FF = 3 * D_MODEL
FFN_CONV = 3

LANES = 128
SUBLANES = 8
MXU_DIM = 256
VMEM_LIMIT_BYTES = 56 * 1024 * 1024

DT_PAD = LANES
HEADS_PER_MXU_TILE = MXU_DIM // SSM_HEAD_DIM
FFN_COL_BLOCK = 1024
GELU_C0 = (2.0 / 3.141592653589793) ** 0.5
GELU_C1 = GELU_C0 * 0.044715

_SEG_Q = (0, ATT_WIDTH)
_SEG_K = (ATT_WIDTH, 2 * ATT_WIDTH)
_SEG_V = (2 * ATT_WIDTH, 3 * ATT_WIDTH)
_SEG_Z = (3 * ATT_WIDTH, 3 * ATT_WIDTH + D_INNER)
_SEG_XBC = (_SEG_Z[1], _SEG_Z[1] + CONV_DIM)
_SEG_DT = (_SEG_XBC[1], _SEG_XBC[1] + N_HEADS_S)
_SEG_GATES = (_SEG_DT[1], _SEG_DT[1] + 2 * D_MODEL)


def _resident(shape):
    nd = len(shape)
    return pl.BlockSpec(shape, lambda *_: (0,) * nd, pipeline_mode=pl.Buffered(1))


def _params(semantics):
    return pltpu.CompilerParams(dimension_semantics=semantics, vmem_limit_bytes=VMEM_LIMIT_BYTES)


def _rms_scale(x):
    return lax.rsqrt(jnp.mean(x * x, axis=-1, keepdims=True) + EPS)


def _silu(x):
    half = 0.5 * x
    return half * (1.0 + jnp.tanh(half))


def _conv_heads(ctx, width):
    heads = [jnp.pad(ctx[:, width - 1 - k:], ((0, 0), (0, SUBLANES - k), (0, 0))) for k in range(1, width)]
    return jnp.concatenate(heads, axis=1)


def _softplus(x):
    return jnp.maximum(x, 0.0) + jnp.log(1.0 + jnp.exp(-jnp.abs(x)))


def _split3(v):
    hi = v.astype(BF16)
    r1 = v - hi.astype(F32)
    mid = r1.astype(BF16)
    lo = (r1 - mid.astype(F32)).astype(BF16)
    return jnp.concatenate([hi, mid, lo], axis=-1)


def _ssd_load_carry(ctx_ref, h0_ref, state, heads):
    for k in range(SSM_CONV - 1):
        heads[k] = ctx_ref[0, k * SUBLANES:(k + 1) * SUBLANES, :]
    for g in range(N_GROUPS):
        if h0_ref is None:
            state[g] = jnp.zeros((D_STATE, GROUP_WIDTH), F32)
        else:
            state[g] = h0_ref[0, g * GROUP_WIDTH:(g + 1) * GROUP_WIDTH, :].T


def _ssd_emit_carry(st_out, ctx_out, state, heads):
    ctx_out[0] = heads[SSM_CONV - 2]
    for g in range(N_GROUPS):
        st_out[0, g * GROUP_WIDTH:(g + 1) * GROUP_WIDTH, :] = state[g].T


def _ssd_conv(x_raw, prm, heads, valid_rows):
    cw_ref, cb_ref = prm[:2]
    n_rows = x_raw.shape[0]
    row_in_slab = lax.broadcasted_iota(jnp.int32, (n_rows, CONV_DIM), 0) % SUBLANES
    acc = cb_ref[...] + x_raw * cw_ref[SSM_CONV - 1:SSM_CONV, :]
    for k in range(1, SSM_CONV):
        rot = pltpu.roll(x_raw.reshape(n_rows // SUBLANES, SUBLANES, CONV_DIM), k, 1).reshape(n_rows, CONV_DIM)
        above = jnp.concatenate([heads[k - 1], rot[:n_rows - SUBLANES]], axis=0)
        acc = acc + jnp.where(row_in_slab < k, above, rot) * cw_ref[SSM_CONV - 1 - k:SSM_CONV - k, :]
        heads[k - 1] = rot[valid_rows - SUBLANES:valid_rows]
    return _silu(acc)


def _ssd_steps(dt_raw, prm, valid_rows):
    dtb_ref, alog_ref, expand_ref = prm[2], prm[3], prm[6]
    n_rows = dt_raw.shape[0]
    dt = _softplus(dt_raw + dtb_ref[...])
    if valid_rows < n_rows:
        rows = lax.broadcasted_iota(jnp.int32, (n_rows, DT_PAD), 0)
        dt = jnp.where(rows < valid_rows, dt, 0.0)
    ac = dt * (-jnp.exp(alog_ref[...]))
    ri = lax.broadcasted_iota(jnp.int32, (n_rows, n_rows), 0)
    ci = lax.broadcasted_iota(jnp.int32, (n_rows, n_rows), 1)
    same_chunk_causal = jnp.logical_and(ri // CHUNK == ci // CHUNK, ri >= ci)
    acs = jnp.dot(same_chunk_causal.astype(F32), ac, preferred_element_type=F32,
                  precision=lax.Precision.HIGHEST)
    expand = lambda v: jnp.dot(_split3(v), expand_ref[...], preferred_element_type=F32)
    return expand(acs), expand(dt)


def _ssd_chunk(xc, acol, dtcol, z, prm, state):
    dskip_ref, gssm_ref = prm[4], prm[5]
    q = CHUNK

    xs = xc[:, :D_INNER]
    bm = xc[:, D_INNER:D_INNER + N_GROUPS * D_STATE]
    cm = xc[:, D_INNER + N_GROUPS * D_STATE:].astype(BF16)

    last = acol[q - 1:q, :]
    dx = dtcol * xs
    wj = (jnp.exp(last - acol) * dx).astype(BF16)
    ea = jnp.exp(acol)

    lane = lax.broadcasted_iota(jnp.int32, (q, D_INNER), 1)
    rowi = lax.broadcasted_iota(jnp.int32, (q, D_INNER), 0)
    arow = jnp.sum(jnp.where(lane % q == rowi, acol, 0.0), axis=0, keepdims=True)

    tri = (lax.broadcasted_iota(jnp.int32, (q, MXU_DIM), 0)
           >= lax.broadcasted_iota(jnp.int32, (q, MXU_DIM), 1) % q)
    br = lax.broadcasted_iota(jnp.int32, (MXU_DIM, MXU_DIM), 0) // SSM_HEAD_DIM
    bc = lax.broadcasted_iota(jnp.int32, (MXU_DIM, MXU_DIM), 1) // SSM_HEAD_DIM
    blockdiag = br == bc

    y_blocks = []
    for g in range(N_GROUPS):
        cg = cm[:, g * D_STATE:(g + 1) * D_STATE]
        bg = bm[:, g * D_STATE:(g + 1) * D_STATE].astype(BF16)
        bg4 = jnp.concatenate([bg] * HEADS_PER_MXU_TILE, axis=0)
        cb4 = lax.dot_general(cg, bg4, (((1,), (1,)), ((), ())), preferred_element_type=F32)
        sg = state[g].astype(BF16)
        for half in range(GROUP_WIDTH // MXU_DIM):
            lo = g * GROUP_WIDTH + half * MXU_DIM
            cols = slice(lo, lo + MXU_DIM)
            seg = acol[:, cols] - arow[:, cols]
            decay = jnp.where(tri, jnp.exp(jnp.where(tri, seg, 0.0)), 0.0)
            mp = (cb4 * decay).astype(BF16)
            dxb = dx[:, cols].astype(BF16)
            bd = jnp.where(blockdiag, jnp.concatenate([dxb] * HEADS_PER_MXU_TILE, axis=0),
                           jnp.zeros((), BF16))
            y = jnp.dot(mp, bd, preferred_element_type=F32)
            y = y + jnp.dot(cg, sg[:, half * MXU_DIM:(half + 1) * MXU_DIM],
                            preferred_element_type=F32) * ea[:, cols]
            y_blocks.append(y + dskip_ref[:, cols] * xs[:, cols])

    el = jnp.exp(last)
    for g in range(N_GROUPS):
        cols = slice(g * GROUP_WIDTH, (g + 1) * GROUP_WIDTH)
        bgt = bm[:, g * D_STATE:(g + 1) * D_STATE].T.astype(BF16)
        state[g] = state[g] * el[:, cols] + jnp.dot(bgt, wj[:, cols], preferred_element_type=F32)

    yg = jnp.concatenate(y_blocks, axis=-1) * _silu(z.astype(F32))
    return (yg * _rms_scale(yg) * gssm_ref[...]).astype(BF16)


def _ssd_scratch():
    return [pltpu.VMEM((N_GROUPS, D_STATE, GROUP_WIDTH), F32),
            pltpu.VMEM((SSM_CONV - 1, SUBLANES, CONV_DIM), F32)]


def _ssd_kernel(*refs, has_state, valid_rows, n_chunks):
    if has_state:
        xbc_ref, dt_ref, z_ref, ctx_ref, h0_ref = refs[:5]
        refs = refs[5:]
    else:
        xbc_ref, dt_ref, z_ref, ctx_ref = refs[:4]
        h0_ref = None
        refs = refs[4:]
    prm, (yb_out, st_out, ctx_out, state, heads) = refs[:7], refs[7:]
    c = pl.program_id(1)

    @pl.when(c == 0)
    def _load_carry():
        _ssd_load_carry(ctx_ref, h0_ref, state, heads)

    xc = _ssd_conv(xbc_ref[0], prm, heads, valid_rows)
    acol, dtcol = _ssd_steps(dt_ref[0], prm, valid_rows)
    yb_out[0] = _ssd_chunk(xc, acol, dtcol, z_ref[0], prm, state)

    @pl.when(c == n_chunks - 1)
    def _emit_carry():
        _ssd_emit_carry(st_out, ctx_out, state, heads)


def _ssd(xbc, dt_raw, z, ctx, h0, prm, valid_rows):
    b, seq, _ = xbc.shape
    n_chunks = seq // CHUNK
    has_state = h0 is not None
    chunk = lambda w: pl.BlockSpec((1, CHUNK, w), lambda bi, ci: (bi, ci, 0))
    per_batch = lambda r, w: pl.BlockSpec((1, r, w), lambda bi, ci: (bi, 0, 0))
    in_specs = [chunk(CONV_DIM), chunk(DT_PAD), chunk(D_INNER), per_batch((SSM_CONV - 1) * SUBLANES, CONV_DIM)]
    args = [xbc, dt_raw, z, _conv_heads(ctx, SSM_CONV)]
    if has_state:
        in_specs.append(per_batch(D_INNER, D_STATE))
        args.append(h0)
    in_specs += [_resident(p.shape) for p in prm]
    args += list(prm)
    return pl.pallas_call(
        functools.partial(_ssd_kernel, has_state=has_state, valid_rows=valid_rows, n_chunks=n_chunks),
        grid=(b, n_chunks),
        in_specs=in_specs,
        out_specs=[chunk(D_INNER), per_batch(D_INNER, D_STATE), per_batch(SUBLANES, CONV_DIM)],
        out_shape=[jax.ShapeDtypeStruct((b, seq, D_INNER), BF16),
                   jax.ShapeDtypeStruct((b, D_INNER, D_STATE), F32),
                   jax.ShapeDtypeStruct((b, SUBLANES, CONV_DIM), F32)],
        scratch_shapes=_ssd_scratch(),
        compiler_params=_params(("parallel", "arbitrary")),
        name="ssd",
    )(*args)


def _inproj_kernel(*refs, tiles_per_batch, tail_tiles, fuse_ssd, tm):
    x_ref, g_ref, w_ref, wgates_ref, wdt_ref, gq_ref, gk_ref, hsum_ref = refs[:8]
    if fuse_ssd:
        ctx_ref, prm = refs[8], refs[9:16]
        q_out, k_out, v_out, gates_out, ktail_out, vtail_out, yb_out, st_out, ctx_out, state, heads = refs[16:]
    else:
        q_out, k_out, v_out, gates_out, ktail_out, vtail_out, z_out, xbc_out, dt_out = refs[8:]
    tile_in_batch = pl.program_id(0) % tiles_per_batch

    if fuse_ssd:
        @pl.when(tile_in_batch == 0)
        def _load_carry():
            _ssd_load_carry(ctx_ref, None, state, heads)

    x = x_ref[...]
    h = (x * _rms_scale(x) * g_ref[...]).astype(BF16)

    def seg(bounds):
        return jnp.dot(h, w_ref[:, bounds[0]:bounds[1]], preferred_element_type=F32)

    def head_norm(t, g):
        sq = (t * t).astype(BF16)
        parts = []
        for blk in range(ATT_WIDTH // MXU_DIM):
            cols = slice(blk * MXU_DIM, (blk + 1) * MXU_DIM)
            ss = jnp.dot(sq[:, cols], hsum_ref[...], preferred_element_type=F32)
            parts.append(t[:, cols] * lax.rsqrt(ss * (1.0 / HEAD_DIM_A) + EPS))
        return jnp.concatenate(parts, axis=-1) * g

    xbc = seg(_SEG_XBC)
    zb = seg(_SEG_Z).astype(BF16)
    dt_raw = jnp.dot(h, wdt_ref[...], preferred_element_type=F32)
    if fuse_ssd:
        xc = _ssd_conv(xbc, prm, heads, tm)
        acol, dtcol = _ssd_steps(dt_raw, prm, tm)
        for c in range(tm // CHUNK):
            rows = slice(c * CHUNK, (c + 1) * CHUNK)
            yb_out[rows, :] = _ssd_chunk(xc[rows, :], acol[rows, :], dtcol[rows, :], zb[rows, :], prm, state)
    else:
        xbc_out[...] = xbc
        z_out[...] = zb
        dt_out[...] = dt_raw

    q_out[...] = (head_norm(seg(_SEG_Q), gq_ref[...]) * (HEAD_DIM_A ** -0.5)).astype(BF16)
    k = head_norm(seg(_SEG_K), gk_ref[...])
    v = seg(_SEG_V)
    k_out[...] = k.astype(BF16)
    v_out[...] = v.astype(BF16)
    gates_out[...] = jnp.dot(h, wgates_ref[...], preferred_element_type=F32).astype(BF16)

    @pl.when(tile_in_batch >= tiles_per_batch - tail_tiles)
    def _emit_tail():
        ktail_out[0] = k.T
        vtail_out[0] = v.T

    if fuse_ssd:
        @pl.when(tile_in_batch == tiles_per_batch - 1)
        def _emit_carry():
            _ssd_emit_carry(st_out, ctx_out, state, heads)


def _inproj(x2d, w, tm, seq, ssm_ctx=None):
    t = x2d.shape[0]
    fuse_ssd = ssm_ctx is not None
    tiles_per_batch = max(seq // tm, 1)
    tail_tiles = max(min(ATT_PAST, seq) // tm, 1)
    n_tiles = t // tm
    n_batches = n_tiles // tiles_per_batch
    row = lambda width: pl.BlockSpec((tm, width), lambda i: (i, 0))
    first_tail = tiles_per_batch - tail_tiles
    tail = pl.BlockSpec((1, ATT_WIDTH, tm), lambda i: (
        i // tiles_per_batch, 0, jnp.maximum(i % tiles_per_batch - first_tail, 0)))
    per_batch = lambda r, width: pl.BlockSpec((1, r, width), lambda i: (i // tiles_per_batch, 0, 0))
    rows_out = lambda width, dtype: (row(width), jax.ShapeDtypeStruct((t, width), dtype))
    tail_out = (tail, jax.ShapeDtypeStruct((n_batches, ATT_WIDTH, tail_tiles * tm), F32))

    in_specs = [row(D_MODEL), _resident((1, D_MODEL)), _resident((D_MODEL, _SEG_XBC[1])),
                _resident(w["w_gates"].shape), _resident(w["w_dt"].shape),
                _resident((1, ATT_WIDTH)), _resident((1, ATT_WIDTH)), _resident((MXU_DIM, MXU_DIM))]
    args = [x2d, w["g_mix"], w["w_in"], w["w_gates"], w["w_dt"], w["g_q"], w["g_k"], w["hsum"]]
    outs = [rows_out(ATT_WIDTH, BF16), rows_out(ATT_WIDTH, BF16), rows_out(ATT_WIDTH, BF16),
            rows_out(2 * D_MODEL, BF16), tail_out, tail_out]
    scratch = []
    if fuse_ssd:
        assert seq % tm == 0 and tm % CHUNK == 0
        prm = _ssd_params(w)
        in_specs += [per_batch((SSM_CONV - 1) * SUBLANES, CONV_DIM)] + [_resident(p.shape) for p in prm]
        args += [_conv_heads(ssm_ctx, SSM_CONV)] + list(prm)
        outs += [rows_out(D_INNER, BF16),
                 (per_batch(D_INNER, D_STATE), jax.ShapeDtypeStruct((n_batches, D_INNER, D_STATE), F32)),
                 (per_batch(SUBLANES, CONV_DIM), jax.ShapeDtypeStruct((n_batches, SUBLANES, CONV_DIM), F32))]
        scratch = _ssd_scratch()
    else:
        outs += [rows_out(D_INNER, BF16), rows_out(CONV_DIM, F32), rows_out(DT_PAD, F32)]
    return pl.pallas_call(
        functools.partial(_inproj_kernel, tiles_per_batch=tiles_per_batch, tail_tiles=tail_tiles,
                          fuse_ssd=fuse_ssd, tm=tm),
        grid=(n_tiles,),
        in_specs=in_specs,
        out_specs=[o[0] for o in outs],
        out_shape=[o[1] for o in outs],
        scratch_shapes=scratch,
        compiler_params=_params(("arbitrary",)),
        name="inproj_ssd" if fuse_ssd else "inproj",
    )(*args)


HEADS_PER_LANE_TILE = LANES // HEAD_DIM_A


def _attn_kernel(*refs, has_cache, seq, qb, width):
    if has_cache:
        q_ref, k_ref, v_ref, bias_ref, ck_ref, cv_ref, o_ref, kp, vp = refs
    else:
        q_ref, k_ref, v_ref, bias_ref, o_ref, kp, vp = refs
    nk = ATT_PAST + qb

    if has_cache:
        kp[0:ATT_PAST, :] = ck_ref[0].T.astype(BF16)
        vp[0:ATT_PAST, :] = cv_ref[0].T.astype(BF16)
    kp[ATT_PAST:ATT_PAST + seq, :] = k_ref[0]
    vp[ATT_PAST:ATT_PAST + seq, :] = v_ref[0]

    lane = lax.broadcasted_iota(jnp.int32, (qb, LANES), 1)
    low_half = lane < HEAD_DIM_A

    def q_block(start, first_col):
        kb = kp[pl.ds(start + first_col, nk - first_col), :]
        vb = vp[pl.ds(start + first_col, nk - first_col), :]
        q = q_ref[0, pl.ds(start, qb), :]
        outs = []
        for pair in range(width // LANES):
            cols = slice(pair * LANES, (pair + 1) * LANES)
            qp, kpair, vpair = q[:, cols], kb[:, cols], vb[:, cols]
            zero = jnp.zeros((), BF16)
            q2 = jnp.concatenate([jnp.where(low_half, qp, zero), jnp.where(low_half, zero, qp)], axis=0)
            s = lax.dot_general(q2, kpair, (((1,), (1,)), ((), ())), preferred_element_type=F32)
            first_head = pair * HEADS_PER_LANE_TILE
            bias = bias_ref[first_head:first_head + HEADS_PER_LANE_TILE, :, first_col:]
            s = s + bias.reshape(HEADS_PER_LANE_TILE * qb, nk - first_col)
            m = jnp.max(s, axis=-1, keepdims=True)
            p = jnp.exp(s - m)
            denom = jnp.sum(p, axis=-1, keepdims=True)
            o2 = jnp.dot(p.astype(BF16), vpair, preferred_element_type=F32) / denom
            outs.append(jnp.where(low_half, o2[:qb], o2[qb:]))
        o_ref[0, pl.ds(start, qb), :] = jnp.concatenate(outs, axis=-1).astype(BF16)

    n_blocks = seq // qb
    n_peeled = 0 if has_cache else min(ATT_PAST // qb, n_blocks)
    for i in range(n_peeled):
        q_block(i * qb, ATT_PAST - i * qb)

    n_steady = n_blocks - n_peeled
    unroll = 6 if n_steady % 6 == 0 and n_steady > 0 else 1

    def steady(i, carry):
        for u in range(unroll):
            q_block(pl.multiple_of((n_peeled + i * unroll + u) * qb, qb), 0)
        return carry

    lax.fori_loop(0, n_steady // unroll, steady, 0)


def _attention(q, k, v, bias, cache, qb):
    b, seq, _ = q.shape
    nk = ATT_PAST + qb
    has_cache = cache is not None
    width = MXU_DIM if seq > ATT_PAST else ATT_WIDTH
    hb = ATT_WIDTH // width
    seq_spec = pl.BlockSpec((1, seq, width), lambda bi, hi: (bi, 0, hi))
    in_specs = [seq_spec, seq_spec, seq_spec,
                pl.BlockSpec((width // HEAD_DIM_A, qb, nk), lambda bi, hi: (hi, 0, 0))]
    args = [q, k, v, bias]
    if has_cache:
        cspec = pl.BlockSpec((1, width, ATT_PAST), lambda bi, hi: (bi, hi, 0))
        in_specs += [cspec, cspec]
        args += list(cache)
    return pl.pallas_call(
        functools.partial(_attn_kernel, has_cache=has_cache, seq=seq, qb=qb, width=width),
        grid=(b, hb),
        in_specs=in_specs,
        out_specs=seq_spec,
        out_shape=jax.ShapeDtypeStruct((b, seq, ATT_WIDTH), BF16),
        scratch_shapes=[pltpu.VMEM((ATT_PAST + seq, width), BF16),
                        pltpu.VMEM((ATT_PAST + seq, width), BF16)],
        compiler_params=_params(("parallel", "parallel")),
        name="band_attention",
    )(*args)


def _attention_bias(table, qb):
    nk = ATT_PAST + qb
    n_diag = qb + nk - 1
    diag_rel = (ATT_PAST + qb - 1) - jnp.arange(n_diag)
    r = table[:, jnp.clip(diag_rel, -REL_CLIP, REL_CLIP) + REL_CLIP]
    h = table.shape[0]
    row_len = 1 << n_diag.bit_length()
    rp = jnp.roll(jnp.pad(r, ((0, 0), (0, row_len - n_diag))), -(qb - 1), axis=1).reshape(h, 1, row_len)

    def bias_kernel(rp_ref, o_ref):
        rolled = pltpu.roll(jnp.broadcast_to(rp_ref[0], (qb, row_len)), 0, 1, stride=1, stride_axis=0)
        qc = lax.broadcasted_iota(jnp.int32, (qb, nk), 0) // CHUNK
        kc = lax.broadcasted_iota(jnp.int32, (qb, nk), 1) // CHUNK
        visible = jnp.logical_and(kc >= qc, kc <= qc + ATT_PAST // CHUNK)
        o_ref[0] = jnp.where(visible, rolled[:, :nk], NEG_INF)

    return pl.pallas_call(
        bias_kernel,
        grid=(h,),
        in_specs=[pl.BlockSpec((1, 1, row_len), lambda hi: (hi, 0, 0))],
        out_specs=pl.BlockSpec((1, qb, nk), lambda hi: (hi, 0, 0)),
        out_shape=jax.ShapeDtypeStruct((h, qb, nk), F32),
        compiler_params=_params(("parallel",)),
        name="attention_bias",
    )(rp.astype(F32))


def _ffn_kernel(x_ref, ya_ref, yb_ref, gates_ref, pe_ref, ctx_ref, bg_ref, wa_ref, wb_ref, wo_ref,
                gffn_ref, wup_ref, cw_ref, cb_ref, wdown_ref, gple_ref, wgate_ref, wple_ref,
                o_ref, ctx_out, heads, *, tm, seq_rows, tiles_per_batch):
    i = pl.program_id(0)
    n_seqs = tm // seq_rows

    @pl.when(i % tiles_per_batch == 0)
    def _load_context():
        for k in range(FFN_CONV - 1):
            heads[k] = ctx_ref[:, k * SUBLANES:(k + 1) * SUBLANES, :]

    mix_gate = jax.nn.sigmoid(gates_ref[...].astype(F32) + bg_ref[...])
    mixed = (mix_gate[:, :D_MODEL] * jnp.dot(ya_ref[...], wa_ref[...], preferred_element_type=F32)
             + mix_gate[:, D_MODEL:] * jnp.dot(yb_ref[...], wb_ref[...], preferred_element_type=F32))
    x = x_ref[...] + jnp.dot(mixed.astype(BF16), wo_ref[...], preferred_element_type=F32)
    xn = (x * _rms_scale(x) * gffn_ref[...]).astype(BF16)
    n_slabs = tm // SUBLANES
    row_in_slab = lax.broadcasted_iota(jnp.int32, (tm, FFN_COL_BLOCK), 0) % SUBLANES

    def up(j, part):
        lo = part * D_FF + j * FFN_COL_BLOCK
        return jnp.dot(xn, wup_ref[:, lo:lo + FFN_COL_BLOCK], preferred_element_type=F32)

    def conv(u, j, part):
        lo = part * D_FF + j * FFN_COL_BLOCK
        cols = slice(lo, lo + FFN_COL_BLOCK)
        out = cb_ref[:, cols] + u * cw_ref[FFN_CONV - 1:FFN_CONV, cols]
        for k in range(1, FFN_CONV):
            rot = pltpu.roll(u.reshape(n_slabs, SUBLANES, FFN_COL_BLOCK), k, 1).reshape(tm, FFN_COL_BLOCK)
            pieces = []
            for s in range(n_seqs):
                pieces += [heads[k - 1, s, :, cols], rot[s * seq_rows:(s + 1) * seq_rows - SUBLANES]]
            shifted = jnp.where(row_in_slab < k, jnp.concatenate(pieces, axis=0), rot)
            out = out + shifted * cw_ref[FFN_CONV - 1 - k:FFN_CONV - k, cols]
            for s in range(n_seqs):
                heads[k - 1, s, :, cols] = rot[(s + 1) * seq_rows - SUBLANES:(s + 1) * seq_rows]
        return out

    n_blocks = D_FF // FFN_COL_BLOCK
    acc = jnp.zeros((tm, D_MODEL), F32)
    pending = (up(0, 0), up(0, 1))
    for j in range(n_blocks):
        ug, uv = pending
        if j + 1 < n_blocks:
            pending = (up(j + 1, 0), up(j + 1, 1))
        g = conv(ug, j, 0)
        gate_arg = g * (GELU_C0 + GELU_C1 * (g * g))
        hidden = (g * (1.0 + jnp.tanh(gate_arg)) * conv(uv, j, 1)).astype(BF16)
        acc = acc + jnp.dot(hidden, wdown_ref[j * FFN_COL_BLOCK:(j + 1) * FFN_COL_BLOCK, :],
                            preferred_element_type=F32)

    @pl.when(i % tiles_per_batch == tiles_per_batch - 1)
    def _emit_context():
        ctx_out[...] = heads[FFN_CONV - 2]

    x2 = x + acc
    gate = jax.nn.sigmoid(jnp.dot((x2 * _rms_scale(x2) * gple_ref[...]).astype(BF16), wgate_ref[...],
                                  preferred_element_type=F32))
    o_ref[...] = x2 + gate * jnp.dot(pe_ref[...].astype(BF16), wple_ref[...], preferred_element_type=F32)


def _merge_ffn(x2d, ya, yb, gates, pe2d, ctx, prm, seq, tm):
    t = x2d.shape[0]
    batch = t // seq
    seq_rows = min(seq, tm)
    seqs_per_tile = tm // seq_rows
    tiles_per_batch = seq // seq_rows
    row = lambda w: pl.BlockSpec((tm, w), lambda i: (i, 0))
    per_seq = lambda r: pl.BlockSpec((seqs_per_tile, r, 2 * D_FF), lambda i: (i // tiles_per_batch, 0, 0))
    y, ctx_out = pl.pallas_call(
        functools.partial(_ffn_kernel, tm=tm, seq_rows=seq_rows, tiles_per_batch=tiles_per_batch),
        grid=(t // tm,),
        in_specs=[row(D_MODEL), row(ATT_WIDTH), row(D_INNER), row(2 * D_MODEL), row(PLE_DIM),
                  per_seq((FFN_CONV - 1) * SUBLANES)] + [_resident(p.shape) for p in prm],
        out_specs=[row(D_MODEL), per_seq(SUBLANES)],
        out_shape=[jax.ShapeDtypeStruct((t, D_MODEL), F32),
                   jax.ShapeDtypeStruct((batch, SUBLANES, 2 * D_FF), F32)],
        scratch_shapes=[pltpu.VMEM((FFN_CONV - 1, seqs_per_tile, SUBLANES, 2 * D_FF), F32)],
        compiler_params=_params(("arbitrary",)),
        name="merge_convffn_ple",
    )(x2d, ya, yb, gates, pe2d, _conv_heads(ctx, FFN_CONV), *prm)
    return y, ctx_out[:, :FFN_CONV - 1]


def _ssd_params(w):
    return (w["conv_ssm_w"], w["conv_ssm_b"], w["dt_bias"], w["a_log"], w["d_skip"], w["g_ssm"], w["expand"])


def _layer(x, pe, kv_cache, ssm_ctx, h0, ffn_ctx, w, tm, tq, tm_ffn):
    b, seq, _ = x.shape
    t = b * seq
    x2d = x.reshape(t, D_MODEL)
    as3 = lambda a: a.reshape(b, seq, -1)
    fuse_ssd = h0 is None and seq % tm == 0 and tm % CHUNK == 0
    if fuse_ssd:
        qn, kn, v, gates, k_tail, v_tail, yb, st, ctx_ssm = _inproj(x2d, w, tm, seq, ssm_ctx)
    else:
        qn, kn, v, gates, k_tail, v_tail, z, xbc, dt_raw = _inproj(x2d, w, tm, seq)
        seq_pad = -(-seq // CHUNK) * CHUNK
        valid_rows = CHUNK - (seq_pad - seq)
        pad = lambda a: jnp.pad(as3(a), ((0, 0), (0, seq_pad - seq), (0, 0)))
        yb, st, ctx_ssm = _ssd(pad(xbc), pad(dt_raw), pad(z), ssm_ctx, h0, _ssd_params(w), valid_rows)
        yb = yb[:, :seq].reshape(t, D_INNER)

    cache = None
    if kv_cache is not None:
        cache = tuple(jnp.transpose(c, (0, 2, 3, 1)).reshape(b, ATT_WIDTH, ATT_PAST) for c in kv_cache)
    ya = _attention(as3(qn), as3(kn), as3(v), _attention_bias(w["rel_bias"], tq), cache, tq)

    ffn_prm = (w["b_gate"], w["w_proj_a"], w["w_proj_b"], w["w_out"],
               w["g_ffn"], w["w_up"], w["ffn_conv_w"], w["ffn_conv_b"], w["w_down"],
               w["g_ple"], w["w_ple_gate"], w["w_ple"])
    y, ctx_ffn = _merge_ffn(x2d, ya.reshape(t, ATT_WIDTH), yb, gates, pe.reshape(t, PLE_DIM), ffn_ctx, ffn_prm,
                            seq, tm_ffn)

    def heads(tail_t):
        if seq >= tm:
            return jnp.transpose(tail_t.reshape(b, N_HEADS_A, HEAD_DIM_A, -1), (0, 3, 1, 2))
        return jnp.transpose(tail_t.reshape(N_HEADS_A, HEAD_DIM_A, b, seq), (2, 3, 0, 1))

    state = st.reshape(b, N_HEADS_S, SSM_HEAD_DIM, D_STATE)
    return y.reshape(b, seq, D_MODEL), (heads(k_tail), heads(v_tail), state, ctx_ssm[:, :SSM_CONV - 1], ctx_ffn)


def _prep_weights(i, g_mix, w_in, b_gate, g_q, g_k, rel_bias, conv_ssm_w, conv_ssm_b, dt_bias, a_log, d_skip,
                  g_ssm, w_proj_a, w_proj_b, w_out, g_ffn, w_up, ffn_conv_w, ffn_conv_b, w_down, g_ple,
                  w_ple_gate, w_ple):
    row = lambda a: a[i].reshape(1, -1).astype(F32)
    pad_lanes = lambda a, n: jnp.pad(a, ((0, 0), (0, n - a.shape[1])))
    wi = w_in[i].astype(BF16)
    head_of_lane = jnp.arange(D_INNER) // SSM_HEAD_DIM
    expand1 = (jnp.arange(DT_PAD)[:, None] == head_of_lane[None, :]).astype(BF16)
    blk = jnp.arange(MXU_DIM) // HEAD_DIM_A
    half_on_values = jnp.where(jnp.arange(2 * D_FF) < D_FF, 1.0, 0.5).astype(F32)[None, :]
    return {
        "g_mix": row(g_mix), "w_in": wi, "w_gates": wi[:, _SEG_GATES[0]:_SEG_GATES[1]],
        "w_dt": pad_lanes(wi[:, _SEG_DT[0]:_SEG_DT[1]], DT_PAD), "b_gate": row(b_gate),
        "g_q": jnp.tile(row(g_q), (1, N_HEADS_A)), "g_k": jnp.tile(row(g_k), (1, N_HEADS_A)),
        "hsum": (blk[:, None] == blk[None, :]).astype(BF16),
        "rel_bias": rel_bias[i].astype(F32),
        "conv_ssm_w": jnp.pad(conv_ssm_w[i], ((0, SUBLANES - SSM_CONV), (0, 0))), "conv_ssm_b": row(conv_ssm_b),
        "dt_bias": pad_lanes(row(dt_bias), DT_PAD), "a_log": pad_lanes(row(a_log), DT_PAD),
        "d_skip": jnp.repeat(row(d_skip), SSM_HEAD_DIM, axis=1), "g_ssm": row(g_ssm),
        "expand": jnp.concatenate([expand1] * 3, axis=0),
        "w_proj_a": w_proj_a[i].astype(BF16), "w_proj_b": w_proj_b[i].astype(BF16), "w_out": w_out[i].astype(BF16),
        "g_ffn": row(g_ffn), "w_up": w_up[i].astype(BF16),
        "ffn_conv_w": jnp.pad(ffn_conv_w[i] * half_on_values, ((0, SUBLANES - FFN_CONV), (0, 0))),
        "ffn_conv_b": row(ffn_conv_b) * half_on_values,
        "w_down": w_down[i].astype(BF16), "g_ple": row(g_ple),
        "w_ple_gate": w_ple_gate[i].astype(BF16), "w_ple": w_ple[i].astype(BF16),
    }


def kernel(x_prompt, x_sample, cache_k, cache_v, state_ssm, state_conv_ssm, state_conv_ffn, p_prompt, p_sample, g_mix, w_in, b_gate, g_q, g_k, rel_bias, conv_ssm_w, conv_ssm_b, dt_bias, a_log, d_skip, g_ssm, w_proj_a, w_proj_b, w_out, g_ffn, w_up, ffn_conv_w, ffn_conv_b, w_down, g_ple, w_ple_gate, w_ple):
    depth = w_in.shape[0]
    bp = x_prompt.shape[0]
    bs, seq_s = x_sample.shape[:2]
    yp, ys = x_prompt, x_sample
    sp, ss = [], []
    for i in range(depth):
        w = _prep_weights(i, g_mix, w_in, b_gate, g_q, g_k, rel_bias, conv_ssm_w, conv_ssm_b, dt_bias, a_log,
                          d_skip, g_ssm, w_proj_a, w_proj_b, w_out, g_ffn, w_up, ffn_conv_w, ffn_conv_b, w_down,
                          g_ple, w_ple_gate, w_ple)
        yp, st_p = _layer(yp, p_prompt[i], None,
                          jnp.zeros((bp, SSM_CONV - 1, CONV_DIM), F32), None,
                          jnp.zeros((bp, FFN_CONV - 1, 2 * D_FF), F32), w, tm=256, tq=4 * CHUNK, tm_ffn=256)
        h0 = state_ssm[i].reshape(bs, D_INNER, D_STATE)
        ys, st_s = _layer(ys, p_sample[i], (cache_k[i], cache_v[i]), state_conv_ssm[i], h0,
                          state_conv_ffn[i], w, tm=seq_s * bs, tq=seq_s, tm_ffn=seq_s * bs)
        sp.append(st_p)
        ss.append(st_s)

    stack = lambda lst, j: jnp.stack([s[j] for s in lst], axis=0)
    return (yp, ys,
            stack(sp, 0), stack(sp, 1), stack(sp, 2), stack(sp, 3), stack(sp, 4),
            stack(ss, 0), stack(ss, 1), stack(ss, 2), stack(ss, 3), stack(ss, 4))
```

```python
import functools

import jax
import jax.numpy as jnp
from jax import lax
from jax.experimental import pallas as pl
from jax.experimental.pallas import tpu as pltpu

F32 = jnp.float32
BF16 = jnp.bfloat16

D_MODEL = 1024
CHUNK = 64
PLE_DIM = 256
EPS = 1e-6
NEG_INF = -1e30
N_HEADS_A = 16
HEAD_DIM_A = 64
ATT_WIDTH = N_HEADS_A * HEAD_DIM_A
ATT_PAST = 8 * CHUNK
REL_CLIP = 128
D_INNER = 2 * D_MODEL
SSM_HEAD_DIM = 64
N_HEADS_S = D_INNER // SSM_HEAD_DIM
N_GROUPS = 4
GROUP_WIDTH = D_INNER // N_GROUPS
D_STATE = 128
SSM_CONV = 4
CONV_DIM = D_INNER + 2 * N_GROUPS * D_STATE
D_FF = 3 * D_MODEL
FFN_CONV = 3

LANES = 128
SUBLANES = 8
MXU_DIM = 256
VMEM_LIMIT_BYTES = 56 * 1024 * 1024

DT_PAD = LANES
HEADS_PER_MXU_TILE = MXU_DIM // SSM_HEAD_DIM
FFN_COL_BLOCK = 1024
GELU_C0 = (2.0 / 3.141592653589793) ** 0.5
GELU_C1 = GELU_C0 * 0.044715

_SEG_Q = (0, ATT_WIDTH)
_SEG_K = (ATT_WIDTH, 2 * ATT_WIDTH)
_SEG_V = (2 * ATT_WIDTH, 3 * ATT_WIDTH)
_SEG_Z = (3 * ATT_WIDTH, 3 * ATT_WIDTH + D_INNER)
_SEG_XBC = (_SEG_Z[1], _SEG_Z[1] + CONV_DIM)
_SEG_DT = (_SEG_XBC[1], _SEG_XBC[1] + N_HEADS_S)
_SEG_GATES = (_SEG_DT[1], _SEG_DT[1] + 2 * D_MODEL)


def _resident(shape):
    nd = len(shape)
    return pl.BlockSpec(shape, lambda *_: (0,) * nd, pipeline_mode=pl.Buffered(1))


def _params(semantics):
    return pltpu.CompilerParams(dimension_semantics=semantics, vmem_limit_bytes=VMEM_LIMIT_BYTES)


def _rms_scale(x):
    return lax.rsqrt(jnp.mean(x * x, axis=-1, keepdims=True) + EPS)


def _silu(x):
    half = 0.5 * x
    return half * (1.0 + jnp.tanh(half))


def _conv_heads(ctx, width):
    heads = [jnp.pad(ctx[:, width - 1 - k:], ((0, 0), (0, SUBLANES - k), (0, 0))) for k in range(1, width)]
    return jnp.concatenate(heads, axis=1)


def _softplus(x):
    return jnp.maximum(x, 0.0) + jnp.log(1.0 + jnp.exp(-jnp.abs(x)))


def _split3(v):
    hi = v.astype(BF16)
    r1 = v - hi.astype(F32)
    mid = r1.astype(BF16)
    lo = (r1 - mid.astype(F32)).astype(BF16)
    return jnp.concatenate([hi, mid, lo], axis=-1)


def _ssd_load_carry(ctx_ref, h0_ref, state, heads):
    for k in range(SSM_CONV - 1):
        heads[k] = ctx_ref[0, k * SUBLANES:(k + 1) * SUBLANES, :]
    for g in range(N_GROUPS):
        if h0_ref is None:
            state[g] = jnp.zeros((D_STATE, GROUP_WIDTH), F32)
        else:
            state[g] = h0_ref[0, g * GROUP_WIDTH:(g + 1) * GROUP_WIDTH, :].T


def _ssd_emit_carry(st_out, ctx_out, state, heads):
    ctx_out[0] = heads[SSM_CONV - 2]
    for g in range(N_GROUPS):
        st_out[0, g * GROUP_WIDTH:(g + 1) * GROUP_WIDTH, :] = state[g].T


def _ssd_conv(x_raw, prm, heads, valid_rows):
    cw_ref, cb_ref = prm[:2]
    n_rows = x_raw.shape[0]
    row_in_slab = lax.broadcasted_iota(jnp.int32, (n_rows, CONV_DIM), 0) % SUBLANES
    acc = cb_ref[...] + x_raw * cw_ref[SSM_CONV - 1:SSM_CONV, :]
    for k in range(1, SSM_CONV):
        rot = pltpu.roll(x_raw.reshape(n_rows // SUBLANES, SUBLANES, CONV_DIM), k, 1).reshape(n_rows, CONV_DIM)
        above = jnp.concatenate([heads[k - 1], rot[:n_rows - SUBLANES]], axis=0)
        acc = acc + jnp.where(row_in_slab < k, above, rot) * cw_ref[SSM_CONV - 1 - k:SSM_CONV - k, :]
        heads[k - 1] = rot[valid_rows - SUBLANES:valid_rows]
    return _silu(acc)


def _ssd_steps(dt_raw, prm, valid_rows):
    dtb_ref, alog_ref, expand_ref = prm[2], prm[3], prm[6]
    n_rows = dt_raw.shape[0]
    dt = _softplus(dt_raw + dtb_ref[...])
    if valid_rows < n_rows:
        rows = lax.broadcasted_iota(jnp.int32, (n_rows, DT_PAD), 0)
        dt = jnp.where(rows < valid_rows, dt, 0.0)
    ac = dt * (-jnp.exp(alog_ref[...]))
    ri = lax.broadcasted_iota(jnp.int32, (n_rows, n_rows), 0)
    ci = lax.broadcasted_iota(jnp.int32, (n_rows, n_rows), 1)
    same_chunk_causal = jnp.logical_and(ri // CHUNK == ci // CHUNK, ri >= ci)
    acs = jnp.dot(same_chunk_causal.astype(F32), ac, preferred_element_type=F32,
                  precision=lax.Precision.HIGHEST)
    expand = lambda v: jnp.dot(_split3(v), expand_ref[...], preferred_element_type=F32)
    return expand(acs), expand(dt)


def _ssd_chunk(xc, acol, dtcol, z, prm, state):
    dskip_ref, gssm_ref = prm[4], prm[5]
    q = CHUNK

    xs = xc[:, :D_INNER]
    bm = xc[:, D_INNER:D_INNER + N_GROUPS * D_STATE]
    cm = xc[:, D_INNER + N_GROUPS * D_STATE:].astype(BF16)

    last = acol[q - 1:q, :]
    dx = dtcol * xs
    wj = (jnp.exp(last - acol) * dx).astype(BF16)
    ea = jnp.exp(acol)

    lane = lax.broadcasted_iota(jnp.int32, (q, D_INNER), 1)
    rowi = lax.broadcasted_iota(jnp.int32, (q, D_INNER), 0)
    arow = jnp.sum(jnp.where(lane % q == rowi, acol, 0.0), axis=0, keepdims=True)

    tri = (lax.broadcasted_iota(jnp.int32, (q, MXU_DIM), 0)
           >= lax.broadcasted_iota(jnp.int32, (q, MXU_DIM), 1) % q)
    br = lax.broadcasted_iota(jnp.int32, (MXU_DIM, MXU_DIM), 0) // SSM_HEAD_DIM
    bc = lax.broadcasted_iota(jnp.int32, (MXU_DIM, MXU_DIM), 1) // SSM_HEAD_DIM
    blockdiag = br == bc

    y_blocks = []
    for g in range(N_GROUPS):
        cg = cm[:, g * D_STATE:(g + 1) * D_STATE]
        bg = bm[:, g * D_STATE:(g + 1) * D_STATE].astype(BF16)
        bg4 = jnp.concatenate([bg] * HEADS_PER_MXU_TILE, axis=0)
        cb4 = lax.dot_general(cg, bg4, (((1,), (1,)), ((), ())), preferred_element_type=F32)
        sg = state[g].astype(BF16)
        for half in range(GROUP_WIDTH // MXU_DIM):
            lo = g * GROUP_WIDTH + half * MXU_DIM
            cols = slice(lo, lo + MXU_DIM)
            seg = acol[:, cols] - arow[:, cols]
            decay = jnp.where(tri, jnp.exp(jnp.where(tri, seg, 0.0)), 0.0)
            mp = (cb4 * decay).astype(BF16)
            dxb = dx[:, cols].astype(BF16)
            bd = jnp.where(blockdiag, jnp.concatenate([dxb] * HEADS_PER_MXU_TILE, axis=0),
                           jnp.zeros((), BF16))
            y = jnp.dot(mp, bd, preferred_element_type=F32)
            y = y + jnp.dot(cg, sg[:, half * MXU_DIM:(half + 1) * MXU_DIM],
                            preferred_element_type=F32) * ea[:, cols]
            y_blocks.append(y + dskip_ref[:, cols] * xs[:, cols])

    el = jnp.exp(last)
    for g in range(N_GROUPS):
        cols = slice(g * GROUP_WIDTH, (g + 1) * GROUP_WIDTH)
        bgt = bm[:, g * D_STATE:(g + 1) * D_STATE].T.astype(BF16)
        state[g] = state[g] * el[:, cols] + jnp.dot(bgt, wj[:, cols], preferred_element_type=F32)

    yg = jnp.concatenate(y_blocks, axis=-1) * _silu(z.astype(F32))
    return (yg * _rms_scale(yg) * gssm_ref[...]).astype(BF16)


def _ssd_scratch():
    return [pltpu.VMEM((N_GROUPS, D_STATE, GROUP_WIDTH), F32),
            pltpu.VMEM((SSM_CONV - 1, SUBLANES, CONV_DIM), F32)]


def _ssd_kernel(*refs, has_state, valid_rows, n_chunks):
    if has_state:
        xbc_ref, dt_ref, z_ref, ctx_ref, h0_ref = refs[:5]
        refs = refs[5:]
    else:
        xbc_ref, dt_ref, z_ref, ctx_ref = refs[:4]
        h0_ref = None
        refs = refs[4:]
    prm, (yb_out, st_out, ctx_out, state, heads) = refs[:7], refs[7:]
    c = pl.program_id(1)

    @pl.when(c == 0)
    def _load_carry():
        _ssd_load_carry(ctx_ref, h0_ref, state, heads)

    xc = _ssd_conv(xbc_ref[0], prm, heads, valid_rows)
    acol, dtcol = _ssd_steps(dt_ref[0], prm, valid_rows)
    yb_out[0] = _ssd_chunk(xc, acol, dtcol, z_ref[0], prm, state)

    @pl.when(c == n_chunks - 1)
    def _emit_carry():
        _ssd_emit_carry(st_out, ctx_out, state, heads)


def _ssd(xbc, dt_raw, z, ctx, h0, prm, valid_rows):
    b, seq, _ = xbc.shape
    n_chunks = seq // CHUNK
    has_state = h0 is not None
    chunk = lambda w: pl.BlockSpec((1, CHUNK, w), lambda bi, ci: (bi, ci, 0))
    per_batch = lambda r, w: pl.BlockSpec((1, r, w), lambda bi, ci: (bi, 0, 0))
    in_specs = [chunk(CONV_DIM), chunk(DT_PAD), chunk(D_INNER), per_batch((SSM_CONV - 1) * SUBLANES, CONV_DIM)]
    args = [xbc, dt_raw, z, _conv_heads(ctx, SSM_CONV)]
    if has_state:
        in_specs.append(per_batch(D_INNER, D_STATE))
        args.append(h0)
    in_specs += [_resident(p.shape) for p in prm]
    args += list(prm)
    return pl.pallas_call(
        functools.partial(_ssd_kernel, has_state=has_state, valid_rows=valid_rows, n_chunks=n_chunks),
        grid=(b, n_chunks),
        in_specs=in_specs,
        out_specs=[chunk(D_INNER), per_batch(D_INNER, D_STATE), per_batch(SUBLANES, CONV_DIM)],
        out_shape=[jax.ShapeDtypeStruct((b, seq, D_INNER), BF16),
                   jax.ShapeDtypeStruct((b, D_INNER, D_STATE), F32),
                   jax.ShapeDtypeStruct((b, SUBLANES, CONV_DIM), F32)],
        scratch_shapes=_ssd_scratch(),
        compiler_params=_params(("parallel", "arbitrary")),
        name="ssd",
    )(*args)


def _inproj_kernel(*refs, tiles_per_batch, tail_tiles, fuse_ssd, tm):
    x_ref, g_ref, w_ref, wgates_ref, wdt_ref, gq_ref, gk_ref, hsum_ref = refs[:8]
    if fuse_ssd:
        ctx_ref, prm = refs[8], refs[9:16]
        q_out, k_out, v_out, gates_out, ktail_out, vtail_out, yb_out, st_out, ctx_out, state, heads = refs[16:]
    else:
        q_out, k_out, v_out, gates_out, ktail_out, vtail_out, z_out, xbc_out, dt_out = refs[8:]
    tile_in_batch = pl.program_id(0) % tiles_per_batch

    if fuse_ssd:
        @pl.when(tile_in_batch == 0)
        def _load_carry():
            _ssd_load_carry(ctx_ref, None, state, heads)

    x = x_ref[...]
    h = (x * _rms_scale(x) * g_ref[...]).astype(BF16)

    def seg(bounds):
        return jnp.dot(h, w_ref[:, bounds[0]:bounds[1]], preferred_element_type=F32)

    def head_norm(t, g):
        sq = (t * t).astype(BF16)
        parts = []
        for blk in range(ATT_WIDTH // MXU_DIM):
            cols = slice(blk * MXU_DIM, (blk + 1) * MXU_DIM)
            ss = jnp.dot(sq[:, cols], hsum_ref[...], preferred_element_type=F32)
            parts.append(t[:, cols] * lax.rsqrt(ss * (1.0 / HEAD_DIM_A) + EPS))
        return jnp.concatenate(parts, axis=-1) * g

    xbc = seg(_SEG_XBC)
    zb = seg(_SEG_Z).astype(BF16)
    dt_raw = jnp.dot(h, wdt_ref[...], preferred_element_type=F32)
    if fuse_ssd:
        xc = _ssd_conv(xbc, prm, heads, tm)
        acol, dtcol = _ssd_steps(dt_raw, prm, tm)
        for c in range(tm // CHUNK):
            rows = slice(c * CHUNK, (c + 1) * CHUNK)
            yb_out[rows, :] = _ssd_chunk(xc[rows, :], acol[rows, :], dtcol[rows, :], zb[rows, :], prm, state)
    else:
        xbc_out[...] = xbc
        z_out[...] = zb
        dt_out[...] = dt_raw

    q_out[...] = (head_norm(seg(_SEG_Q), gq_ref[...]) * (HEAD_DIM_A ** -0.5)).astype(BF16)
    k = head_norm(seg(_SEG_K), gk_ref[...])
    v = seg(_SEG_V)
    k_out[...] = k.astype(BF16)
    v_out[...] = v.astype(BF16)
    gates_out[...] = jnp.dot(h, wgates_ref[...], preferred_element_type=F32).astype(BF16)

    @pl.when(tile_in_batch >= tiles_per_batch - tail_tiles)
    def _emit_tail():
        ktail_out[0] = k.T
        vtail_out[0] = v.T

    if fuse_ssd:
        @pl.when(tile_in_batch == tiles_per_batch - 1)
        def _emit_carry():
            _ssd_emit_carry(st_out, ctx_out, state, heads)


def _inproj(x2d, w, tm, seq, ssm_ctx=None):
    t = x2d.shape[0]
    fuse_ssd = ssm_ctx is not None
    tiles_per_batch = max(seq // tm, 1)
    tail_tiles = max(min(ATT_PAST, seq) // tm, 1)
    n_tiles = t // tm
    n_batches = n_tiles // tiles_per_batch
    row = lambda width: pl.BlockSpec((tm, width), lambda i: (i, 0))
    first_tail = tiles_per_batch - tail_tiles
    tail = pl.BlockSpec((1, ATT_WIDTH, tm), lambda i: (
        i // tiles_per_batch, 0, jnp.maximum(i % tiles_per_batch - first_tail, 0)))
    per_batch = lambda r, width: pl.BlockSpec((1, r, width), lambda i: (i // tiles_per_batch, 0, 0))
    rows_out = lambda width, dtype: (row(width), jax.ShapeDtypeStruct((t, width), dtype))
    tail_out = (tail, jax.ShapeDtypeStruct((n_batches, ATT_WIDTH, tail_tiles * tm), F32))

    in_specs = [row(D_MODEL), _resident((1, D_MODEL)), _resident((D_MODEL, _SEG_XBC[1])),
                _resident(w["w_gates"].shape), _resident(w["w_dt"].shape),
                _resident((1, ATT_WIDTH)), _resident((1, ATT_WIDTH)), _resident((MXU_DIM, MXU_DIM))]
    args = [x2d, w["g_mix"], w["w_in"], w["w_gates"], w["w_dt"], w["g_q"], w["g_k"], w["hsum"]]
    outs = [rows_out(ATT_WIDTH, BF16), rows_out(ATT_WIDTH, BF16), rows_out(ATT_WIDTH, BF16),
            rows_out(2 * D_MODEL, BF16), tail_out, tail_out]
    scratch = []
    if fuse_ssd:
        assert seq % tm == 0 and tm % CHUNK == 0
        prm = _ssd_params(w)
        in_specs += [per_batch((SSM_CONV - 1) * SUBLANES, CONV_DIM)] + [_resident(p.shape) for p in prm]
        args += [_conv_heads(ssm_ctx, SSM_CONV)] + list(prm)
        outs += [rows_out(D_INNER, BF16),
                 (per_batch(D_INNER, D_STATE), jax.ShapeDtypeStruct((n_batches, D_INNER, D_STATE), F32)),
                 (per_batch(SUBLANES, CONV_DIM), jax.ShapeDtypeStruct((n_batches, SUBLANES, CONV_DIM), F32))]
        scratch = _ssd_scratch()
    else:
        outs += [rows_out(D_INNER, BF16), rows_out(CONV_DIM, F32), rows_out(DT_PAD, F32)]
    return pl.pallas_call(
        functools.partial(_inproj_kernel, tiles_per_batch=tiles_per_batch, tail_tiles=tail_tiles,
                          fuse_ssd=fuse_ssd, tm=tm),
        grid=(n_tiles,),
        in_specs=in_specs,
        out_specs=[o[0] for o in outs],
        out_shape=[o[1] for o in outs],
        scratch_shapes=scratch,
        compiler_params=_params(("arbitrary",)),
        name="inproj_ssd" if fuse_ssd else "inproj",
    )(*args)


HEADS_PER_LANE_TILE = LANES // HEAD_DIM_A


def _attn_kernel(*refs, has_cache, seq, qb, width):
    if has_cache:
        q_ref, k_ref, v_ref, bias_ref, ck_ref, cv_ref, o_ref, kp, vp = refs
    else:
        q_ref, k_ref, v_ref, bias_ref, o_ref, kp, vp = refs
    nk = ATT_PAST + qb

    if has_cache:
        kp[0:ATT_PAST, :] = ck_ref[0].T.astype(BF16)
        vp[0:ATT_PAST, :] = cv_ref[0].T.astype(BF16)
    kp[ATT_PAST:ATT_PAST + seq, :] = k_ref[0]
    vp[ATT_PAST:ATT_PAST + seq, :] = v_ref[0]

    lane = lax.broadcasted_iota(jnp.int32, (qb, LANES), 1)
    low_half = lane < HEAD_DIM_A

    def q_block(start, first_col):
        kb = kp[pl.ds(start + first_col, nk - first_col), :]
        vb = vp[pl.ds(start + first_col, nk - first_col), :]
        q = q_ref[0, pl.ds(start, qb), :]
        outs = []
        for pair in range(width // LANES):
            cols = slice(pair * LANES, (pair + 1) * LANES)
            qp, kpair, vpair = q[:, cols], kb[:, cols], vb[:, cols]
            zero = jnp.zeros((), BF16)
            q2 = jnp.concatenate([jnp.where(low_half, qp, zero), jnp.where(low_half, zero, qp)], axis=0)
            s = lax.dot_general(q2, kpair, (((1,), (1,)), ((), ())), preferred_element_type=F32)
            first_head = pair * HEADS_PER_LANE_TILE
            bias = bias_ref[first_head:first_head + HEADS_PER_LANE_TILE, :, first_col:]
            s = s + bias.reshape(HEADS_PER_LANE_TILE * qb, nk - first_col)
            m = jnp.max(s, axis=-1, keepdims=True)
            p = jnp.exp(s - m)
            denom = jnp.sum(p, axis=-1, keepdims=True)
            o2 = jnp.dot(p.astype(BF16), vpair, preferred_element_type=F32) / denom
            outs.append(jnp.where(low_half, o2[:qb], o2[qb:]))
        o_ref[0, pl.ds(start, qb), :] = jnp.concatenate(outs, axis=-1).astype(BF16)

    n_blocks = seq // qb
    n_peeled = 0 if has_cache else min(ATT_PAST // qb, n_blocks)
    for i in range(n_peeled):
        q_block(i * qb, ATT_PAST - i * qb)

    n_steady = n_blocks - n_peeled
    unroll = 6 if n_steady % 6 == 0 and n_steady > 0 else 1

    def steady(i, carry):
        for u in range(unroll):
            q_block(pl.multiple_of((n_peeled + i * unroll + u) * qb, qb), 0)
        return carry

    lax.fori_loop(0, n_steady // unroll, steady, 0)


def _attention(q, k, v, bias, cache, qb):
    b, seq, _ = q.shape
    nk = ATT_PAST + qb
    has_cache = cache is not None
    width = MXU_DIM if seq > ATT_PAST else ATT_WIDTH
    hb = ATT_WIDTH // width
    seq_spec = pl.BlockSpec((1, seq, width), lambda bi, hi: (bi, 0, hi))
    in_specs = [seq_spec, seq_spec, seq_spec,
                pl.BlockSpec((width // HEAD_DIM_A, qb, nk), lambda bi, hi: (hi, 0, 0))]
    args = [q, k, v, bias]
    if has_cache:
        cspec = pl.BlockSpec((1, width, ATT_PAST), lambda bi, hi: (bi, hi, 0))
        in_specs += [cspec, cspec]
        args += list(cache)
    return pl.pallas_call(
        functools.partial(_attn_kernel, has_cache=has_cache, seq=seq, qb=qb, width=width),
        grid=(b, hb),
        in_specs=in_specs,
        out_specs=seq_spec,
        out_shape=jax.ShapeDtypeStruct((b, seq, ATT_WIDTH), BF16),
        scratch_shapes=[pltpu.VMEM((ATT_PAST + seq, width), BF16),
                        pltpu.VMEM((ATT_PAST + seq, width), BF16)],
        compiler_params=_params(("parallel", "parallel")),
        name="band_attention",
    )(*args)


def _attention_bias(table, qb):
    nk = ATT_PAST + qb
    n_diag = qb + nk - 1
    diag_rel = (ATT_PAST + qb - 1) - jnp.arange(n_diag)
    r = table[:, jnp.clip(diag_rel, -REL_CLIP, REL_CLIP) + REL_CLIP]
    h = table.shape[0]
    row_len = 1 << n_diag.bit_length()
    rp = jnp.roll(jnp.pad(r, ((0, 0), (0, row_len - n_diag))), -(qb - 1), axis=1).reshape(h, 1, row_len)

    def bias_kernel(rp_ref, o_ref):
        rolled = pltpu.roll(jnp.broadcast_to(rp_ref[0], (qb, row_len)), 0, 1, stride=1, stride_axis=0)
        qc = lax.broadcasted_iota(jnp.int32, (qb, nk), 0) // CHUNK
        kc = lax.broadcasted_iota(jnp.int32, (qb, nk), 1) // CHUNK
        visible = jnp.logical_and(kc >= qc, kc <= qc + ATT_PAST // CHUNK)
        o_ref[0] = jnp.where(visible, rolled[:, :nk], NEG_INF)

    return pl.pallas_call(
        bias_kernel,
        grid=(h,),
        in_specs=[pl.BlockSpec((1, 1, row_len), lambda hi: (hi, 0, 0))],
        out_specs=pl.BlockSpec((1, qb, nk), lambda hi: (hi, 0, 0)),
        out_shape=jax.ShapeDtypeStruct((h, qb, nk), F32),
        compiler_params=_params(("parallel",)),
        name="attention_bias",
    )(rp.astype(F32))


def _ffn_kernel(x_ref, ya_ref, yb_ref, gates_ref, pe_ref, ctx_ref, bg_ref, wa_ref, wb_ref, wo_ref,
                gffn_ref, wup_ref, cw_ref, cb_ref, wdown_ref, gple_ref, wgate_ref, wple_ref,
                o_ref, ctx_out, heads, *, tm, seq_rows, tiles_per_batch):
    i = pl.program_id(0)
    n_seqs = tm // seq_rows

    @pl.when(i % tiles_per_batch == 0)
    def _load_context():
        for k in range(FFN_CONV - 1):
            heads[k] = ctx_ref[:, k * SUBLANES:(k + 1) * SUBLANES, :]

    mix_gate = jax.nn.sigmoid(gates_ref[...].astype(F32) + bg_ref[...])
    mixed = (mix_gate[:, :D_MODEL] * jnp.dot(ya_ref[...], wa_ref[...], preferred_element_type=F32)
             + mix_gate[:, D_MODEL:] * jnp.dot(yb_ref[...], wb_ref[...], preferred_element_type=F32))
    x = x_ref[...] + jnp.dot(mixed.astype(BF16), wo_ref[...], preferred_element_type=F32)
    xn = (x * _rms_scale(x) * gffn_ref[...]).astype(BF16)
    n_slabs = tm // SUBLANES
    row_in_slab = lax.broadcasted_iota(jnp.int32, (tm, FFN_COL_BLOCK), 0) % SUBLANES

    def up(j, part):
        lo = part * D_FF + j * FFN_COL_BLOCK
        return jnp.dot(xn, wup_ref[:, lo:lo + FFN_COL_BLOCK], preferred_element_type=F32)

    def conv(u, j, part):
        lo = part * D_FF + j * FFN_COL_BLOCK
        cols = slice(lo, lo + FFN_COL_BLOCK)
        out = cb_ref[:, cols] + u * cw_ref[FFN_CONV - 1:FFN_CONV, cols]
        for k in range(1, FFN_CONV):
            rot = pltpu.roll(u.reshape(n_slabs, SUBLANES, FFN_COL_BLOCK), k, 1).reshape(tm, FFN_COL_BLOCK)
            pieces = []
            for s in range(n_seqs):
                pieces += [heads[k - 1, s, :, cols], rot[s * seq_rows:(s + 1) * seq_rows - SUBLANES]]
            shifted = jnp.where(row_in_slab < k, jnp.concatenate(pieces, axis=0), rot)
            out = out + shifted * cw_ref[FFN_CONV - 1 - k:FFN_CONV - k, cols]
            for s in range(n_seqs):
                heads[k - 1, s, :, cols] = rot[(s + 1) * seq_rows - SUBLANES:(s + 1) * seq_rows]
        return out

    n_blocks = D_FF // FFN_COL_BLOCK
    hidden_blocks = []
    pending = (up(0, 0), up(0, 1))
    for j in range(n_blocks):
        ug, uv = pending
        if j + 1 < n_blocks:
            pending = (up(j + 1, 0), up(j + 1, 1))
        g = conv(ug, j, 0)
        gate_arg = g * (GELU_C0 + GELU_C1 * (g * g))
        hidden_blocks.append((g * (1.0 + jnp.tanh(gate_arg)) * conv(uv, j, 1)).astype(BF16))
    acc = jnp.dot(jnp.concatenate(hidden_blocks, axis=-1), wdown_ref[...], preferred_element_type=F32)

    @pl.when(i % tiles_per_batch == tiles_per_batch - 1)
    def _emit_context():
        ctx_out[...] = heads[FFN_CONV - 2]

    x2 = x + acc
    gate = jax.nn.sigmoid(jnp.dot((x2 * _rms_scale(x2) * gple_ref[...]).astype(BF16), wgate_ref[...],
                                  preferred_element_type=F32))
    o_ref[...] = x2 + gate * jnp.dot(pe_ref[...].astype(BF16), wple_ref[...], preferred_element_type=F32)


def _merge_ffn(x2d, ya, yb, gates, pe2d, ctx, prm, seq, tm):
    t = x2d.shape[0]
    batch = t // seq
    seq_rows = min(seq, tm)
    seqs_per_tile = tm // seq_rows
    tiles_per_batch = seq // seq_rows
    row = lambda w: pl.BlockSpec((tm, w), lambda i: (i, 0))
    per_seq = lambda r: pl.BlockSpec((seqs_per_tile, r, 2 * D_FF), lambda i: (i // tiles_per_batch, 0, 0))
    y, ctx_out = pl.pallas_call(
        functools.partial(_ffn_kernel, tm=tm, seq_rows=seq_rows, tiles_per_batch=tiles_per_batch),
        grid=(t // tm,),
        in_specs=[row(D_MODEL), row(ATT_WIDTH), row(D_INNER), row(2 * D_MODEL), row(PLE_DIM),
                  per_seq((FFN_CONV - 1) * SUBLANES)] + [_resident(p.shape) for p in prm],
        out_specs=[row(D_MODEL), per_seq(SUBLANES)],
        out_shape=[jax.ShapeDtypeStruct((t, D_MODEL), F32),
                   jax.ShapeDtypeStruct((batch, SUBLANES, 2 * D_FF), F32)],
        scratch_shapes=[pltpu.VMEM((FFN_CONV - 1, seqs_per_tile, SUBLANES, 2 * D_FF), F32)],
        compiler_params=_params(("arbitrary",)),
        name="merge_convffn_ple",
    )(x2d, ya, yb, gates, pe2d, _conv_heads(ctx, FFN_CONV), *prm)
    return y, ctx_out[:, :FFN_CONV - 1]


def _ssd_params(w):
    return (w["conv_ssm_w"], w["conv_ssm_b"], w["dt_bias"], w["a_log"], w["d_skip"], w["g_ssm"], w["expand"])


def _layer(x, pe, kv_cache, ssm_ctx, h0, ffn_ctx, w, tm, tq, tm_ffn):
    b, seq, _ = x.shape
    t = b * seq
    x2d = x.reshape(t, D_MODEL)
    as3 = lambda a: a.reshape(b, seq, -1)
    fuse_ssd = h0 is None and seq % tm == 0 and tm % CHUNK == 0
    if fuse_ssd:
        qn, kn, v, gates, k_tail, v_tail, yb, st, ctx_ssm = _inproj(x2d, w, tm, seq, ssm_ctx)
    else:
        qn, kn, v, gates, k_tail, v_tail, z, xbc, dt_raw = _inproj(x2d, w, tm, seq)
        seq_pad = -(-seq // CHUNK) * CHUNK
        valid_rows = CHUNK - (seq_pad - seq)
        pad = lambda a: jnp.pad(as3(a), ((0, 0), (0, seq_pad - seq), (0, 0)))
        yb, st, ctx_ssm = _ssd(pad(xbc), pad(dt_raw), pad(z), ssm_ctx, h0, _ssd_params(w), valid_rows)
        yb = yb[:, :seq].reshape(t, D_INNER)

    cache = None
    if kv_cache is not None:
        cache = tuple(jnp.transpose(c, (0, 2, 3, 1)).reshape(b, ATT_WIDTH, ATT_PAST) for c in kv_cache)
    ya = _attention(as3(qn), as3(kn), as3(v), _attention_bias(w["rel_bias"], tq), cache, tq)

    ffn_prm = (w["b_gate"], w["w_proj_a"], w["w_proj_b"], w["w_out"],
               w["g_ffn"], w["w_up"], w["ffn_conv_w"], w["ffn_conv_b"], w["w_down"],
               w["g_ple"], w["w_ple_gate"], w["w_ple"])
    y, ctx_ffn = _merge_ffn(x2d, ya.reshape(t, ATT_WIDTH), yb, gates, pe.reshape(t, PLE_DIM), ffn_ctx, ffn_prm,
                            seq, tm_ffn)

    def heads(tail_t):
        if seq >= tm:
            return jnp.transpose(tail_t.reshape(b, N_HEADS_A, HEAD_DIM_A, -1), (0, 3, 1, 2))
        return jnp.transpose(tail_t.reshape(N_HEADS_A, HEAD_DIM_A, b, seq), (2, 3, 0, 1))

    state = st.reshape(b, N_HEADS_S, SSM_HEAD_DIM, D_STATE)
    return y.reshape(b, seq, D_MODEL), (heads(k_tail), heads(v_tail), state, ctx_ssm[:, :SSM_CONV - 1], ctx_ffn)


def _prep_weights(i, g_mix, w_in, b_gate, g_q, g_k, rel_bias, conv_ssm_w, conv_ssm_b, dt_bias, a_log, d_skip,
                  g_ssm, w_proj_a, w_proj_b, w_out, g_ffn, w_up, ffn_conv_w, ffn_conv_b, w_down, g_ple,
                  w_ple_gate, w_ple):
    row = lambda a: a[i].reshape(1, -1).astype(F32)
    pad_lanes = lambda a, n: jnp.pad(a, ((0, 0), (0, n - a.shape[1])))
    wi = w_in[i].astype(BF16)
    head_of_lane = jnp.arange(D_INNER) // SSM_HEAD_DIM
    expand1 = (jnp.arange(DT_PAD)[:, None] == head_of_lane[None, :]).astype(BF16)
    blk = jnp.arange(MXU_DIM) // HEAD_DIM_A
    half_on_values = jnp.where(jnp.arange(2 * D_FF) < D_FF, 1.0, 0.5).astype(F32)[None, :]
    return {
        "g_mix": row(g_mix), "w_in": wi, "w_gates": wi[:, _SEG_GATES[0]:_SEG_GATES[1]],
        "w_dt": pad_lanes(wi[:, _SEG_DT[0]:_SEG_DT[1]], DT_PAD), "b_gate": row(b_gate),
        "g_q": jnp.tile(row(g_q), (1, N_HEADS_A)), "g_k": jnp.tile(row(g_k), (1, N_HEADS_A)),
        "hsum": (blk[:, None] == blk[None, :]).astype(BF16),
        "rel_bias": rel_bias[i].astype(F32),
        "conv_ssm_w": jnp.pad(conv_ssm_w[i], ((0, SUBLANES - SSM_CONV), (0, 0))), "conv_ssm_b": row(conv_ssm_b),
        "dt_bias": pad_lanes(row(dt_bias), DT_PAD), "a_log": pad_lanes(row(a_log), DT_PAD),
        "d_skip": jnp.repeat(row(d_skip), SSM_HEAD_DIM, axis=1), "g_ssm": row(g_ssm),
        "expand": jnp.concatenate([expand1] * 3, axis=0),
        "w_proj_a": w_proj_a[i].astype(BF16), "w_proj_b": w_proj_b[i].astype(BF16), "w_out": w_out[i].astype(BF16),
        "g_ffn": row(g_ffn), "w_up": w_up[i].astype(BF16),
        "ffn_conv_w": jnp.pad(ffn_conv_w[i] * half_on_values, ((0, SUBLANES - FFN_CONV), (0, 0))),
        "ffn_conv_b": row(ffn_conv_b) * half_on_values,
        "w_down": w_down[i].astype(BF16), "g_ple": row(g_ple),
        "w_ple_gate": w_ple_gate[i].astype(BF16), "w_ple": w_ple[i].astype(BF16),
    }


def kernel(x_prompt, x_sample, cache_k, cache_v, state_ssm, state_conv_ssm, state_conv_ffn, p_prompt, p_sample, g_mix, w_in, b_gate, g_q, g_k, rel_bias, conv_ssm_w, conv_ssm_b, dt_bias, a_log, d_skip, g_ssm, w_proj_a, w_proj_b, w_out, g_ffn, w_up, ffn_conv_w, ffn_conv_b, w_down, g_ple, w_ple_gate, w_ple):
    depth = w_in.shape[0]
    bp = x_prompt.shape[0]
    bs, seq_s = x_sample.shape[:2]
    yp, ys = x_prompt, x_sample
    sp, ss = [], []
    for i in range(depth):
        w = _prep_weights(i, g_mix, w_in, b_gate, g_q, g_k, rel_bias, conv_ssm_w, conv_ssm_b, dt_bias, a_log,
                          d_skip, g_ssm, w_proj_a, w_proj_b, w_out, g_ffn, w_up, ffn_conv_w, ffn_conv_b, w_down,
                          g_ple, w_ple_gate, w_ple)
        yp, st_p = _layer(yp, p_prompt[i], None,
                          jnp.zeros((bp, SSM_CONV - 1, CONV_DIM), F32), None,
                          jnp.zeros((bp, FFN_CONV - 1, 2 * D_FF), F32), w, tm=256, tq=4 * CHUNK, tm_ffn=256)
        h0 = state_ssm[i].reshape(bs, D_INNER, D_STATE)
        ys, st_s = _layer(ys, p_sample[i], (cache_k[i], cache_v[i]), state_conv_ssm[i], h0,
                          state_conv_ffn[i], w, tm=seq_s * bs, tq=seq_s, tm_ffn=seq_s * bs)
        sp.append(st_p)
        ss.append(st_s)

    stack = lambda lst, j: jnp.stack([s[j] for s in lst], axis=0)
    return (yp, ys,
            stack(sp, 0), stack(sp, 1), stack(sp, 2), stack(sp, 3), stack(sp, 4),
            stack(ss, 0), stack(ss, 1), stack(ss, 2), stack(ss, 3), stack(ss, 4))
```
